```python
import math
import jax, jax.numpy as jnp
from jax import lax
import numpy as np

D_MODEL = 1024
BATCH = 2
SEQ = 8192
DEPTH = 4
DEC_BATCH = 128
DEC_SEQ = 8
PAST_LEN = 2048
PAGE_SIZE = 128

A_HEADS = 4
A_DH = 64
A_DV = 2 * A_DH
A_WIDTH = A_HEADS * A_DV
B_HEADS = 4
B_DK = 128
B_DV = 128
B_WIDTH = B_HEADS * B_DV
CONV_W = 4
CONV_CH = 2 * B_HEADS * B_DK + B_HEADS * B_DV
CHUNK = 64
X_HEADS = 4
X_DH = 128
X_WIDTH = X_HEADS * X_DH
N_MEM = 256
MIX_WIDTH = A_WIDTH + B_WIDTH + X_WIDTH
IN_SPLITS_W = (A_HEADS * 2 * A_DH, A_HEADS * 2 * A_DH, A_WIDTH, A_WIDTH,
               B_HEADS * B_DK, B_HEADS * B_DK, B_WIDTH, B_WIDTH, B_HEADS, B_HEADS,
               X_WIDTH, X_WIDTH)
D_IN = sum(IN_SPLITS_W)
IN_SPLITS = tuple(sum(IN_SPLITS_W[:i + 1]) for i in range(len(IN_SPLITS_W) - 1))
Q_BLOCK = 128
NEG_INF = -1e30
EPS = 1e-6

kernel_name = "hybrid_diffattn_gdn_memx_step"


def rms_norm(x, w):
    xf = x.astype(jnp.float32)
    y = xf * lax.rsqrt(jnp.mean(xf * xf, axis=-1, keepdims=True) + EPS)
    return (y * w.astype(jnp.float32)).astype(x.dtype)


def l2_normalize(x):
    return x * lax.rsqrt(jnp.sum(x * x, axis=-1, keepdims=True) + EPS)


def alibi_slopes():
    return jnp.asarray(np.array([2.0 ** (-8.0 * (h + 1) / A_HEADS) for h in range(A_HEADS)], np.float32))


def diff_lambda_value(p, lam_init):
    p = p.astype(jnp.float32)
    return jnp.exp(jnp.sum(p[0] * p[1])) - jnp.exp(jnp.sum(p[2] * p[3])) + lam_init


def diff_attention_block(q, k, v, q_pos, k_pos, lam):
    s = jnp.einsum("bqhmd,bkhmd->bhmqk", q, k) * (A_DH ** -0.5)
    dist = (q_pos[:, None] - k_pos[None, :]).astype(jnp.float32)
    s = s - alibi_slopes()[:, None, None, None] * dist
    s = jnp.where(dist >= 0, s, NEG_INF)
    p = jax.nn.softmax(s, axis=-1)
    a = p[:, :, 0] - lam * p[:, :, 1]
    return jnp.einsum("bhqk,bkhd->bqhd", a, v)


def diff_attention(q, k, v, q_pos, k_pos, lam):
    B, Lq = q.shape[0], q.shape[1]
    if Lq <= Q_BLOCK or Lq % Q_BLOCK:
        return diff_attention_block(q, k, v, q_pos, k_pos, lam)
    nb = Lq // Q_BLOCK
    qb = jnp.swapaxes(q.reshape((B, nb, Q_BLOCK) + q.shape[2:]), 0, 1)
    pb = q_pos.reshape(nb, Q_BLOCK)
    out = lax.map(lambda qp: diff_attention_block(qp[0], k, v, qp[1], k_pos, lam), (qb, pb))
    return jnp.swapaxes(out, 0, 1).reshape(B, Lq, A_HEADS, A_DV)


def causal_conv(x, buf, w):
    L = x.shape[1]
    xp = jnp.concatenate([buf.astype(x.dtype), x], axis=1)
    y = xp[:, 0:L] * w[0]
    for j in range(1, CONV_W):
        y = y + xp[:, j:j + L] * w[j]
    return jax.nn.silu(y), xp[:, L:]


def gated_delta_rule(q, k, v, g, beta, S0):
    B, L, H, DK = q.shape
    DV = v.shape[-1]
    C = min(CHUNK, L)
    pad = (-L) % C
    n = (L + pad) // C

    def chunks(t):
        t = jnp.pad(t, [(0, 0), (0, pad)] + [(0, 0)] * (t.ndim - 2))
        t = t.reshape((B, n, C) + t.shape[2:])
        return jnp.moveaxis(t, (1, 3), (0, 2))

    qc, kc, vc, gc, bc = chunks(q), chunks(k), chunks(v), chunks(g), chunks(beta)
    gcum = jnp.cumsum(gc, axis=-1)
    idx = jnp.arange(C)
    incl = idx[:, None] >= idx[None, :]
    strict = idx[:, None] > idx[None, :]
    gdiff = gcum[..., :, None] - gcum[..., None, :]
    decay = jnp.where(incl, jnp.exp(jnp.where(incl, gdiff, 0.0)), 0.0)
    kb = kc * bc[..., None]
    m = jnp.where(strict, jnp.einsum("nbhid,nbhjd->nbhij", kb, kc) * decay, 0.0)
    a = m + jnp.eye(C, dtype=m.dtype)
    rhs = jnp.concatenate([vc * bc[..., None], kb * jnp.exp(gcum)[..., None]], axis=-1)
    sol = lax.linalg.triangular_solve(a, rhs, left_side=True, lower=True, unit_diagonal=True)
    u, w = sol[..., :DV], sol[..., DV:]
    qk = jnp.einsum("nbhid,nbhjd->nbhij", qc, kc) * decay

    def step(S, inp):
        q_c, k_c, u_c, w_c, g_c, qk_c = inp
        v_new = u_c - jnp.einsum("bhck,bhkv->bhcv", w_c, S)
        o = (jnp.einsum("bhck,bhkv->bhcv", q_c * jnp.exp(g_c)[..., None], S)
             + jnp.einsum("bhij,bhjv->bhiv", qk_c, v_new))
        g_last = g_c[..., -1]
        k_dec = k_c * jnp.exp(g_last[..., None] - g_c)[..., None]
        S = S * jnp.exp(g_last)[..., None, None] + jnp.einsum("bhck,bhcv->bhkv", k_dec, v_new)
        return S, o

    S, o = lax.scan(step, S0, (qc, kc, u, w, gcum, qk))
    o = jnp.moveaxis(o, (0, 2), (1, 3)).reshape(B, n * C, H, DV)[:, :L]
    return o, S


def cross_attention(q, mk, mv):
    s = jnp.einsum("blhd,bmhd->bhlm", q, mk) * (X_DH ** -0.5)
    p = jax.nn.softmax(s, axis=-1)
    return jnp.einsum("bhlm,bmhd->blhd", p, mv)


def mixer_layer(h, norm_w, w_in, lam, lam_scale, diff_norm_w, conv_w, a_log, dt_bias, delta_norm_w, w_out,
                past_k, past_v, q_pos, k_pos, S0, conv0, mem_k, mem_v):
    f32 = jnp.float32
    B, L, _ = h.shape
    hn = rms_norm(h, norm_w)
    proj = hn @ w_in
    aq, ak, av, ag, bq, bk, bv, bg, ba, bb, xq, xg = jnp.split(proj, IN_SPLITS, axis=-1)

    k_rows = ak.reshape(B, L, A_HEADS, 2 * A_DH)
    v_rows = av.reshape(B, L, A_HEADS, A_DV)
    if past_k is None:
        k_all, v_all = k_rows, v_rows
    else:
        k_all = jnp.concatenate([past_k.astype(k_rows.dtype), k_rows], axis=1)
        v_all = jnp.concatenate([past_v.astype(v_rows.dtype), v_rows], axis=1)
    Lk = k_all.shape[1]
    o_a = diff_attention(aq.reshape(B, L, A_HEADS, 2, A_DH).astype(f32),
                         k_all.reshape(B, Lk, A_HEADS, 2, A_DH).astype(f32),
                         v_all.astype(f32), q_pos, k_pos, lam)
    o_a = rms_norm(o_a, diff_norm_w) * lam_scale
    o_a = o_a.reshape(B, L, A_WIDTH) * jax.nn.silu(ag.astype(f32))

    qkv, conv_new = causal_conv(jnp.concatenate([bq, bk, bv], axis=-1), conv0, conv_w)
    dq, dk, dv = jnp.split(qkv.astype(f32), [B_HEADS * B_DK, 2 * B_HEADS * B_DK], axis=-1)
    dq = l2_normalize(dq.reshape(B, L, B_HEADS, B_DK)) * (B_DK ** -0.5)
    dk = l2_normalize(dk.reshape(B, L, B_HEADS, B_DK))
    dv = dv.reshape(B, L, B_HEADS, B_DV)
    beta = jax.nn.sigmoid(bb.astype(f32))
    g = -jnp.exp(a_log.astype(f32)) * jax.nn.softplus(ba.astype(f32) + dt_bias.astype(f32))
    o_b, S_new = gated_delta_rule(dq, dk, dv, g, beta, S0.astype(f32))
    o_b = rms_norm(o_b, delta_norm_w) * jax.nn.silu(bg.astype(f32).reshape(B, L, B_HEADS, B_DV))
    o_b = o_b.reshape(B, L, B_WIDTH)

    o_x = cross_attention(xq.reshape(B, L, X_HEADS, X_DH).astype(f32), mem_k.astype(f32), mem_v.astype(f32))
    o_x = o_x.reshape(B, L, X_WIDTH) * jax.nn.silu(xg.astype(f32))

    mix = jnp.concatenate([o_a, o_b, o_x], axis=-1).astype(h.dtype)
    h = h + mix @ w_out
    return h, k_rows, v_rows, S_new, conv_new


def setup_inputs(seed: int = 0) -> dict:
    key = jax.random.key(seed)
    ks = jax.random.split(key, 24)
    nrm = jax.random.normal
    n_pages = PAST_LEN // PAGE_SIZE
    n_used = DEC_BATCH * n_pages
    n_phys = (n_used * 5) // 4
    x_prompt = nrm(ks[0], (BATCH, SEQ, D_MODEL), jnp.float32)
    x_sample = nrm(ks[1], (DEC_BATCH, DEC_SEQ, D_MODEL), jnp.float32)
    cache_k = nrm(ks[2], (DEPTH, n_phys, PAGE_SIZE, A_HEADS, 2 * A_DH), jnp.float32)
    cache_v = nrm(ks[3], (DEPTH, n_phys, PAGE_SIZE, A_HEADS, A_DV), jnp.float32)
    state_delta = 0.5 * nrm(ks[4], (DEPTH, DEC_BATCH, B_HEADS, B_DK, B_DV), jnp.float32)
    state_conv = nrm(ks[5], (DEPTH, DEC_BATCH, CONV_W - 1, CONV_CH), jnp.float32)
    cache_mem_k = nrm(ks[6], (DEPTH, DEC_BATCH, N_MEM, X_HEADS, X_DH), jnp.float32)
    cache_mem_v = nrm(ks[7], (DEPTH, DEC_BATCH, N_MEM, X_HEADS, X_DH), jnp.float32)
    page_table = jax.random.permutation(ks[8], n_phys)[:n_used].reshape(DEC_BATCH, n_pages).astype(jnp.int32)
    mem_prompt = nrm(ks[9], (BATCH, N_MEM, D_MODEL), jnp.float32)
    norm_w = 1.0 + 0.02 * nrm(ks[10], (DEPTH, D_MODEL), jnp.float32)
    w_in = nrm(ks[11], (DEPTH, D_MODEL, D_IN), jnp.float32) * (D_MODEL ** -0.5)
    diff_lambda = 0.1 * nrm(ks[12], (DEPTH, 4, A_DH), jnp.float32)
    diff_norm_w = 1.0 + 0.02 * nrm(ks[13], (DEPTH, A_DV), jnp.float32)
    conv_w = nrm(ks[14], (DEPTH, CONV_W, CONV_CH), jnp.float32) * (CONV_W ** -0.5)
    a_log = jnp.log(jax.random.uniform(ks[15], (DEPTH, B_HEADS), jnp.float32, 1.0, 16.0))
    dt = jnp.exp(jax.random.uniform(ks[16], (DEPTH, B_HEADS), jnp.float32, math.log(1e-3), math.log(0.1)))
    dt_bias = dt + jnp.log(-jnp.expm1(-dt))
    delta_norm_w = 1.0 + 0.02 * nrm(ks[17], (DEPTH, B_DV), jnp.float32)
    norm_mem = 1.0 + 0.02 * nrm(ks[18], (DEPTH, D_MODEL), jnp.float32)
    w_mem_kv = nrm(ks[19], (DEPTH, D_MODEL, 2 * X_WIDTH), jnp.float32) * (D_MODEL ** -0.5)
    w_out = nrm(ks[20], (DEPTH, MIX_WIDTH, D_MODEL), jnp.float32) * (MIX_WIDTH ** -0.5)
    norm_f = 1.0 + 0.02 * nrm(ks[21], (D_MODEL,), jnp.float32)
    return {"x_prompt": x_prompt, "x_sample": x_sample, "cache_k": cache_k, "cache_v": cache_v,
            "state_delta": state_delta, "state_conv": state_conv, "cache_mem_k": cache_mem_k,
            "cache_mem_v": cache_mem_v, "page_table": page_table, "mem_prompt": mem_prompt,
            "norm_w": norm_w, "w_in": w_in, "diff_lambda": diff_lambda, "diff_norm_w": diff_norm_w,
            "conv_w": conv_w, "a_log": a_log, "dt_bias": dt_bias, "delta_norm_w": delta_norm_w,
            "norm_mem": norm_mem, "w_mem_kv": w_mem_kv, "w_out": w_out, "norm_f": norm_f}


def reference(x_prompt, x_sample, cache_k, cache_v, state_delta, state_conv, cache_mem_k, cache_mem_v,
              page_table, mem_prompt, norm_w, w_in, diff_lambda, diff_norm_w, conv_w, a_log, dt_bias,
              delta_norm_w, norm_mem, w_mem_kv, w_out, norm_f):
    Bp, Lp, _ = x_prompt.shape
    Bs, Ls, _ = x_sample.shape
    n_mem = mem_prompt.shape[1]
    past = page_table.shape[1] * PAGE_SIZE
    pos_p = jnp.arange(Lp)
    pos_sq = past + jnp.arange(Ls)
    pos_sk = jnp.arange(past + Ls)
    hp, hs = x_prompt, x_sample
    kp_l, vp_l, Sp_l, cp_l, mkp_l, mvp_l = [], [], [], [], [], []
    ks_l, vs_l, Ss_l, cs_l = [], [], [], []
    for l in range(DEPTH):
        lam_init = 0.8 - 0.6 * math.exp(-0.3 * l)
        lam = diff_lambda_value(diff_lambda[l], lam_init)
        lam_scale = 1.0 - lam_init
        mkv = rms_norm(mem_prompt, norm_mem[l]) @ w_mem_kv[l]
        mk_p = mkv[..., :X_WIDTH].reshape(Bp, n_mem, X_HEADS, X_DH)
        mv_p = mkv[..., X_WIDTH:].reshape(Bp, n_mem, X_HEADS, X_DH)
        S0_p = jnp.zeros((Bp, B_HEADS, B_DK, B_DV), jnp.float32)
        c0_p = jnp.zeros((Bp, CONV_W - 1, CONV_CH), x_prompt.dtype)
        hp, kp, vp, Sp, cp = mixer_layer(hp, norm_w[l], w_in[l], lam, lam_scale, diff_norm_w[l], conv_w[l],
                                         a_log[l], dt_bias[l], delta_norm_w[l], w_out[l],
                                         None, None, pos_p, pos_p, S0_p, c0_p, mk_p, mv_p)
        kp_l.append(kp); vp_l.append(vp); Sp_l.append(Sp.astype(x_prompt.dtype)); cp_l.append(cp)
        mkp_l.append(mk_p); mvp_l.append(mv_p)
        pk = cache_k[l, page_table].reshape(Bs, past, A_HEADS, 2 * A_DH)
        pv = cache_v[l, page_table].reshape(Bs, past, A_HEADS, A_DV)
        hs, kk, vv, Ss, cs = mixer_layer(hs, norm_w[l], w_in[l], lam, lam_scale, diff_norm_w[l], conv_w[l],
                                         a_log[l], dt_bias[l], delta_norm_w[l], w_out[l],
                                         pk, pv, pos_sq, pos_sk, state_delta[l], state_conv[l],
                                         cache_mem_k[l], cache_mem_v[l])
        ks_l.append(kk); vs_l.append(vv); Ss_l.append(Ss.astype(state_delta.dtype)); cs_l.append(cs)
    y_prompt = rms_norm(hp, norm_f)
    y_sample = rms_norm(hs, norm_f)
    new_k_prompt = jnp.stack(kp_l)
    new_v_prompt = jnp.stack(vp_l)
    new_delta_prompt = jnp.stack(Sp_l)
    new_conv_prompt = jnp.stack(cp_l)
    new_mem_k_prompt = jnp.stack(mkp_l)
    new_mem_v_prompt = jnp.stack(mvp_l)
    new_k_sample = jnp.stack(ks_l)
    new_v_sample = jnp.stack(vs_l)
    new_delta_sample = jnp.stack(Ss_l)
    new_conv_sample = jnp.stack(cs_l)
    return (y_prompt, y_sample, new_k_prompt, new_v_prompt, new_delta_prompt, new_conv_prompt,
            new_mem_k_prompt, new_mem_v_prompt, new_k_sample, new_v_sample, new_delta_sample, new_conv_sample)
```

```python
import functools
import math

import jax
import jax.numpy as jnp
from jax import lax
from jax.experimental import pallas as pl
from jax.experimental.pallas import tpu as pltpu

F32 = jnp.float32
BF16 = jnp.bfloat16

D_MODEL = 1024
N_HEADS = 4
HEAD_W = 128
GROUP_W = N_HEADS * HEAD_W
A_DH = 64
CONV_W = 4
CHUNK = 64
PAGE = 128
EPS = 1e-6
NEG_INF = -1e30
N_MAIN = 10 * GROUP_W
LANES = 128
VMEM_LIMIT = 48 * 1024 * 1024

SEG_AQ, SEG_AK, SEG_AV, SEG_AG, SEG_BQ, SEG_BK, SEG_BV, SEG_BG, SEG_XQ, SEG_XG = range(10)


def _cparams(sem):
    return pltpu.CompilerParams(dimension_semantics=sem, vmem_limit_bytes=VMEM_LIMIT)


def _iota(shape, dim):
    return lax.broadcasted_iota(jnp.int32, shape, dim)


def _div(x, d):
    assert d & (d - 1) == 0
    return x >> (d.bit_length() - 1)


def _sigmoid(x):
    return 1.0 / (1.0 + jnp.exp(-x))


def _silu(x):
    return x * _sigmoid(x)


def _softplus(x):
    return jnp.maximum(x, 0.0) + jnp.log(1.0 + jnp.exp(-jnp.abs(x)))


def _dot(a, b):
    return jnp.dot(a, b, preferred_element_type=F32)


def _dot_nt(a, b):
    return lax.dot_general(a, b, (((1,), (1,)), ((), ())), preferred_element_type=F32)


def _rms(x, w):
    ms = jnp.mean(x * x, axis=-1, keepdims=True)
    return x * lax.rsqrt(ms + EPS) * w


def _lambda_value(p, lam_init):
    a = jnp.sum(p[0:1, :] * p[1:2, :], axis=-1, keepdims=True)
    b = jnp.sum(p[2:3, :] * p[3:4, :], axis=-1, keepdims=True)
    return jnp.exp(a) - jnp.exp(b) + lam_init


def _proj_kernel(x_ref, nw_ref, w_ref, wg_ref, p_ref, g_ref, hn_ref):
    @pl.when(pl.program_id(1) == 0)
    def _():
        hn = _rms(x_ref[...], nw_ref[...]).astype(BF16)
        hn_ref[...] = hn
        g_ref[...] = _dot(hn, wg_ref[...])

    p_ref[...] = _dot(hn_ref[...], w_ref[...])


def _proj(x2d, norm_w, w_main, w_gate):
    t = x2d.shape[0]
    tm = min(t, 1024)
    tn = 1024
    return pl.pallas_call(
        _proj_kernel,
        grid=(t // tm, N_MAIN // tn),
        in_specs=[
            pl.BlockSpec((tm, D_MODEL), lambda i, j: (i, 0)),
            pl.BlockSpec((1, D_MODEL), lambda i, j: (0, 0)),
            pl.BlockSpec((D_MODEL, tn), lambda i, j: (0, j)),
            pl.BlockSpec((D_MODEL, LANES), lambda i, j: (0, 0)),
        ],
        out_specs=[
            pl.BlockSpec((tm, tn), lambda i, j: (i, j)),
            pl.BlockSpec((tm, LANES), lambda i, j: (i, 0)),
        ],
        out_shape=[jax.ShapeDtypeStruct((t, N_MAIN), F32), jax.ShapeDtypeStruct((t, LANES), F32)],
        scratch_shapes=[pltpu.VMEM((tm, D_MODEL), BF16)],
        compiler_params=_cparams(("parallel", "arbitrary")),
        name="proj",
    )(x2d, norm_w.reshape(1, D_MODEL), w_main, w_gate)


def _memkv_kernel(x_ref, nw_ref, w_ref, k_ref, v_ref):
    hn = _rms(x_ref[...], nw_ref[...]).astype(BF16)
    kv = _dot(hn, w_ref[...])
    k_ref[...] = kv[:, :GROUP_W]
    v_ref[...] = kv[:, GROUP_W:]


def _memkv(mem2d, norm_mem, w_mem_kv_bf16):
    depth = norm_mem.shape[0]
    t = mem2d.shape[0]
    return pl.pallas_call(
        _memkv_kernel,
        grid=(depth,),
        in_specs=[
            pl.BlockSpec((t, D_MODEL), lambda l: (0, 0)),
            pl.BlockSpec((None, 1, D_MODEL), lambda l: (l, 0, 0)),
            pl.BlockSpec((None, D_MODEL, 2 * GROUP_W), lambda l: (l, 0, 0)),
        ],
        out_specs=[
            pl.BlockSpec((None, t, GROUP_W), lambda l: (l, 0, 0)),
            pl.BlockSpec((None, t, GROUP_W), lambda l: (l, 0, 0)),
        ],
        out_shape=[jax.ShapeDtypeStruct((depth, t, GROUP_W), F32)] * 2,
        compiler_params=_cparams(("parallel",)),
        name="memkv",
    )(mem2d, norm_mem.reshape(depth, 1, D_MODEL), w_mem_kv_bf16)


def _alibi_slope(h):
    return jnp.where(h == 0, 0.25, jnp.where(h == 1, 0.0625, jnp.where(h == 2, 0.015625, 0.00390625))).astype(F32)


def _diff_epilogue(o1, o2, lam, dnw, lam_scale, gate):
    o = o1 - lam * o2
    return _rms(o, dnw) * lam_scale * _silu(gate)


def _attnp_kernel(q_ref, k_ref, v_ref, g_ref, lamp_ref, dnw_ref, o_ref, qs_ref, m_ref, l_ref, acc_ref,
                  *, tq, tk, lam_init):
    h = pl.program_id(1)
    i = pl.program_id(2)
    j = pl.program_id(3)

    @pl.when(j == 0)
    def _():
        q = q_ref[...] * (A_DH ** -0.5)
        lane = _iota(q.shape, 1)
        qs_ref[0:tq, :] = jnp.where(lane < A_DH, q, 0.0).astype(BF16)
        qs_ref[tq:2 * tq, :] = jnp.where(lane >= A_DH, q, 0.0).astype(BF16)
        m_ref[...] = jnp.full(m_ref.shape, NEG_INF, F32)
        l_ref[...] = jnp.zeros(l_ref.shape, F32)
        acc_ref[...] = jnp.zeros(acc_ref.shape, F32)

    @pl.when(j * tk <= i * tq + (tq - 1))
    def _():
        k = k_ref[...].astype(BF16)
        v = v_ref[...].astype(BF16)
        s = _dot_nt(qs_ref[...], k)
        kpos = j * tk + _iota((1, tk), 1)
        s = s + _alibi_slope(h) * (kpos - i * tq).astype(F32)
        qpos = i * tq + (_iota((2 * tq, 1), 0) & (tq - 1))
        s = jnp.where(kpos <= qpos, s, NEG_INF)
        m_prev = m_ref[...]
        m_new = jnp.maximum(m_prev, jnp.max(s, axis=-1, keepdims=True))
        alpha = jnp.exp(m_prev - m_new)
        p = jnp.exp(s - m_new)
        l_ref[...] = alpha * l_ref[...] + jnp.sum(p, axis=-1, keepdims=True)
        acc_ref[...] = alpha * acc_ref[...] + _dot(p.astype(BF16), v)
        m_ref[...] = m_new

    @pl.when(j == pl.num_programs(3) - 1)
    def _():
        o = acc_ref[...] / l_ref[...]
        lam = _lambda_value(lamp_ref[...], lam_init)
        y = _diff_epilogue(o[0:tq], o[tq:2 * tq], lam, dnw_ref[...], 1.0 - lam_init, g_ref[...])
        o_ref[...] = y.astype(o_ref.dtype)


def _attn_prompt(p3, lam_p, dnw, lam_init):
    b, l, _ = p3.shape
    tq = min(l, 512)
    tk = tq
    nq, nk = l // tq, l // tk

    def kv_map(seg):
        return lambda bi, h, i, j: (bi, jnp.minimum(j, (i * tq + tq - 1) // tk), seg * N_HEADS + h)

    return pl.pallas_call(
        functools.partial(_attnp_kernel, tq=tq, tk=tk, lam_init=lam_init),
        grid=(b, N_HEADS, nq, nk),
        in_specs=[
            pl.BlockSpec((None, tq, HEAD_W), lambda bi, h, i, j: (bi, i, SEG_AQ * N_HEADS + h)),
            pl.BlockSpec((None, tk, HEAD_W), kv_map(SEG_AK)),
            pl.BlockSpec((None, tk, HEAD_W), kv_map(SEG_AV)),
            pl.BlockSpec((None, tq, HEAD_W), lambda bi, h, i, j: (bi, i, SEG_AG * N_HEADS + h)),
            pl.BlockSpec((4, A_DH), lambda bi, h, i, j: (0, 0)),
            pl.BlockSpec((1, HEAD_W), lambda bi, h, i, j: (0, 0)),
        ],
        out_specs=pl.BlockSpec((None, tq, HEAD_W), lambda bi, h, i, j: (bi, i, h)),
        out_shape=jax.ShapeDtypeStruct((b, l, GROUP_W), BF16),
        scratch_shapes=[
            pltpu.VMEM((2 * tq, HEAD_W), BF16),
            pltpu.VMEM((2 * tq, 1), F32),
            pltpu.VMEM((2 * tq, 1), F32),
            pltpu.VMEM((2 * tq, HEAD_W), F32),
        ],
        compiler_params=_cparams(("parallel", "parallel", "parallel", "arbitrary")),
        name="attn_prompt",
    )(p3, p3, p3, p3, lam_p, dnw.reshape(1, HEAD_W))


def _attns_kernel(pt_ref, x_ref, lamp_ref, dnw_ref, *rest, n_pages, ls, lam_init):
    k_refs = rest[:n_pages]
    v_refs = rest[n_pages:2 * n_pages]
    o_ref = rest[2 * n_pages]
    del pt_ref
    rows = 2 * ls * N_HEADS
    past = n_pages * PAGE

    x = x_ref[...]
    q = x[:, 0:GROUP_W] * (A_DH ** -0.5)
    k_new = x[:, GROUP_W:2 * GROUP_W]
    v_new = x[:, 2 * GROUP_W:3 * GROUP_W]
    gate = x[:, 3 * GROUP_W:4 * GROUP_W]

    lane_q = _iota((ls, HEAD_W), 1)
    parts = []
    for h in range(N_HEADS):
        qh = q[:, h * HEAD_W:(h + 1) * HEAD_W]
        parts += [jnp.where(lane_q < A_DH, qh, 0.0), jnp.where(lane_q >= A_DH, qh, 0.0)]
    qs = jnp.concatenate(parts, axis=0)
    row_head_w = _div(_iota((rows, GROUP_W), 0), 2 * ls)
    qb = jnp.where(row_head_w == _div(_iota((rows, GROUP_W), 1), HEAD_W),
                   jnp.concatenate([qs] * N_HEADS, axis=1), 0.0).astype(BF16)
    qs = qs.astype(BF16)

    row_head = _div(_iota((rows, 1), 0), 2 * ls)
    slope = _alibi_slope(row_head)
    tok = _iota((rows, 1), 0) & (ls - 1)

    col = _iota((1, N_HEADS * PAGE), 1)
    own = (col & (N_HEADS - 1)) == row_head
    key_in_page = _div(col, N_HEADS)
    s_pages = []
    for p in range(n_pages):
        s = _dot_nt(qs, k_refs[p][...].astype(BF16))
        s = s + slope * (key_in_page + (p * PAGE - past)).astype(F32)
        s_pages.append(jnp.where(own, s, NEG_INF))
    lane = _iota((1, PAGE), 1)
    pad = jnp.zeros((PAGE - ls, GROUP_W), F32)
    k_pad = jnp.concatenate([k_new, pad], axis=0).astype(BF16)
    v_pad = jnp.concatenate([v_new, pad], axis=0).astype(BF16)
    s_new = _dot_nt(qb, k_pad) + slope * lane.astype(F32)
    s_new = jnp.where(lane <= tok, s_new, NEG_INF)

    m = s_new.max(axis=-1, keepdims=True)
    for s in s_pages:
        m = jnp.maximum(m, s.max(axis=-1, keepdims=True))
    e = jnp.exp(s_new - m)
    l = jnp.sum(e, axis=-1, keepdims=True)
    o_new = _dot(e.astype(BF16), v_pad)
    acc = jnp.zeros((rows, HEAD_W), F32)
    for h in range(N_HEADS):
        acc = acc + jnp.where(row_head == h, o_new[:, h * HEAD_W:(h + 1) * HEAD_W], 0.0)
    for p, s in enumerate(s_pages):
        e = jnp.exp(s - m)
        l = l + jnp.sum(e, axis=-1, keepdims=True)
        acc = acc + _dot(e.astype(BF16), v_refs[p][...].astype(BF16))
    o = acc / l

    lam = _lambda_value(lamp_ref[...], lam_init)
    outs = []
    for h in range(N_HEADS):
        o1 = o[2 * ls * h:2 * ls * h + ls]
        o2 = o[2 * ls * h + ls:2 * ls * (h + 1)]
        outs.append(_diff_epilogue(o1, o2, lam, dnw_ref[...], 1.0 - lam_init, gate[:, h * HEAD_W:(h + 1) * HEAD_W]))
    o_ref[...] = jnp.concatenate(outs, axis=1).astype(o_ref.dtype)


def _attn_sample(p3, cache_k4, cache_v4, layer, page_table, lam_p, dnw, lam_init):
    bs, ls, _ = p3.shape
    n_pages = page_table.shape[1]

    def page_map(p):
        return lambda b, pt: (layer, pt[b * n_pages + p], 0, 0)

    page_specs = lambda n: [pl.BlockSpec((None, None, N_HEADS * PAGE, HEAD_W), page_map(p)) for p in range(n)]
    grid_spec = pltpu.PrefetchScalarGridSpec(
        num_scalar_prefetch=1,
        grid=(bs,),
        in_specs=[
            pl.BlockSpec((None, ls, 4 * GROUP_W), lambda b, pt: (b, 0, 0)),
            pl.BlockSpec((4, A_DH), lambda b, pt: (0, 0)),
            pl.BlockSpec((1, HEAD_W), lambda b, pt: (0, 0)),
        ] + page_specs(n_pages) + page_specs(n_pages),
        out_specs=pl.BlockSpec((None, ls, GROUP_W), lambda b, pt: (b, 0, 0)),
    )
    return pl.pallas_call(
        functools.partial(_attns_kernel, n_pages=n_pages, ls=ls, lam_init=lam_init),
        grid_spec=grid_spec,
        out_shape=jax.ShapeDtypeStruct((bs, ls, GROUP_W), F32),
        compiler_params=_cparams(("parallel",)),
        name="attn_sample",
    )(page_table.reshape(-1), p3, lam_p, dnw.reshape(1, HEAD_W), *([cache_k4] * n_pages), *([cache_v4] * n_pages))


def _cross_kernel(q_ref, g_ref, mk_ref, mv_ref, o_ref, *, nseq):
    for s in range(nseq):
        q = q_ref[s] * (HEAD_W ** -0.5)
        gate = g_ref[s]
        mk = mk_ref[s].astype(BF16)
        mv = mv_ref[s].astype(BF16)
        outs = []
        for h in range(N_HEADS):
            sl = slice(h * HEAD_W, (h + 1) * HEAD_W)
            sc = _dot_nt(q[:, sl].astype(BF16), mk[:, sl])
            e = jnp.exp(sc - sc.max(axis=-1, keepdims=True))
            o = _dot(e.astype(BF16), mv[:, sl]) / jnp.sum(e, axis=-1, keepdims=True)
            outs.append(o * _silu(gate[:, sl]))
        o_ref[s] = jnp.concatenate(outs, axis=1).astype(o_ref.dtype)


def _cross_prompt(p4, mk, mv):
    nb, nt, tq, _ = p4.shape
    n_mem = mk.shape[1]
    return pl.pallas_call(
        functools.partial(_cross_kernel, nseq=1),
        grid=(nb, nt),
        in_specs=[
            pl.BlockSpec((1, None, tq, GROUP_W), lambda b, i: (b, i, 0, SEG_XQ)),
            pl.BlockSpec((1, None, tq, GROUP_W), lambda b, i: (b, i, 0, SEG_XG)),
            pl.BlockSpec((1, n_mem, GROUP_W), lambda b, i: (b, 0, 0)),
            pl.BlockSpec((1, n_mem, GROUP_W), lambda b, i: (b, 0, 0)),
        ],
        out_specs=pl.BlockSpec((1, None, tq, GROUP_W), lambda b, i: (b, i, 0, 0)),
        out_shape=jax.ShapeDtypeStruct((nb, nt, tq, GROUP_W), BF16),
        compiler_params=_cparams(("parallel", "parallel")),
        name="cross_prompt",
    )(p4, p4, mk, mv)


def _cross_sample_kernel(q_ref, g_ref, mk_ref, mv_ref, o_ref, *, nseq, ls):
    rows = N_HEADS * ls
    n_rows = mk_ref.shape[1]
    own = (_iota((1, n_rows), 1) & (N_HEADS - 1)) == _div(_iota((rows, 1), 0), ls)
    for s in range(nseq):
        q = q_ref[s] * (HEAD_W ** -0.5)
        gate = g_ref[s]
        qx = jnp.concatenate([q[:, h * HEAD_W:(h + 1) * HEAD_W] for h in range(N_HEADS)], axis=0)
        sc = jnp.where(own, _dot_nt(qx.astype(BF16), mk_ref[s].astype(BF16)), NEG_INF)
        e = jnp.exp(sc - sc.max(axis=-1, keepdims=True))
        o = _dot(e.astype(BF16), mv_ref[s].astype(BF16)) / jnp.sum(e, axis=-1, keepdims=True)
        o = jnp.concatenate([o[h * ls:(h + 1) * ls] for h in range(N_HEADS)], axis=1)
        o_ref[s] = o * _silu(gate)


def _cross_sample(p3, mk, mv, layer):
    bs, ls, _ = p3.shape
    n_rows = mk.shape[2]
    nseq = 8
    return pl.pallas_call(
        functools.partial(_cross_sample_kernel, nseq=nseq, ls=ls),
        grid=(bs // nseq,),
        in_specs=[
            pl.BlockSpec((nseq, ls, GROUP_W), lambda b: (b, 0, SEG_XQ)),
            pl.BlockSpec((nseq, ls, GROUP_W), lambda b: (b, 0, SEG_XG)),
            pl.BlockSpec((None, nseq, n_rows, HEAD_W), lambda b: (layer, b, 0, 0)),
            pl.BlockSpec((None, nseq, n_rows, HEAD_W), lambda b: (layer, b, 0, 0)),
        ],
        out_specs=pl.BlockSpec((nseq, ls, GROUP_W), lambda b: (b, 0, 0)),
        out_shape=jax.ShapeDtypeStruct((bs, ls, GROUP_W), F32),
        compiler_params=_cparams(("parallel",)),
        name="cross_sample",
    )(p3, p3, mk, mv)


def _stack_heads(a):
    return jnp.concatenate([a[:, h * HEAD_W:(h + 1) * HEAD_W] for h in range(N_HEADS)], axis=0)


def _col(x, lane_idx):
    lane = _iota(x.shape, 1)
    return jnp.sum(jnp.where(lane == lane_idx, x, 0.0), axis=-1, keepdims=True)


def _split3_dot(mat_bf16, x):
    hi = x.astype(BF16)
    r1 = x - hi.astype(F32)
    mid = r1.astype(BF16)
    lo = (r1 - mid.astype(F32)).astype(BF16)
    return _dot(mat_bf16, hi) + _dot(mat_bf16, mid) + _dot(mat_bf16, lo)


def _gdn_block(qkv, ab, alog_row, dt_row, blk):
    n = N_HEADS * CHUNK
    sh = int(math.log2(blk))
    q = _stack_heads(qkv[0])
    k = _stack_heads(qkv[1])
    v = _stack_heads(qkv[2])
    qn = q * lax.rsqrt(jnp.sum(q * q, axis=-1, keepdims=True) + EPS) * (HEAD_W ** -0.5)
    kn = k * lax.rsqrt(jnp.sum(k * k, axis=-1, keepdims=True) + EPS)

    lane = _iota((1, LANES), 1)
    neg_a = jnp.where(lane < N_HEADS, -jnp.exp(alog_row), 0.0)
    g = neg_a * _softplus(ab + dt_row)
    beta_full = _sigmoid(ab)

    ti = _iota((CHUNK, CHUNK), 0)
    tj = _iota((CHUNK, CHUNK), 1)
    same_t = (ti >> sh) == (tj >> sh)
    gcum = _split3_dot((same_t & (tj <= ti)).astype(BF16), g)
    gtot = _split3_dot(same_t.astype(BF16), g)

    gc = jnp.concatenate([_col(gcum, h) for h in range(N_HEADS)], axis=0)
    gl = jnp.concatenate([_col(gtot, h) for h in range(N_HEADS)], axis=0)
    beta = jnp.concatenate([_col(beta_full, N_HEADS + h) for h in range(N_HEADS)], axis=0)

    gt = jnp.concatenate([gcum, gcum], axis=0).T
    gr = jnp.concatenate([jnp.where(lane < CHUNK, gt[0:1, :], gt[1:2, :]),
                          jnp.where(lane < CHUNK, gt[2:3, :], gt[3:4, :])], axis=1)

    ri = _iota((n, n), 0)
    ci = _iota((n, n), 1)
    same = (ri >> sh) == (ci >> sh)
    incl = same & (ci <= ri)
    decay = jnp.where(incl, jnp.exp(jnp.where(incl, gc - gr, 0.0)), 0.0)

    kb = kn * beta
    kq = _dot_nt(jnp.concatenate([kb, qn], axis=0).astype(BF16), kn.astype(BF16))
    nmat = jnp.where(same & (ci < ri), kq[0:n] * decay, 0.0)
    qkm = kq[n:2 * n] * decay

    pair = ((ri >> 1) == (ci >> 1)) & ((ri & 1) == 1) & ((ci & 1) == 0)
    x = jnp.where(ri == ci, 1.0, 0.0) - jnp.where(pair, nmat, 0.0)
    b = 2
    while b < blk:
        sb = int(math.log2(2 * b))
        mask = ((ri >> sb) == (ci >> sb)) & ((ri & (2 * b - 1)) >= b) & ((ci & (2 * b - 1)) < b)
        lb = jnp.where(mask, nmat, 0.0).astype(BF16)
        xb = x.astype(BF16)
        x = x - _dot(_dot(xb, lb).astype(BF16), xb)
        b *= 2

    eg = jnp.exp(gc)
    rhs = jnp.concatenate([v * beta, kb * eg], axis=1).astype(BF16)
    sol = _dot(x.astype(BF16), rhs)
    u = sol[:, 0:HEAD_W]
    w = sol[:, HEAD_W:2 * HEAD_W]
    kdec = kn * jnp.exp(gl - gc)
    return u, w, qn * eg, qkm, kdec, gl


def _pad_t(a):
    return jnp.concatenate([a, jnp.zeros((LANES - CHUNK, LANES), F32)], axis=0).T


def _pad_rows(a):
    return jnp.concatenate([a, jnp.zeros((LANES - CHUNK, LANES), F32)], axis=0)


def _gdn_finish(o, gate, dnw):
    outs = []
    for h in range(N_HEADS):
        oh = o[h * CHUNK:(h + 1) * CHUNK]
        outs.append(_rms(oh, dnw) * _silu(gate[:, h * HEAD_W:(h + 1) * HEAD_W]))
    return jnp.concatenate(outs, axis=1)


def _conv_taps(xbuf, p, base, nrows, cw):
    acc = xbuf[p, pl.ds(base + 5, nrows), :] * cw[0:1, p * GROUP_W:(p + 1) * GROUP_W]
    for t in range(1, CONV_W):
        acc = acc + xbuf[p, pl.ds(base + 5 + t, nrows), :] * cw[t:t + 1, p * GROUP_W:(p + 1) * GROUP_W]
    return _silu(acc)


def _gdnp_kernel(bq_ref, bk_ref, bv_ref, bg_ref, ab_ref, cw_ref, alog_ref, dt_ref, dnw_ref,
                 o_ref, sout_ref, cout_ref, xbuf, s_scr, *, tb):
    t = pl.program_id(1)

    @pl.when(t == 0)
    def _():
        xbuf[:, 0:8, :] = jnp.zeros((3, 8, GROUP_W), F32)
        s_scr[...] = jnp.zeros(s_scr.shape, F32)

    x_refs = (bq_ref, bk_ref, bv_ref)
    for p in range(3):
        xbuf[p, 8:8 + tb, :] = x_refs[p][...]
    cw = cw_ref[...]
    qkv = [_conv_taps(xbuf, p, 0, tb, cw) for p in range(3)]
    for p in range(3):
        xbuf[p, 5:8, :] = xbuf[p, tb + 5:tb + 8, :]

    gate = bg_ref[...]
    ab = ab_ref[...]
    for c in range(tb // CHUNK):
        rs = slice(c * CHUNK, (c + 1) * CHUNK)
        u, w, qg, qkm, kdec, gl = _gdn_block([a[rs] for a in qkv], ab[rs], alog_ref[...], dt_ref[...], CHUNK)
        vnews, qss = [], []
        for h in range(N_HEADS):
            hs = slice(h * CHUNK, (h + 1) * CHUNK)
            s_old = s_scr[h]
            wq = jnp.concatenate([w[hs], qg[hs]], axis=0).astype(BF16)
            r = _dot(wq, s_old.astype(BF16))
            vn = u[hs] - r[0:CHUNK]
            vnews.append(vn)
            qss.append(r[CHUNK:2 * CHUNK])
            upd = _dot(_pad_t(kdec[hs]).astype(BF16), _pad_rows(vn).astype(BF16))
            s_scr[h] = s_old * jnp.exp(gl[h * CHUNK:h * CHUNK + 1, :]) + upd
        vnew = jnp.concatenate(vnews, axis=0)
        o = jnp.concatenate(qss, axis=0) + _dot(qkm.astype(BF16), vnew.astype(BF16))
        o_ref[rs, :] = _gdn_finish(o, gate[rs], dnw_ref[...]).astype(o_ref.dtype)

    @pl.when(t == pl.num_programs(1) - 1)
    def _():
        sout_ref[...] = s_scr[...]
        for p in range(3):
            cout_ref[:, p * GROUP_W:(p + 1) * GROUP_W] = xbuf[p, 5:8, :]


def _gdn_prompt(p3, g3, conv_w, alog_row, dt_row, dnw):
    b, l, _ = p3.shape
    tb = min(l, 256)
    seg = lambda s: pl.BlockSpec((None, tb, GROUP_W), lambda bi, t: (bi, t, s))
    const = lambda shape: pl.BlockSpec(shape, lambda bi, t: (0,) * len(shape))
    return pl.pallas_call(
        functools.partial(_gdnp_kernel, tb=tb),
        grid=(b, l // tb),
        in_specs=[seg(SEG_BQ), seg(SEG_BK), seg(SEG_BV), seg(SEG_BG),
                  pl.BlockSpec((None, tb, LANES), lambda bi, t: (bi, t, 0)),
                  const((CONV_W, 3 * GROUP_W)), const((1, LANES)), const((1, LANES)), const((1, HEAD_W))],
        out_specs=[
            pl.BlockSpec((None, tb, GROUP_W), lambda bi, t: (bi, t, 0)),
            pl.BlockSpec((None, N_HEADS, HEAD_W, HEAD_W), lambda bi, t: (bi, 0, 0, 0)),
            pl.BlockSpec((None, CONV_W - 1, 3 * GROUP_W), lambda bi, t: (bi, 0, 0)),
        ],
        out_shape=[
            jax.ShapeDtypeStruct((b, l, GROUP_W), BF16),
            jax.ShapeDtypeStruct((b, N_HEADS, HEAD_W, HEAD_W), F32),
            jax.ShapeDtypeStruct((b, CONV_W - 1, 3 * GROUP_W), F32),
        ],
        scratch_shapes=[pltpu.VMEM((3, tb + 8, GROUP_W), F32), pltpu.VMEM((N_HEADS, HEAD_W, HEAD_W), F32)],
        compiler_params=_cparams(("parallel", "arbitrary")),
        name="gdn_prompt",
    )(p3, p3, p3, p3, g3, conv_w, alog_row, dt_row, dnw.reshape(1, HEAD_W))


def _gdns_kernel(bq_ref, bk_ref, bv_ref, bg_ref, ab_ref, cw_ref, alog_ref, dt_ref, dnw_ref, c0_ref, s0_ref,
                 o_ref, sout_ref, cout_ref, xbuf, *, ls, nseq):
    x_refs = (bq_ref, bk_ref, bv_ref)
    cw = cw_ref[...]
    qkv = []
    for p in range(3):
        for s in range(nseq):
            xbuf[p, s * 16 + 5:s * 16 + 8, :] = c0_ref[s, :, p * GROUP_W:(p + 1) * GROUP_W]
            xbuf[p, s * 16 + 8:s * 16 + 8 + ls, :] = x_refs[p][s * ls:(s + 1) * ls, :]
        qkv.append(jnp.concatenate([_conv_taps(xbuf, p, s * 16, ls, cw) for s in range(nseq)], axis=0))
        for s in range(nseq):
            cout_ref[s, :, p * GROUP_W:(p + 1) * GROUP_W] = xbuf[p, s * 16 + 5 + ls:s * 16 + 8 + ls, :]

    u, w, qg, qkm, kdec, gl = _gdn_block(qkv, ab_ref[...], alog_ref[...], dt_ref[...], ls)
    seq_of_row = _div(_iota((2 * CHUNK, 1), 0) & (CHUNK - 1), ls)
    vnews, qss = [], []
    for h in range(N_HEADS):
        hs = slice(h * CHUNK, (h + 1) * CHUNK)
        wq = jnp.concatenate([w[hs], qg[hs]], axis=0)
        r = jnp.zeros((2 * CHUNK, HEAD_W), F32)
        for s in range(nseq):
            r = r + _dot(jnp.where(seq_of_row == s, wq, 0.0).astype(BF16), s0_ref[s, h].astype(BF16))
        vn = u[hs] - r[0:CHUNK]
        vnews.append(vn)
        qss.append(r[CHUNK:2 * CHUNK])
        vn_pad = _pad_rows(vn).astype(BF16)
        for s in range(nseq):
            kd = jnp.where(seq_of_row[0:CHUNK] == s, kdec[hs], 0.0)
            row = h * CHUNK + s * ls
            sout_ref[s, h] = s0_ref[s, h] * jnp.exp(gl[row:row + 1, :]) + _dot(_pad_t(kd).astype(BF16), vn_pad)
    vnew = jnp.concatenate(vnews, axis=0)
    o = jnp.concatenate(qss, axis=0) + _dot(qkm.astype(BF16), vnew.astype(BF16))
    o_ref[...] = _gdn_finish(o, bg_ref[...], dnw_ref[...]).astype(o_ref.dtype)


def _gdn_sample(p2, g2, ls, conv_w, alog_row, dt_row, dnw, conv0, s0, layer):
    bs = s0.shape[1]
    nseq = CHUNK // ls
    seg = lambda s: pl.BlockSpec((CHUNK, GROUP_W), lambda i: (i, s))
    const = lambda shape: pl.BlockSpec(shape, lambda i: (0,) * len(shape))
    return pl.pallas_call(
        functools.partial(_gdns_kernel, ls=ls, nseq=nseq),
        grid=(bs // nseq,),
        in_specs=[seg(SEG_BQ), seg(SEG_BK), seg(SEG_BV), seg(SEG_BG),
                  pl.BlockSpec((CHUNK, LANES), lambda i: (i, 0)),
                  const((CONV_W, 3 * GROUP_W)), const((1, LANES)), const((1, LANES)), const((1, HEAD_W)),
                  pl.BlockSpec((None, nseq, CONV_W - 1, 3 * GROUP_W), lambda i: (layer, i, 0, 0)),
                  pl.BlockSpec((None, nseq, N_HEADS, HEAD_W, HEAD_W), lambda i: (layer, i, 0, 0, 0))],
        out_specs=[
            pl.BlockSpec((CHUNK, GROUP_W), lambda i: (i, 0)),
            pl.BlockSpec((nseq, N_HEADS, HEAD_W, HEAD_W), lambda i: (i, 0, 0, 0)),
            pl.BlockSpec((nseq, CONV_W - 1, 3 * GROUP_W), lambda i: (i, 0, 0)),
        ],
        out_shape=[
            jax.ShapeDtypeStruct((bs * ls, GROUP_W), F32),
            jax.ShapeDtypeStruct(s0.shape[1:], F32),
            jax.ShapeDtypeStruct(conv0.shape[1:], F32),
        ],
        scratch_shapes=[pltpu.VMEM((3, nseq * 16, GROUP_W), F32)],
        compiler_params=_cparams(("parallel",)),
        name="gdn_sample",
    )(p2, p2, p2, p2, g2, conv_w, alog_row, dt_row, dnw.reshape(1, HEAD_W), conv0, s0)


def _outproj_kernel(h_ref, oa_ref, ob_ref, ox_ref, w_ref, nf_ref, o_ref, *, final):
    acc = h_ref[...]
    for n, r in enumerate((oa_ref, ob_ref, ox_ref)):
        acc = acc + _dot(r[...].astype(BF16), w_ref[n * GROUP_W:(n + 1) * GROUP_W, :])
    o_ref[...] = _rms(acc, nf_ref[...]) if final else acc


def _outproj(h2d, oa, ob, ox, w_out, norm_f, final):
    t = h2d.shape[0]
    tm = min(t, 1024)
    row = lambda w: pl.BlockSpec((tm, w), lambda i: (i, 0))
    return pl.pallas_call(
        functools.partial(_outproj_kernel, final=final),
        grid=(t // tm,),
        in_specs=[row(D_MODEL), row(GROUP_W), row(GROUP_W), row(GROUP_W),
                  pl.BlockSpec((3 * GROUP_W, D_MODEL), lambda i: (0, 0)),
                  pl.BlockSpec((1, D_MODEL), lambda i: (0, 0))],
        out_specs=row(D_MODEL),
        out_shape=jax.ShapeDtypeStruct((t, D_MODEL), F32),
        compiler_params=_cparams(("parallel",)),
        name="outproj",
    )(h2d, oa, ob, ox, w_out, norm_f.reshape(1, D_MODEL))


def _pad_lanes(v):
    return jnp.pad(v.astype(F32), (0, LANES - v.shape[0])).reshape(1, LANES)


def kernel(x_prompt, x_sample, cache_k, cache_v, state_delta, state_conv, cache_mem_k, cache_mem_v, page_table, mem_prompt, norm_w, w_in, diff_lambda, diff_norm_w, conv_w, a_log, dt_bias, delta_norm_w, norm_mem, w_mem_kv, w_out, norm_f):
    bp, lp, _ = x_prompt.shape
    bs, ls, _ = x_sample.shape
    depth = w_in.shape[0]
    n_mem = mem_prompt.shape[1]
    n_gate = 2 * N_HEADS
    main_w = N_MAIN - 2 * GROUP_W

    w_main = jnp.concatenate([w_in[:, :, :main_w], w_in[:, :, main_w + n_gate:]], axis=2).astype(BF16)
    w_gate = jnp.pad(w_in[:, :, main_w:main_w + n_gate], ((0, 0), (0, 0), (0, LANES - n_gate))).astype(BF16)
    w_out_b = w_out.astype(BF16)
    cache_k4 = cache_k.reshape(cache_k.shape[0], cache_k.shape[1], PAGE * N_HEADS, HEAD_W)
    cache_v4 = cache_v.reshape(cache_v.shape[0], cache_v.shape[1], PAGE * N_HEADS, HEAD_W)
    mem_k4 = cache_mem_k.reshape(depth, bs, n_mem * N_HEADS, HEAD_W)
    mem_v4 = cache_mem_v.reshape(depth, bs, n_mem * N_HEADS, HEAD_W)

    mk_all, mv_all = _memkv(mem_prompt.reshape(bp * n_mem, D_MODEL), norm_mem, w_mem_kv.astype(BF16))

    hp = x_prompt.reshape(bp * lp, D_MODEL)
    hs = x_sample.reshape(bs * ls, D_MODEL)
    tq_x = min(lp, 512)
    kp, vp, sp, cp, ks, vs, ss, cs = ([] for _ in range(8))
    for l in range(depth):
        lam_init = 0.8 - 0.6 * math.exp(-0.3 * l)
        alog_row = _pad_lanes(a_log[l])
        dt_row = _pad_lanes(dt_bias[l])
        final = l == depth - 1

        pp, gp = _proj(hp, norm_w[l], w_main[l], w_gate[l])
        pp3 = pp.reshape(bp, lp, N_MAIN)
        oa = _attn_prompt(pp3, diff_lambda[l], diff_norm_w[l], lam_init)
        ob, s_new, c_new = _gdn_prompt(pp3, gp.reshape(bp, lp, LANES), conv_w[l], alog_row, dt_row, delta_norm_w[l])
        ox = _cross_prompt(pp.reshape(bp, lp // tq_x, tq_x, N_MAIN), mk_all[l].reshape(bp, n_mem, GROUP_W),
                           mv_all[l].reshape(bp, n_mem, GROUP_W))
        hp = _outproj(hp, oa.reshape(bp * lp, GROUP_W), ob.reshape(bp * lp, GROUP_W),
                      ox.reshape(bp * lp, GROUP_W), w_out_b[l], norm_f, final)
        kp.append(pp3[:, :, SEG_AK * GROUP_W:(SEG_AK + 1) * GROUP_W])
        vp.append(pp3[:, :, SEG_AV * GROUP_W:(SEG_AV + 1) * GROUP_W])
        sp.append(s_new)
        cp.append(c_new)

        ps, gs = _proj(hs, norm_w[l], w_main[l], w_gate[l])
        ps3 = ps.reshape(bs, ls, N_MAIN)
        oa = _attn_sample(ps3, cache_k4, cache_v4, l, page_table, diff_lambda[l], diff_norm_w[l], lam_init)
        ob, s_new, c_new = _gdn_sample(ps, gs, ls, conv_w[l], alog_row, dt_row, delta_norm_w[l],
                                       state_conv, state_delta, l)
        ox = _cross_sample(ps3, mem_k4, mem_v4, l)
        hs = _outproj(hs, oa.reshape(bs * ls, GROUP_W), ob, ox.reshape(bs * ls, GROUP_W), w_out_b[l], norm_f, final)
        ks.append(ps3[:, :, SEG_AK * GROUP_W:(SEG_AK + 1) * GROUP_W])
        vs.append(ps3[:, :, SEG_AV * GROUP_W:(SEG_AV + 1) * GROUP_W])
        ss.append(s_new)
        cs.append(c_new)

    heads = lambda xs, b, n: jnp.stack(xs).reshape(depth, b, n, N_HEADS, HEAD_W)
    return (hp.reshape(bp, lp, D_MODEL), hs.reshape(bs, ls, D_MODEL),
            heads(kp, bp, lp), heads(vp, bp, lp), jnp.stack(sp), jnp.stack(cp),
            mk_all.reshape(depth, bp, n_mem, N_HEADS, HEAD_W), mv_all.reshape(depth, bp, n_mem, N_HEADS, HEAD_W),
            heads(ks, bs, ls), heads(vs, bs, ls), jnp.stack(ss), jnp.stack(cs))
```

```python
import functools
import math

import jax
import jax.numpy as jnp
from jax import lax
from jax.experimental import pallas as pl
from jax.experimental.pallas import tpu as pltpu

F32 = jnp.float32
BF16 = jnp.bfloat16

D_MODEL = 1024
N_HEADS = 4
HEAD_W = 128
GROUP_W = N_HEADS * HEAD_W
A_DH = 64
CONV_W = 4
CHUNK = 64
GDN_GROUP = 4
PAGE = 128
EPS = 1e-6
NEG_INF = -1e30
LOG2E = math.log2(math.e)
N_MAIN = 10 * GROUP_W
LANES = 128
VMEM_LIMIT = 48 * 1024 * 1024

SEG_AQ, SEG_AK, SEG_AV, SEG_AG, SEG_BQ, SEG_BK, SEG_BV, SEG_BG, SEG_XQ, SEG_XG = range(10)


def _cparams(sem):
    return pltpu.CompilerParams(dimension_semantics=sem, vmem_limit_bytes=VMEM_LIMIT)


def _iota(shape, dim):
    return lax.broadcasted_iota(jnp.int32, shape, dim)


def _div(x, d):
    assert d & (d - 1) == 0
    return x >> (d.bit_length() - 1)


def _sigmoid(x):
    return 1.0 / (1.0 + jnp.exp(-x))


def _silu(x):
    return x * _sigmoid(x)


def _softplus(x):
    return jnp.maximum(x, 0.0) + jnp.log(1.0 + jnp.exp(-jnp.abs(x)))


def _dot(a, b):
    return jnp.dot(a, b, preferred_element_type=F32)


def _dot_nt(a, b):
    return lax.dot_general(a, b, (((1,), (1,)), ((), ())), preferred_element_type=F32)


def _rms(x, w):
    ms = jnp.mean(x * x, axis=-1, keepdims=True)
    return x * lax.rsqrt(ms + EPS) * w


def _lambda_value(p, lam_init):
    a = jnp.sum(p[0:1, :] * p[1:2, :], axis=-1, keepdims=True)
    b = jnp.sum(p[2:3, :] * p[3:4, :], axis=-1, keepdims=True)
    return jnp.exp(a) - jnp.exp(b) + lam_init


def _proj_kernel(x_ref, nw_ref, w_ref, wg_ref, *rest, n16, tm):
    p_ref, g_ref, a16_ref, k_out, v_out, hn_ref = rest[-6:]
    j = pl.program_id(1)

    @pl.when(j == 0)
    def _():
        hn = _rms(x_ref[...], nw_ref[...]).astype(BF16)
        hn_ref[...] = hn
        g_ref[...] = _dot(hn, wg_ref[...])

    res = _dot(hn_ref[...], w_ref[...])
    p_ref[...] = res

    @pl.when(j < n16)
    def _():
        a16_ref[...] = res.astype(BF16)

    def scatter_heads(out_ref, cols):
        for h in range(N_HEADS):
            out_ref[pl.ds(h, tm, stride=N_HEADS), :] = cols[:, h * HEAD_W:(h + 1) * HEAD_W]

    @pl.when(j == 0)
    def _():
        scatter_heads(k_out, res[:, GROUP_W:2 * GROUP_W])

    @pl.when(j == 1)
    def _():
        scatter_heads(v_out, res[:, 0:GROUP_W])


def _proj(x2d, norm_w, w_main, w_gate, layer, depth, kv_all):
    t = x2d.shape[0]
    tm = min(t, 1024)
    tn = 1024
    n16 = 4 * GROUP_W // tn
    kv_spec = pl.BlockSpec((None, N_HEADS * tm, HEAD_W), lambda i, j: (layer, i, 0))
    kv_shape = jax.ShapeDtypeStruct((depth, N_HEADS * t, HEAD_W), F32)
    n_in = 4
    extra_in, extra_specs, aliases = (), [], {}
    if kv_all is not None:
        extra_in = tuple(kv_all)
        extra_specs = [pl.BlockSpec(memory_space=pl.ANY)] * 2
        aliases = {n_in: 3, n_in + 1: 4}
    return pl.pallas_call(
        functools.partial(_proj_kernel, n16=n16, tm=tm),
        grid=(t // tm, N_MAIN // tn),
        in_specs=[
            pl.BlockSpec((tm, D_MODEL), lambda i, j: (i, 0)),
            pl.BlockSpec((1, D_MODEL), lambda i, j: (0, 0)),
            pl.BlockSpec((D_MODEL, tn), lambda i, j: (0, j)),
            pl.BlockSpec((D_MODEL, LANES), lambda i, j: (0, 0)),
        ] + extra_specs,
        out_specs=[
            pl.BlockSpec((tm, tn), lambda i, j: (i, j)),
            pl.BlockSpec((tm, LANES), lambda i, j: (i, 0)),
            pl.BlockSpec((tm, tn), lambda i, j: (i, jnp.minimum(j, n16 - 1))),
            kv_spec, kv_spec,
        ],
        out_shape=[jax.ShapeDtypeStruct((t, N_MAIN), F32), jax.ShapeDtypeStruct((t, LANES), F32),
                   jax.ShapeDtypeStruct((t, 4 * GROUP_W), BF16), kv_shape, kv_shape],
        scratch_shapes=[pltpu.VMEM((tm, D_MODEL), BF16)],
        input_output_aliases=aliases,
        compiler_params=_cparams(("parallel", "arbitrary")),
        name="proj",
    )(x2d, norm_w.reshape(1, D_MODEL), w_main, w_gate, *extra_in)


def _memkv_kernel(x_ref, nw_ref, w_ref, k_ref, v_ref):
    hn = _rms(x_ref[...], nw_ref[...]).astype(BF16)
    kv = _dot(hn, w_ref[...])
    k_ref[...] = kv[:, :GROUP_W]
    v_ref[...] = kv[:, GROUP_W:]


def _memkv(mem2d, norm_mem, w_mem_kv_bf16):
    depth = norm_mem.shape[0]
    t = mem2d.shape[0]
    return pl.pallas_call(
        _memkv_kernel,
        grid=(depth,),
        in_specs=[
            pl.BlockSpec((t, D_MODEL), lambda l: (0, 0)),
            pl.BlockSpec((None, 1, D_MODEL), lambda l: (l, 0, 0)),
            pl.BlockSpec((None, D_MODEL, 2 * GROUP_W), lambda l: (l, 0, 0)),
        ],
        out_specs=[
            pl.BlockSpec((None, t, GROUP_W), lambda l: (l, 0, 0)),
            pl.BlockSpec((None, t, GROUP_W), lambda l: (l, 0, 0)),
        ],
        out_shape=[jax.ShapeDtypeStruct((depth, t, GROUP_W), F32)] * 2,
        compiler_params=_cparams(("parallel",)),
        name="memkv",
    )(mem2d, norm_mem.reshape(depth, 1, D_MODEL), w_mem_kv_bf16)


def _alibi_slope(h):
    return jnp.where(h == 0, 0.25, jnp.where(h == 1, 0.0625, jnp.where(h == 2, 0.015625, 0.00390625))).astype(F32)


def _diff_epilogue(o1, o2, lam, dnw, lam_scale, gate):
    o = o1 - lam * o2
    return _rms(o, dnw) * lam_scale * _silu(gate)


def _attnp_kernel(q_ref, k_ref, v_ref, g_ref, lamp_ref, dnw_ref, o_ref, qs_ref, s_a, s_b, m_ref, l_ref, acc_ref,
                  *, tq, lam_init):
    h = pl.program_id(1)
    i = pl.program_id(2)
    tk = tq
    slope2 = _alibi_slope(h) * LOG2E

    q = q_ref[...] * (A_DH ** -0.5 * LOG2E)
    lane = _iota(q.shape, 1)
    qs_ref[0] = jnp.where(lane < A_DH, q, 0.0).astype(BF16)
    qs_ref[1] = jnp.where(lane >= A_DH, q, 0.0).astype(BF16)
    m_ref[...] = jnp.full(m_ref.shape, NEG_INF, F32)
    l_ref[...] = jnp.zeros(l_ref.shape, F32)
    acc_ref[...] = jnp.zeros(acc_ref.shape, F32)

    def scores(mp, j):
        return _dot_nt(qs_ref[mp], k_ref[pl.ds(pl.multiple_of(j * tk, tk), tk), :])

    def update(mp, s_ref, j, diagonal):
        kcol = _iota((1, tk), 1)
        s = s_ref[...] + slope2 * (kcol + (j - i) * tk).astype(F32)
        if diagonal:
            s = jnp.where(kcol <= _iota((tq, 1), 0), s, NEG_INF)
        m_prev = m_ref[mp]
        m_new = jnp.maximum(m_prev, jnp.max(s, axis=-1, keepdims=True))
        alpha = jnp.exp2(m_prev - m_new)
        p = jnp.exp2(s - m_new)
        l_ref[mp] = alpha * l_ref[mp] + jnp.sum(p, axis=-1, keepdims=True)
        v = v_ref[pl.ds(pl.multiple_of(j * tk, tk), tk), :]
        acc_ref[mp] = alpha * acc_ref[mp] + _dot(p.astype(BF16), v)
        m_ref[mp] = m_new

    s_a[...] = scores(0, 0)

    def body(j, carry):
        s_b[...] = scores(1, j)
        update(0, s_a, j, False)
        s_a[...] = scores(0, j + 1)
        update(1, s_b, j, False)
        return carry

    lax.fori_loop(0, i, body, 0)
    s_b[...] = scores(1, i)
    update(0, s_a, i, True)
    update(1, s_b, i, True)

    lam = _lambda_value(lamp_ref[...], lam_init)
    y = _diff_epilogue(acc_ref[0] / l_ref[0], acc_ref[1] / l_ref[1], lam, dnw_ref[...], 1.0 - lam_init, g_ref[...])
    o_ref[...] = y.astype(o_ref.dtype)


def _attn_prompt(p3, a16, lam_p, dnw, lam_init):
    b, l, _ = p3.shape
    tq = min(l, 512)
    return pl.pallas_call(
        functools.partial(_attnp_kernel, tq=tq, lam_init=lam_init),
        grid=(b, N_HEADS, l // tq),
        in_specs=[
            pl.BlockSpec((None, tq, HEAD_W), lambda bi, h, i: (bi, i, SEG_AQ * N_HEADS + h)),
            pl.BlockSpec((None, l, HEAD_W), lambda bi, h, i: (bi, 0, SEG_AK * N_HEADS + h)),
            pl.BlockSpec((None, l, HEAD_W), lambda bi, h, i: (bi, 0, SEG_AV * N_HEADS + h)),
            pl.BlockSpec((None, tq, HEAD_W), lambda bi, h, i: (bi, i, SEG_AG * N_HEADS + h)),
            pl.BlockSpec((4, A_DH), lambda bi, h, i: (0, 0)),
            pl.BlockSpec((1, HEAD_W), lambda bi, h, i: (0, 0)),
        ],
        out_specs=pl.BlockSpec((None, tq, HEAD_W), lambda bi, h, i: (bi, i, h)),
        out_shape=jax.ShapeDtypeStruct((b, l, GROUP_W), BF16),
        scratch_shapes=[
            pltpu.VMEM((2, tq, HEAD_W), BF16),
            pltpu.VMEM((tq, tq), F32),
            pltpu.VMEM((tq, tq), F32),
            pltpu.VMEM((2, tq, 1), F32),
            pltpu.VMEM((2, tq, 1), F32),
            pltpu.VMEM((2, tq, HEAD_W), F32),
        ],
        compiler_params=_cparams(("parallel", "parallel", "arbitrary")),
        name="attn_prompt",
    )(p3, a16, a16, p3, lam_p, dnw.reshape(1, HEAD_W))


def _attns_kernel(pt_ref, x_ref, lamp_ref, dnw_ref, *rest, n_pages, ls, lam_init):
    k_refs = rest[:n_pages]
    v_refs = rest[n_pages:2 * n_pages]
    o_ref = rest[2 * n_pages]
    del pt_ref
    rows = 2 * ls * N_HEADS
    past = n_pages * PAGE

    x = x_ref[...]
    q = x[:, 0:GROUP_W] * (A_DH ** -0.5)
    k_new = x[:, GROUP_W:2 * GROUP_W]
    v_new = x[:, 2 * GROUP_W:3 * GROUP_W]
    gate = x[:, 3 * GROUP_W:4 * GROUP_W]

    lane_q = _iota((ls, HEAD_W), 1)
    parts = []
    for h in range(N_HEADS):
        qh = q[:, h * HEAD_W:(h + 1) * HEAD_W]
        parts += [jnp.where(lane_q < A_DH, qh, 0.0), jnp.where(lane_q >= A_DH, qh, 0.0)]
    qs = jnp.concatenate(parts, axis=0)
    row_head_w = _div(_iota((rows, GROUP_W), 0), 2 * ls)
    qb = jnp.where(row_head_w == _div(_iota((rows, GROUP_W), 1), HEAD_W),
                   jnp.concatenate([qs] * N_HEADS, axis=1), 0.0).astype(BF16)
    qs = qs.astype(BF16)

    row_head = _div(_iota((rows, 1), 0), 2 * ls)
    slope = _alibi_slope(row_head)
    tok = _iota((rows, 1), 0) & (ls - 1)

    col = _iota((1, N_HEADS * PAGE), 1)
    own = (col & (N_HEADS - 1)) == row_head
    key_in_page = _div(col, N_HEADS)
    s_pages = []
    for p in range(n_pages):
        s = _dot_nt(qs, k_refs[p][...].astype(BF16))
        s = s + slope * (key_in_page + (p * PAGE - past)).astype(F32)
        s_pages.append(jnp.where(own, s, NEG_INF))
    lane = _iota((1, PAGE), 1)
    pad = jnp.zeros((PAGE - ls, GROUP_W), F32)
    k_pad = jnp.concatenate([k_new, pad], axis=0).astype(BF16)
    v_pad = jnp.concatenate([v_new, pad], axis=0).astype(BF16)
    s_new = _dot_nt(qb, k_pad) + slope * lane.astype(F32)
    s_new = jnp.where(lane <= tok, s_new, NEG_INF)

    m = s_new.max(axis=-1, keepdims=True)
    for s in s_pages:
        m = jnp.maximum(m, s.max(axis=-1, keepdims=True))
    e = jnp.exp(s_new - m)
    l = jnp.sum(e, axis=-1, keepdims=True)
    o_new = _dot(e.astype(BF16), v_pad)
    acc = jnp.zeros((rows, HEAD_W), F32)
    for h in range(N_HEADS):
        acc = acc + jnp.where(row_head == h, o_new[:, h * HEAD_W:(h + 1) * HEAD_W], 0.0)
    for p, s in enumerate(s_pages):
        e = jnp.exp(s - m)
        l = l + jnp.sum(e, axis=-1, keepdims=True)
        acc = acc + _dot(e.astype(BF16), v_refs[p][...].astype(BF16))
    o = acc / l

    lam = _lambda_value(lamp_ref[...], lam_init)
    outs = []
    for h in range(N_HEADS):
        o1 = o[2 * ls * h:2 * ls * h + ls]
        o2 = o[2 * ls * h + ls:2 * ls * (h + 1)]
        outs.append(_diff_epilogue(o1, o2, lam, dnw_ref[...], 1.0 - lam_init, gate[:, h * HEAD_W:(h + 1) * HEAD_W]))
    o_ref[...] = jnp.concatenate(outs, axis=1).astype(o_ref.dtype)


def _attn_sample(p3, cache_k4, cache_v4, layer, page_table, lam_p, dnw, lam_init):
    bs, ls, _ = p3.shape
    n_pages = page_table.shape[1]

    def page_map(p):
        return lambda b, pt: (layer, pt[b * n_pages + p], 0, 0)

    page_specs = lambda n: [pl.BlockSpec((None, None, N_HEADS * PAGE, HEAD_W), page_map(p)) for p in range(n)]
    grid_spec = pltpu.PrefetchScalarGridSpec(
        num_scalar_prefetch=1,
        grid=(bs,),
        in_specs=[
            pl.BlockSpec((None, ls, 4 * GROUP_W), lambda b, pt: (b, 0, 0)),
            pl.BlockSpec((4, A_DH), lambda b, pt: (0, 0)),
            pl.BlockSpec((1, HEAD_W), lambda b, pt: (0, 0)),
        ] + page_specs(n_pages) + page_specs(n_pages),
        out_specs=pl.BlockSpec((None, ls, GROUP_W), lambda b, pt: (b, 0, 0)),
    )
    return pl.pallas_call(
        functools.partial(_attns_kernel, n_pages=n_pages, ls=ls, lam_init=lam_init),
        grid_spec=grid_spec,
        out_shape=jax.ShapeDtypeStruct((bs, ls, GROUP_W), F32),
        compiler_params=_cparams(("parallel",)),
        name="attn_sample",
    )(page_table.reshape(-1), p3, lam_p, dnw.reshape(1, HEAD_W), *([cache_k4] * n_pages), *([cache_v4] * n_pages))


def _cross_kernel(q_ref, g_ref, mk_ref, mv_ref, o_ref, *, nseq):
    for s in range(nseq):
        q = q_ref[s] * (HEAD_W ** -0.5)
        gate = g_ref[s]
        mk = mk_ref[s].astype(BF16)
        mv = mv_ref[s].astype(BF16)
        outs = []
        for h in range(N_HEADS):
            sl = slice(h * HEAD_W, (h + 1) * HEAD_W)
            sc = _dot_nt(q[:, sl].astype(BF16), mk[:, sl])
            e = jnp.exp(sc - sc.max(axis=-1, keepdims=True))
            o = _dot(e.astype(BF16), mv[:, sl]) / jnp.sum(e, axis=-1, keepdims=True)
            outs.append(o * _silu(gate[:, sl]))
        o_ref[s] = jnp.concatenate(outs, axis=1).astype(o_ref.dtype)


def _cross_prompt(p4, mk, mv):
    nb, nt, tq, _ = p4.shape
    n_mem = mk.shape[1]
    return pl.pallas_call(
        functools.partial(_cross_kernel, nseq=1),
        grid=(nb, nt),
        in_specs=[
            pl.BlockSpec((1, None, tq, GROUP_W), lambda b, i: (b, i, 0, SEG_XQ)),
            pl.BlockSpec((1, None, tq, GROUP_W), lambda b, i: (b, i, 0, SEG_XG)),
            pl.BlockSpec((1, n_mem, GROUP_W), lambda b, i: (b, 0, 0)),
            pl.BlockSpec((1, n_mem, GROUP_W), lambda b, i: (b, 0, 0)),
        ],
        out_specs=pl.BlockSpec((1, None, tq, GROUP_W), lambda b, i: (b, i, 0, 0)),
        out_shape=jax.ShapeDtypeStruct((nb, nt, tq, GROUP_W), BF16),
        compiler_params=_cparams(("parallel", "parallel")),
        name="cross_prompt",
    )(p4, p4, mk, mv)


def _cross_sample_kernel(q_ref, g_ref, mk_ref, mv_ref, o_ref, *, nseq, ls):
    rows = N_HEADS * ls
    n_rows = mk_ref.shape[1]
    own = (_iota((1, n_rows), 1) & (N_HEADS - 1)) == _div(_iota((rows, 1), 0), ls)
    for s in range(nseq):
        q = q_ref[s] * (HEAD_W ** -0.5)
        gate = g_ref[s]
        qx = jnp.concatenate([q[:, h * HEAD_W:(h + 1) * HEAD_W] for h in range(N_HEADS)], axis=0)
        sc = jnp.where(own, _dot_nt(qx.astype(BF16), mk_ref[s].astype(BF16)), NEG_INF)
        e = jnp.exp(sc - sc.max(axis=-1, keepdims=True))
        o = _dot(e.astype(BF16), mv_ref[s].astype(BF16)) / jnp.sum(e, axis=-1, keepdims=True)
        o = jnp.concatenate([o[h * ls:(h + 1) * ls] for h in range(N_HEADS)], axis=1)
        o_ref[s] = o * _silu(gate)


def _cross_sample(p3, mk, mv, layer):
    bs, ls, _ = p3.shape
    n_rows = mk.shape[2]
    nseq = 8
    return pl.pallas_call(
        functools.partial(_cross_sample_kernel, nseq=nseq, ls=ls),
        grid=(bs // nseq,),
        in_specs=[
            pl.BlockSpec((nseq, ls, GROUP_W), lambda b: (b, 0, SEG_XQ)),
            pl.BlockSpec((nseq, ls, GROUP_W), lambda b: (b, 0, SEG_XG)),
            pl.BlockSpec((None, nseq, n_rows, HEAD_W), lambda b: (layer, b, 0, 0)),
            pl.BlockSpec((None, nseq, n_rows, HEAD_W), lambda b: (layer, b, 0, 0)),
        ],
        out_specs=pl.BlockSpec((nseq, ls, GROUP_W), lambda b: (b, 0, 0)),
        out_shape=jax.ShapeDtypeStruct((bs, ls, GROUP_W), F32),
        compiler_params=_cparams(("parallel",)),
        name="cross_sample",
    )(p3, p3, mk, mv)


def _stack_heads(a):
    return jnp.concatenate([a[:, h * HEAD_W:(h + 1) * HEAD_W] for h in range(N_HEADS)], axis=0)


def _col(x, lane_idx):
    lane = _iota(x.shape, 1)
    return jnp.sum(jnp.where(lane == lane_idx, x, 0.0), axis=-1, keepdims=True)


def _split3_dot(mat_bf16, x):
    hi = x.astype(BF16)
    r1 = x - hi.astype(F32)
    mid = r1.astype(BF16)
    lo = (r1 - mid.astype(F32)).astype(BF16)
    return _dot(mat_bf16, hi) + _dot(mat_bf16, mid) + _dot(mat_bf16, lo)


def _gdn_blocks(blocks, alog_row, dt_row, blk):
    n = N_HEADS * CHUNK
    sh = int(math.log2(blk))
    lane = _iota((1, LANES), 1)
    neg_a = jnp.where(lane < N_HEADS, -jnp.exp(alog_row), 0.0)
    ti = _iota((CHUNK, CHUNK), 0)
    tj = _iota((CHUNK, CHUNK), 1)
    same_t = (ti >> sh) == (tj >> sh)
    tri_incl = (same_t & (tj <= ti)).astype(BF16)
    tri_all = same_t.astype(BF16)
    ri = _iota((n, n), 0)
    ci = _iota((n, n), 1)
    same = (ri >> sh) == (ci >> sh)
    incl = same & (ci <= ri)
    strict = same & (ci < ri)

    st = []
    for qkv, ab in blocks:
        q = _stack_heads(qkv[0])
        k = _stack_heads(qkv[1])
        v = _stack_heads(qkv[2])
        qn = q * lax.rsqrt(jnp.sum(q * q, axis=-1, keepdims=True) + EPS) * (HEAD_W ** -0.5)
        kn = k * lax.rsqrt(jnp.sum(k * k, axis=-1, keepdims=True) + EPS)
        g = neg_a * _softplus(ab + dt_row)
        beta_full = _sigmoid(ab)
        gcum = _split3_dot(tri_incl, g)
        gtot = _split3_dot(tri_all, g)
        gc = jnp.concatenate([_col(gcum, h) for h in range(N_HEADS)], axis=0)
        gl = jnp.concatenate([_col(gtot, h) for h in range(N_HEADS)], axis=0)
        beta = jnp.concatenate([_col(beta_full, N_HEADS + h) for h in range(N_HEADS)], axis=0)
        gt = jnp.concatenate([gcum, gcum], axis=0).T
        gr = jnp.concatenate([jnp.where(lane < CHUNK, gt[0:1, :], gt[1:2, :]),
                              jnp.where(lane < CHUNK, gt[2:3, :], gt[3:4, :])], axis=1)
        decay = jnp.where(incl, jnp.exp(jnp.where(incl, gc - gr, 0.0)), 0.0)
        kb = kn * beta
        eg = jnp.exp(gc)
        st.append(dict(qg=qn * eg, kn=kn, gl=gl, gc=gc, decay=decay,
                       lhs=jnp.concatenate([kb, qn], axis=0).astype(BF16),
                       rhs=jnp.concatenate([v * beta, kb * eg], axis=1).astype(BF16)))

    for s in st:
        kq = _dot_nt(s["lhs"], s["kn"].astype(BF16))
        s["nmat"] = jnp.where(strict, kq[0:n] * s["decay"], 0.0)
        s["qkm"] = kq[n:2 * n] * s["decay"]

    pair = ((ri >> 1) == (ci >> 1)) & ((ri & 1) == 1) & ((ci & 1) == 0)
    eye = jnp.where(ri == ci, 1.0, 0.0)
    for s in st:
        s["x"] = eye - jnp.where(pair, s["nmat"], 0.0)
    b = 2
    while b < blk:
        sb = int(math.log2(2 * b))
        mask = ((ri >> sb) == (ci >> sb)) & ((ri & (2 * b - 1)) >= b) & ((ci & (2 * b - 1)) < b)
        for s in st:
            s["xb"] = s["x"].astype(BF16)
            s["t"] = _dot(s["xb"], jnp.where(mask, s["nmat"], 0.0).astype(BF16)).astype(BF16)
        for s in st:
            s["x"] = s["x"] - _dot(s["t"], s["xb"])
        b *= 2

    outs = []
    for s in st:
        sol = _dot(s["x"].astype(BF16), s["rhs"])
        kdec = s["kn"] * jnp.exp(s["gl"] - s["gc"])
        outs.append((sol[:, 0:HEAD_W], sol[:, HEAD_W:2 * HEAD_W], s["qg"], s["qkm"], kdec, s["gl"]))
    return outs


def _pad_t(a):
    return jnp.concatenate([a, jnp.zeros((LANES - CHUNK, LANES), F32)], axis=0).T


def _pad_rows(a):
    return jnp.concatenate([a, jnp.zeros((LANES - CHUNK, LANES), F32)], axis=0)


def _gdn_finish(o, gate, dnw):
    outs = []
    for h in range(N_HEADS):
        oh = o[h * CHUNK:(h + 1) * CHUNK]
        outs.append(_rms(oh, dnw) * _silu(gate[:, h * HEAD_W:(h + 1) * HEAD_W]))
    return jnp.concatenate(outs, axis=1)


def _conv_taps(xbuf, p, base, nrows, cw):
    acc = xbuf[p, pl.ds(base + 5, nrows), :] * cw[0:1, p * GROUP_W:(p + 1) * GROUP_W]
    for t in range(1, CONV_W):
        acc = acc + xbuf[p, pl.ds(base + 5 + t, nrows), :] * cw[t:t + 1, p * GROUP_W:(p + 1) * GROUP_W]
    return _silu(acc)


def _gdnp_kernel(bq_ref, bk_ref, bv_ref, bg_ref, ab_ref, cw_ref, alog_ref, dt_ref, dnw_ref,
                 o_ref, sout_ref, cout_ref, xbuf, s_scr, *, tb):
    t = pl.program_id(1)

    @pl.when(t == 0)
    def _():
        xbuf[:, 0:8, :] = jnp.zeros((3, 8, GROUP_W), F32)
        s_scr[...] = jnp.zeros(s_scr.shape, F32)

    x_refs = (bq_ref, bk_ref, bv_ref)
    for p in range(3):
        xbuf[p, 8:8 + tb, :] = x_refs[p][...]
    cw = cw_ref[...]
    qkv = [_conv_taps(xbuf, p, 0, tb, cw) for p in range(3)]
    for p in range(3):
        xbuf[p, 5:8, :] = xbuf[p, tb + 5:tb + 8, :]

    gate = bg_ref[...]
    ab = ab_ref[...]
    n_chunks = tb // CHUNK
    pre = []
    for c0 in range(0, n_chunks, GDN_GROUP):
        rows = [slice(c * CHUNK, (c + 1) * CHUNK) for c in range(c0, min(c0 + GDN_GROUP, n_chunks))]
        pre += _gdn_blocks([([a[rs] for a in qkv], ab[rs]) for rs in rows], alog_ref[...], dt_ref[...], CHUNK)
    for c in range(n_chunks):
        rs = slice(c * CHUNK, (c + 1) * CHUNK)
        u, w, qg, qkm, kdec, gl = pre[c]
        vnews, qss = [], []
        for h in range(N_HEADS):
            hs = slice(h * CHUNK, (h + 1) * CHUNK)
            s_old = s_scr[h]
            wq = jnp.concatenate([w[hs], qg[hs]], axis=0).astype(BF16)
            r = _dot(wq, s_old.astype(BF16))
            vn = u[hs] - r[0:CHUNK]
            vnews.append(vn)
            qss.append(r[CHUNK:2 * CHUNK])
            upd = _dot(_pad_t(kdec[hs]).astype(BF16), _pad_rows(vn).astype(BF16))
            s_scr[h] = s_old * jnp.exp(gl[h * CHUNK:h * CHUNK + 1, :]) + upd
        vnew = jnp.concatenate(vnews, axis=0)
        o = jnp.concatenate(qss, axis=0) + _dot(qkm.astype(BF16), vnew.astype(BF16))
        o_ref[rs, :] = _gdn_finish(o, gate[rs], dnw_ref[...]).astype(o_ref.dtype)

    @pl.when(t == pl.num_programs(1) - 1)
    def _():
        sout_ref[...] = s_scr[...]
        for p in range(3):
            cout_ref[:, p * GROUP_W:(p + 1) * GROUP_W] = xbuf[p, 5:8, :]


def _gdn_prompt(p3, g3, conv_w, alog_row, dt_row, dnw):
    b, l, _ = p3.shape
    tb = min(l, 512)
    seg = lambda s: pl.BlockSpec((None, tb, GROUP_W), lambda bi, t: (bi, t, s))
    const = lambda shape: pl.BlockSpec(shape, lambda bi, t: (0,) * len(shape))
    return pl.pallas_call(
        functools.partial(_gdnp_kernel, tb=tb),
        grid=(b, l // tb),
        in_specs=[seg(SEG_BQ), seg(SEG_BK), seg(SEG_BV), seg(SEG_BG),
                  pl.BlockSpec((None, tb, LANES), lambda bi, t: (bi, t, 0)),
                  const((CONV_W, 3 * GROUP_W)), const((1, LANES)), const((1, LANES)), const((1, HEAD_W))],
        out_specs=[
            pl.BlockSpec((None, tb, GROUP_W), lambda bi, t: (bi, t, 0)),
            pl.BlockSpec((None, N_HEADS, HEAD_W, HEAD_W), lambda bi, t: (bi, 0, 0, 0)),
            pl.BlockSpec((None, CONV_W - 1, 3 * GROUP_W), lambda bi, t: (bi, 0, 0)),
        ],
        out_shape=[
            jax.ShapeDtypeStruct((b, l, GROUP_W), BF16),
            jax.ShapeDtypeStruct((b, N_HEADS, HEAD_W, HEAD_W), F32),
            jax.ShapeDtypeStruct((b, CONV_W - 1, 3 * GROUP_W), F32),
        ],
        scratch_shapes=[pltpu.VMEM((3, tb + 8, GROUP_W), F32), pltpu.VMEM((N_HEADS, HEAD_W, HEAD_W), F32)],
        compiler_params=_cparams(("parallel", "arbitrary")),
        name="gdn_prompt",
    )(p3, p3, p3, p3, g3, conv_w, alog_row, dt_row, dnw.reshape(1, HEAD_W))


def _gdns_kernel(bq_ref, bk_ref, bv_ref, bg_ref, ab_ref, cw_ref, alog_ref, dt_ref, dnw_ref, c0_ref, s0_ref,
                 *rest, ls, nseq):
    o_ref, sout_ref, cout_ref, xbuf = rest[-4:]
    x_refs = (bq_ref, bk_ref, bv_ref)
    cw = cw_ref[...]
    qkv = []
    for p in range(3):
        for s in range(nseq):
            xbuf[p, s * 16 + 5:s * 16 + 8, :] = c0_ref[s, :, p * GROUP_W:(p + 1) * GROUP_W]
            xbuf[p, s * 16 + 8:s * 16 + 8 + ls, :] = x_refs[p][s * ls:(s + 1) * ls, :]
        qkv.append(jnp.concatenate([_conv_taps(xbuf, p, s * 16, ls, cw) for s in range(nseq)], axis=0))
        for s in range(nseq):
            cout_ref[s, :, p * GROUP_W:(p + 1) * GROUP_W] = xbuf[p, s * 16 + 5 + ls:s * 16 + 8 + ls, :]

    (u, w, qg, qkm, kdec, gl), = _gdn_blocks([(qkv, ab_ref[...])], alog_ref[...], dt_ref[...], ls)
    seq_of_row = _div(_iota((2 * CHUNK, 1), 0) & (CHUNK - 1), ls)
    vnews, qss = [], []
    for h in range(N_HEADS):
        hs = slice(h * CHUNK, (h + 1) * CHUNK)
        wq = jnp.concatenate([w[hs], qg[hs]], axis=0)
        r = jnp.zeros((2 * CHUNK, HEAD_W), F32)
        for s in range(nseq):
            r = r + _dot(jnp.where(seq_of_row == s, wq, 0.0).astype(BF16), s0_ref[s, h].astype(BF16))
        vn = u[hs] - r[0:CHUNK]
        vnews.append(vn)
        qss.append(r[CHUNK:2 * CHUNK])
        vn_pad = _pad_rows(vn).astype(BF16)
        for s in range(nseq):
            kd = jnp.where(seq_of_row[0:CHUNK] == s, kdec[hs], 0.0)
            row = h * CHUNK + s * ls
            sout_ref[s, h] = s0_ref[s, h] * jnp.exp(gl[row:row + 1, :]) + _dot(_pad_t(kd).astype(BF16), vn_pad)
    vnew = jnp.concatenate(vnews, axis=0)
    o = jnp.concatenate(qss, axis=0) + _dot(qkm.astype(BF16), vnew.astype(BF16))
    o_ref[...] = _gdn_finish(o, bg_ref[...], dnw_ref[...]).astype(o_ref.dtype)


def _gdn_sample(p2, g2, ls, conv_w, alog_row, dt_row, dnw, conv0, s0, layer, s_all):
    bs = s0.shape[1]
    nseq = CHUNK // ls
    seg = lambda s: pl.BlockSpec((CHUNK, GROUP_W), lambda i: (i, s))
    const = lambda shape: pl.BlockSpec(shape, lambda i: (0,) * len(shape))
    n_in = 11
    extra_in, extra_specs, aliases = (), [], {}
    if s_all is not None:
        extra_in, extra_specs, aliases = (s_all,), [pl.BlockSpec(memory_space=pl.ANY)], {n_in: 1}
    return pl.pallas_call(
        functools.partial(_gdns_kernel, ls=ls, nseq=nseq),
        grid=(bs // nseq,),
        in_specs=[seg(SEG_BQ), seg(SEG_BK), seg(SEG_BV), seg(SEG_BG),
                  pl.BlockSpec((CHUNK, LANES), lambda i: (i, 0)),
                  const((CONV_W, 3 * GROUP_W)), const((1, LANES)), const((1, LANES)), const((1, HEAD_W)),
                  pl.BlockSpec((None, nseq, CONV_W - 1, 3 * GROUP_W), lambda i: (layer, i, 0, 0)),
                  pl.BlockSpec((None, nseq, N_HEADS, HEAD_W, HEAD_W), lambda i: (layer, i, 0, 0, 0))] + extra_specs,
        out_specs=[
            pl.BlockSpec((CHUNK, GROUP_W), lambda i: (i, 0)),
            pl.BlockSpec((None, nseq, N_HEADS, HEAD_W, HEAD_W), lambda i: (layer, i, 0, 0, 0)),
            pl.BlockSpec((nseq, CONV_W - 1, 3 * GROUP_W), lambda i: (i, 0, 0)),
        ],
        out_shape=[
            jax.ShapeDtypeStruct((bs * ls, GROUP_W), F32),
            jax.ShapeDtypeStruct(s0.shape, F32),
            jax.ShapeDtypeStruct(conv0.shape[1:], F32),
        ],
        scratch_shapes=[pltpu.VMEM((3, nseq * 16, GROUP_W), F32)],
        input_output_aliases=aliases,
        compiler_params=_cparams(("parallel",)),
        name="gdn_sample",
    )(p2, p2, p2, p2, g2, conv_w, alog_row, dt_row, dnw.reshape(1, HEAD_W), conv0, s0, *extra_in)


def _outproj_kernel(h_ref, oa_ref, ob_ref, ox_ref, w_ref, nf_ref, o_ref, *, final):
    acc = h_ref[...]
    for n, r in enumerate((oa_ref, ob_ref, ox_ref)):
        acc = acc + _dot(r[...].astype(BF16), w_ref[n * GROUP_W:(n + 1) * GROUP_W, :])
    o_ref[...] = _rms(acc, nf_ref[...]) if final else acc


def _outproj(h2d, oa, ob, ox, w_out, norm_f, final):
    t = h2d.shape[0]
    tm = min(t, 1024)
    row = lambda w: pl.BlockSpec((tm, w), lambda i: (i, 0))
    return pl.pallas_call(
        functools.partial(_outproj_kernel, final=final),
        grid=(t // tm,),
        in_specs=[row(D_MODEL), row(GROUP_W), row(GROUP_W), row(GROUP_W),
                  pl.BlockSpec((3 * GROUP_W, D_MODEL), lambda i: (0, 0)),
                  pl.BlockSpec((1, D_MODEL), lambda i: (0, 0))],
        out_specs=row(D_MODEL),
        out_shape=jax.ShapeDtypeStruct((t, D_MODEL), F32),
        compiler_params=_cparams(("parallel",)),
        name="outproj",
    )(h2d, oa, ob, ox, w_out, norm_f.reshape(1, D_MODEL))


def _pad_lanes(v):
    return jnp.pad(v.astype(F32), (0, LANES - v.shape[0])).reshape(1, LANES)


def kernel(x_prompt, x_sample, cache_k, cache_v, state_delta, state_conv, cache_mem_k, cache_mem_v, page_table, mem_prompt, norm_w, w_in, diff_lambda, diff_norm_w, conv_w, a_log, dt_bias, delta_norm_w, norm_mem, w_mem_kv, w_out, norm_f):
    bp, lp, _ = x_prompt.shape
    bs, ls, _ = x_sample.shape
    depth = w_in.shape[0]
    n_mem = mem_prompt.shape[1]
    n_gate = 2 * N_HEADS
    main_w = N_MAIN - 2 * GROUP_W

    w_main = jnp.concatenate([w_in[:, :, :main_w], w_in[:, :, main_w + n_gate:]], axis=2).astype(BF16)
    w_gate = jnp.pad(w_in[:, :, main_w:main_w + n_gate], ((0, 0), (0, 0), (0, LANES - n_gate))).astype(BF16)
    w_out_b = w_out.astype(BF16)
    cache_k4 = cache_k.reshape(cache_k.shape[0], cache_k.shape[1], PAGE * N_HEADS, HEAD_W)
    cache_v4 = cache_v.reshape(cache_v.shape[0], cache_v.shape[1], PAGE * N_HEADS, HEAD_W)
    mem_k4 = cache_mem_k.reshape(depth, bs, n_mem * N_HEADS, HEAD_W)
    mem_v4 = cache_mem_v.reshape(depth, bs, n_mem * N_HEADS, HEAD_W)

    mk_all, mv_all = _memkv(mem_prompt.reshape(bp * n_mem, D_MODEL), norm_mem, w_mem_kv.astype(BF16))

    hp = x_prompt.reshape(bp * lp, D_MODEL)
    hs = x_sample.reshape(bs * ls, D_MODEL)
    tq_x = min(lp, 512)
    sp, cp, cs = ([] for _ in range(3))
    kv_p = kv_s = s_all = None
    for l in range(depth):
        lam_init = 0.8 - 0.6 * math.exp(-0.3 * l)
        alog_row = _pad_lanes(a_log[l])
        dt_row = _pad_lanes(dt_bias[l])
        final = l == depth - 1

        pp, gp, pa16, *kv_p = _proj(hp, norm_w[l], w_main[l], w_gate[l], l, depth, kv_p)
        pp3 = pp.reshape(bp, lp, N_MAIN)
        oa = _attn_prompt(pp3, pa16.reshape(bp, lp, 4 * GROUP_W), diff_lambda[l], diff_norm_w[l], lam_init)
        ob, s_new, c_new = _gdn_prompt(pp3, gp.reshape(bp, lp, LANES), conv_w[l], alog_row, dt_row, delta_norm_w[l])
        ox = _cross_prompt(pp.reshape(bp, lp // tq_x, tq_x, N_MAIN), mk_all[l].reshape(bp, n_mem, GROUP_W),
                           mv_all[l].reshape(bp, n_mem, GROUP_W))
        hp = _outproj(hp, oa.reshape(bp * lp, GROUP_W), ob.reshape(bp * lp, GROUP_W),
                      ox.reshape(bp * lp, GROUP_W), w_out_b[l], norm_f, final)
        sp.append(s_new)
        cp.append(c_new)

        ps, gs, _, *kv_s = _proj(hs, norm_w[l], w_main[l], w_gate[l], l, depth, kv_s)
        ps3 = ps.reshape(bs, ls, N_MAIN)
        oa = _attn_sample(ps3, cache_k4, cache_v4, l, page_table, diff_lambda[l], diff_norm_w[l], lam_init)
        ob, s_all, c_new = _gdn_sample(ps, gs, ls, conv_w[l], alog_row, dt_row, delta_norm_w[l],
                                       state_conv, state_delta, l, s_all)
        ox = _cross_sample(ps3, mem_k4, mem_v4, l)
        hs = _outproj(hs, oa.reshape(bs * ls, GROUP_W), ob, ox.reshape(bs * ls, GROUP_W), w_out_b[l], norm_f, final)
        cs.append(c_new)

    heads = lambda x, b, n: x.reshape(depth, b, n, N_HEADS, HEAD_W)
    return (hp.reshape(bp, lp, D_MODEL), hs.reshape(bs, ls, D_MODEL),
            heads(kv_p[0], bp, lp), heads(kv_p[1], bp, lp), jnp.stack(sp), jnp.stack(cp),
            heads(mk_all, bp, n_mem), heads(mv_all, bp, n_mem),
            heads(kv_s[0], bs, ls), heads(kv_s[1], bs, ls), s_all, jnp.stack(cs))
```

```python
import functools
import math

import jax
import jax.numpy as jnp
import numpy as np
from jax import lax
from jax.experimental import pallas as pl
from jax.experimental.pallas import tpu as pltpu

F32 = jnp.float32
BF16 = jnp.bfloat16

D_MODEL = 1024
N_HEADS = 4
HEAD_W = 128
GROUP_W = N_HEADS * HEAD_W
A_DH = 64
CONV_W = 4
CHUNK = 64
GDN_GROUP = 4
PAGE = 128
EPS = 1e-6
NEG_INF = -1e30
LOG2E = math.log2(math.e)
N_MAIN = 10 * GROUP_W
N_A = 4 * GROUP_W
N_P = N_MAIN - N_A
LANES = 128
VMEM_LIMIT = 48 * 1024 * 1024

SEG_AQ, SEG_AK, SEG_AV, SEG_AG = range(4)
SEG_BQ, SEG_BK, SEG_BV, SEG_BG, SEG_XQ, SEG_XG = range(6)


def _cparams(sem):
    return pltpu.CompilerParams(dimension_semantics=sem, vmem_limit_bytes=VMEM_LIMIT)


def _iota(shape, dim):
    return lax.broadcasted_iota(jnp.int32, shape, dim)


def _div(x, d):
    assert d & (d - 1) == 0
    return x >> (d.bit_length() - 1)


def _sigmoid(x):
    return 1.0 / (1.0 + jnp.exp(-x))


def _silu(x):
    return x * _sigmoid(x)


def _softplus(x):
    return jnp.maximum(x, 0.0) + jnp.log(1.0 + jnp.exp(-jnp.abs(x)))


def _dot(a, b):
    return jnp.dot(a, b, preferred_element_type=F32)


def _dot_nt(a, b):
    return lax.dot_general(a, b, (((1,), (1,)), ((), ())), preferred_element_type=F32)


def _rms(x, w):
    ms = jnp.mean(x * x, axis=-1, keepdims=True)
    return x * lax.rsqrt(ms + EPS) * w


def _lambda_value(p, lam_init):
    a = jnp.sum(p[0:1, :] * p[1:2, :], axis=-1, keepdims=True)
    b = jnp.sum(p[2:3, :] * p[3:4, :], axis=-1, keepdims=True)
    return jnp.exp(a) - jnp.exp(b) + lam_init


def _proj_kernel(x_ref, nw_ref, w_ref, wg_ref, *rest, n16, tm):
    p_ref, g_ref, a16_ref, k_out, v_out, hn_ref = rest[-6:]
    j = pl.program_id(1)

    @pl.when(j == 0)
    def _():
        hn = _rms(x_ref[...], nw_ref[...]).astype(BF16)
        hn_ref[...] = hn
        g_ref[...] = _dot(hn, wg_ref[...])

    res = _dot(hn_ref[...], w_ref[...])

    @pl.when(j < n16)
    def _():
        a16_ref[...] = res.astype(BF16)

    @pl.when(j >= n16)
    def _():
        p_ref[...] = res

    def scatter_heads(out_ref, cols):
        for h in range(N_HEADS):
            out_ref[pl.ds(h, tm, stride=N_HEADS), :] = cols[:, h * HEAD_W:(h + 1) * HEAD_W]

    @pl.when(j == 0)
    def _():
        scatter_heads(k_out, res[:, GROUP_W:2 * GROUP_W])

    @pl.when(j == 1)
    def _():
        scatter_heads(v_out, res[:, 0:GROUP_W])


def _proj(x2d, norm_w, w_main, w_gate, layer, depth, kv_all):
    t = x2d.shape[0]
    tm = min(t, 1024)
    tn = 1024
    n16 = N_A // tn
    kv_spec = pl.BlockSpec((None, N_HEADS * tm, HEAD_W), lambda i, j: (layer, i, 0))
    kv_shape = jax.ShapeDtypeStruct((depth, N_HEADS * t, HEAD_W), F32)
    n_in = 4
    extra_in, extra_specs, aliases = (), [], {}
    if kv_all is not None:
        extra_in = tuple(kv_all)
        extra_specs = [pl.BlockSpec(memory_space=pl.ANY)] * 2
        aliases = {n_in: 3, n_in + 1: 4}
    return pl.pallas_call(
        functools.partial(_proj_kernel, n16=n16, tm=tm),
        grid=(t // tm, N_MAIN // tn),
        in_specs=[
            pl.BlockSpec((tm, D_MODEL), lambda i, j: (i, 0)),
            pl.BlockSpec((1, D_MODEL), lambda i, j: (0, 0)),
            pl.BlockSpec((D_MODEL, tn), lambda i, j: (0, j)),
            pl.BlockSpec((D_MODEL, LANES), lambda i, j: (0, 0)),
        ] + extra_specs,
        out_specs=[
            pl.BlockSpec((tm, tn), lambda i, j: (i, jnp.maximum(j - n16, 0))),
            pl.BlockSpec((tm, LANES), lambda i, j: (i, 0)),
            pl.BlockSpec((tm, tn), lambda i, j: (i, jnp.minimum(j, n16 - 1))),
            kv_spec, kv_spec,
        ],
        out_shape=[jax.ShapeDtypeStruct((t, N_P), F32), jax.ShapeDtypeStruct((t, LANES), F32),
                   jax.ShapeDtypeStruct((t, N_A), BF16), kv_shape, kv_shape],
        scratch_shapes=[pltpu.VMEM((tm, D_MODEL), BF16)],
        input_output_aliases=aliases,
        compiler_params=_cparams(("parallel", "arbitrary")),
        name="proj",
    )(x2d, norm_w.reshape(1, D_MODEL), w_main, w_gate, *extra_in)


def _memkv_kernel(x_ref, nw_ref, w_ref, k_ref, v_ref):
    hn = _rms(x_ref[...], nw_ref[...]).astype(BF16)
    kv = _dot(hn, w_ref[...])
    k_ref[...] = kv[:, :GROUP_W]
    v_ref[...] = kv[:, GROUP_W:]


def _memkv(mem2d, norm_mem, w_mem_kv_bf16):
    depth = norm_mem.shape[0]
    t = mem2d.shape[0]
    return pl.pallas_call(
        _memkv_kernel,
        grid=(depth,),
        in_specs=[
            pl.BlockSpec((t, D_MODEL), lambda l: (0, 0)),
            pl.BlockSpec((None, 1, D_MODEL), lambda l: (l, 0, 0)),
            pl.BlockSpec((None, D_MODEL, 2 * GROUP_W), lambda l: (l, 0, 0)),
        ],
        out_specs=[
            pl.BlockSpec((None, t, GROUP_W), lambda l: (l, 0, 0)),
            pl.BlockSpec((None, t, GROUP_W), lambda l: (l, 0, 0)),
        ],
        out_shape=[jax.ShapeDtypeStruct((depth, t, GROUP_W), F32)] * 2,
        compiler_params=_cparams(("parallel",)),
        name="memkv",
    )(mem2d, norm_mem.reshape(depth, 1, D_MODEL), w_mem_kv_bf16)


def _alibi_slope(h):
    return jnp.where(h == 0, 0.25, jnp.where(h == 1, 0.0625, jnp.where(h == 2, 0.015625, 0.00390625))).astype(F32)


def _diff_epilogue(o1, o2, lam, dnw, lam_scale, gate):
    o = o1 - lam * o2
    return _rms(o, dnw) * lam_scale * _silu(gate)


def _bf16_split3(x):
    parts = []
    for _ in range(3):
        p = float(np.asarray(x, np.float32).astype(jnp.bfloat16).astype(np.float32))
        parts.append(p)
        x = x - p
    return parts


_ALIBI_PARTS = [_bf16_split3(2.0 ** (-2.0 * (h + 1)) * LOG2E) for h in range(N_HEADS)]
POS_LO = 128


def _by_head(h, values):
    out = jnp.float32(values[-1])
    for idx in range(len(values) - 2, -1, -1):
        out = jnp.where(h == idx, jnp.float32(values[idx]), out)
    return out


def _sample_attention(x_ref, k_refs, v_refs, lam, dnw, lam_scale, ls):
    n_pages = len(k_refs)
    past = n_pages * PAGE
    rows = 2 * ls * N_HEADS
    x = x_ref[...].astype(F32)
    q = x[:, 0:GROUP_W] * (A_DH ** -0.5 * LOG2E)
    k_new = x[:, GROUP_W:2 * GROUP_W]
    v_new = x[:, 2 * GROUP_W:3 * GROUP_W]
    gate = x[:, 3 * GROUP_W:4 * GROUP_W]

    lane_q = _iota((ls, HEAD_W), 1)
    parts = []
    for h in range(N_HEADS):
        qh = q[:, h * HEAD_W:(h + 1) * HEAD_W]
        parts += [jnp.where(lane_q < A_DH, qh, 0.0), jnp.where(lane_q >= A_DH, qh, 0.0)]
    qs = jnp.concatenate(parts, axis=0)
    row_head_w = _div(_iota((rows, GROUP_W), 0), 2 * ls)
    qb = jnp.where(row_head_w == _div(_iota((rows, GROUP_W), 1), HEAD_W),
                   jnp.concatenate([qs] * N_HEADS, axis=1), 0.0).astype(BF16)
    qs = qs.astype(BF16)

    row_head = _div(_iota((rows, 1), 0), 2 * ls)
    slope2 = _alibi_slope(row_head) * LOG2E
    tok = _iota((rows, 1), 0) & (ls - 1)

    col = _iota((1, N_HEADS * PAGE), 1)
    own = (col & (N_HEADS - 1)) == row_head
    key_in_page = _div(col, N_HEADS)
    s_pages = []
    for p in range(n_pages):
        s = _dot_nt(qs, k_refs[p][...].astype(BF16))
        s = s + slope2 * (key_in_page + (p * PAGE - past)).astype(F32)
        s_pages.append(jnp.where(own, s, NEG_INF))
    lane = _iota((1, PAGE), 1)
    pad = jnp.zeros((PAGE - ls, GROUP_W), F32)
    k_pad = jnp.concatenate([k_new, pad], axis=0).astype(BF16)
    v_pad = jnp.concatenate([v_new, pad], axis=0).astype(BF16)
    s_new = _dot_nt(qb, k_pad) + slope2 * lane.astype(F32)
    s_new = jnp.where(lane <= tok, s_new, NEG_INF)

    m = s_new.max(axis=-1, keepdims=True)
    for s in s_pages:
        m = jnp.maximum(m, s.max(axis=-1, keepdims=True))
    e = jnp.exp2(s_new - m)
    l = jnp.sum(e, axis=-1, keepdims=True)
    o_new = _dot(e.astype(BF16), v_pad)
    acc = jnp.zeros((rows, HEAD_W), F32)
    for h in range(N_HEADS):
        acc = acc + jnp.where(row_head == h, o_new[:, h * HEAD_W:(h + 1) * HEAD_W], 0.0)
    for p, s in enumerate(s_pages):
        e = jnp.exp2(s - m)
        l = l + jnp.sum(e, axis=-1, keepdims=True)
        acc = acc + _dot(e.astype(BF16), v_refs[p][...].astype(BF16))
    o = acc / l

    outs = []
    for h in range(N_HEADS):
        o1 = o[2 * ls * h:2 * ls * h + ls]
        o2 = o[2 * ls * h + ls:2 * ls * (h + 1)]
        outs.append(_diff_epilogue(o1, o2, lam, dnw, lam_scale, gate[:, h * HEAD_W:(h + 1) * HEAD_W]))
    return jnp.concatenate(outs, axis=1)


def _attn_kernel(pt_ref, q_ref, k_ref, v_ref, g_ref, lamp_ref, dnw_ref, xs_ref, *rest, tq, n_pages, ls, lam_init):
    del pt_ref
    k_pages = rest[:n_pages]
    v_pages = rest[n_pages:2 * n_pages]
    o_ref, os_ref, kf_ref, qs_ref, s_a, s_b, m_ref, l_ref, acc_ref = rest[2 * n_pages:]
    h = pl.program_id(1)
    i = pl.program_id(2)
    tk = tq
    n_lane_tiles = tk // LANES
    seq = k_ref.shape[0]
    lam = _lambda_value(lamp_ref[...], lam_init)

    os_ref[...] = _sample_attention(xs_ref, k_pages, v_pages, lam, dnw_ref[...], 1.0 - lam_init, ls)

    @pl.when(i == 0)
    def _():
        def fill(r, carry):
            rows = pl.ds(pl.multiple_of(r * tk, tk), tk)
            k = k_ref[rows, :].astype(F32)
            lane = _iota((tk, HEAD_W), 1)
            pos = r * tk + _iota((tk, HEAD_W), 0)
            hi = _div(pos, POS_LO).astype(F32)
            lo = (pos & (POS_LO - 1)).astype(F32)
            for mp in range(2):
                rel = lane - (A_DH if mp == 0 else 0)
                feat = jnp.where(rel < 3, hi, jnp.where(rel < 6, lo, 0.0))
                own = (lane < A_DH) if mp == 0 else (lane >= A_DH)
                kf_ref[mp, rows, :] = jnp.where(own, k, feat).astype(BF16)
            return carry

        lax.fori_loop(0, seq // tk, fill, 0)

    q = q_ref[...].astype(F32) * (A_DH ** -0.5 * LOG2E)
    lane = _iota((1, HEAD_W), 1)
    c = [_by_head(h, [_ALIBI_PARTS[hh][part] for hh in range(N_HEADS)]) for part in range(3)]
    for mp in range(2):
        rel = lane - (A_DH if mp == 0 else 0)
        qfeat = jnp.zeros((1, HEAD_W), F32)
        for part in range(3):
            qfeat = jnp.where(rel == part, c[part] * POS_LO, jnp.where(rel == 3 + part, c[part], qfeat))
        own = (lane < A_DH) if mp == 0 else (lane >= A_DH)
        qs_ref[mp] = jnp.where(own, q, qfeat).astype(BF16)
    m_ref[...] = jnp.full(m_ref.shape, NEG_INF, F32)
    l_ref[...] = jnp.zeros(l_ref.shape, F32)
    acc_ref[...] = jnp.zeros(acc_ref.shape, F32)

    def scores(mp, j):
        return _dot_nt(qs_ref[mp], kf_ref[mp, pl.ds(pl.multiple_of(j * tk, tk), tk), :])

    def update(mp, s_ref, j, diagonal):
        def tile(cidx):
            s = s_ref[:, cidx * LANES:(cidx + 1) * LANES]
            if diagonal:
                s = jnp.where(cidx * LANES + _iota((1, LANES), 1) <= _iota((tq, 1), 0), s, NEG_INF)
            return s

        mx = tile(0)
        for cidx in range(1, n_lane_tiles):
            mx = jnp.maximum(mx, tile(cidx))
        m_prev = m_ref[mp]
        m_new = jnp.maximum(m_prev, jnp.max(mx, axis=-1, keepdims=True))
        alpha = jnp.exp2(m_prev - m_new)
        lsum = alpha * l_ref[mp]
        ps = []
        for cidx in range(n_lane_tiles):
            p = jnp.exp2(tile(cidx) - m_new)
            lsum = lsum + p
            ps.append(p.astype(BF16))
        v = v_ref[pl.ds(pl.multiple_of(j * tk, tk), tk), :]
        acc_ref[mp] = alpha * acc_ref[mp] + _dot(jnp.concatenate(ps, axis=1), v)
        l_ref[mp] = lsum
        m_ref[mp] = m_new

    s_a[...] = scores(0, 0)

    def step(j):
        s_b[...] = scores(1, j)
        update(0, s_a, j, False)
        s_a[...] = scores(0, j + 1)
        update(1, s_b, j, False)

    def pair(jj, carry):
        step(2 * jj)
        step(2 * jj + 1)
        return carry

    lax.fori_loop(0, i >> 1, pair, 0)

    @pl.when((i & 1) == 1)
    def _():
        step(i - 1)

    s_b[...] = scores(1, i)
    update(0, s_a, i, True)
    update(1, s_b, i, True)

    o1 = acc_ref[0] / jnp.sum(l_ref[0], axis=-1, keepdims=True)
    o2 = acc_ref[1] / jnp.sum(l_ref[1], axis=-1, keepdims=True)
    y = _diff_epilogue(o1, o2, lam, dnw_ref[...], 1.0 - lam_init, g_ref[...].astype(F32))
    o_ref[...] = y.astype(o_ref.dtype)


def _attn(a16, xs16, cache_k4, cache_v4, layer, page_table, lam_p, dnw, lam_init):
    b, l, _ = a16.shape
    bs, ls, _ = xs16.shape
    n_pages = page_table.shape[1]
    tq = min(l, 512)
    nq = l // tq
    assert l <= POS_LO * 256 and tq % LANES == 0
    assert b * N_HEADS * nq == bs, "one sample sequence per prompt grid step"

    def step_id(bi, h, i):
        return (bi * N_HEADS + h) * nq + i

    def page_map(p):
        return lambda bi, h, i, pt: (layer, pt[step_id(bi, h, i) * n_pages + p], 0, 0)

    page_specs = lambda: [pl.BlockSpec((None, None, N_HEADS * PAGE, HEAD_W), page_map(p)) for p in range(n_pages)]
    grid_spec = pltpu.PrefetchScalarGridSpec(
        num_scalar_prefetch=1,
        grid=(b, N_HEADS, nq),
        in_specs=[
            pl.BlockSpec((None, tq, HEAD_W), lambda bi, h, i, pt: (bi, i, SEG_AQ * N_HEADS + h)),
            pl.BlockSpec((None, l, HEAD_W), lambda bi, h, i, pt: (bi, 0, SEG_AK * N_HEADS + h)),
            pl.BlockSpec((None, l, HEAD_W), lambda bi, h, i, pt: (bi, 0, SEG_AV * N_HEADS + h)),
            pl.BlockSpec((None, tq, HEAD_W), lambda bi, h, i, pt: (bi, i, SEG_AG * N_HEADS + h)),
            pl.BlockSpec((4, A_DH), lambda bi, h, i, pt: (0, 0)),
            pl.BlockSpec((1, HEAD_W), lambda bi, h, i, pt: (0, 0)),
            pl.BlockSpec((None, ls, N_A), lambda bi, h, i, pt: (step_id(bi, h, i), 0, 0)),
        ] + page_specs() + page_specs(),
        out_specs=[
            pl.BlockSpec((None, tq, HEAD_W), lambda bi, h, i, pt: (bi, i, h)),
            pl.BlockSpec((None, ls, GROUP_W), lambda bi, h, i, pt: (step_id(bi, h, i), 0, 0)),
        ],
        scratch_shapes=[
            pltpu.VMEM((2, l, HEAD_W), BF16),
            pltpu.VMEM((2, tq, HEAD_W), BF16),
            pltpu.VMEM((tq, tq), F32),
            pltpu.VMEM((tq, tq), F32),
            pltpu.VMEM((2, tq, LANES), F32),
            pltpu.VMEM((2, tq, LANES), F32),
            pltpu.VMEM((2, tq, HEAD_W), F32),
        ],
    )
    return pl.pallas_call(
        functools.partial(_attn_kernel, tq=tq, n_pages=n_pages, ls=ls, lam_init=lam_init),
        grid_spec=grid_spec,
        out_shape=[jax.ShapeDtypeStruct((b, l, GROUP_W), BF16), jax.ShapeDtypeStruct((bs, ls, GROUP_W), F32)],
        compiler_params=_cparams(("parallel", "parallel", "arbitrary")),
        name="attn",
    )(page_table.reshape(-1), a16, a16, a16, a16, lam_p, dnw.reshape(1, HEAD_W), xs16,
      *([cache_k4] * n_pages), *([cache_v4] * n_pages))


def _cross_kernel(q_ref, g_ref, mk_ref, mv_ref, o_ref, *, nseq):
    for s in range(nseq):
        q = q_ref[s] * (HEAD_W ** -0.5)
        gate = g_ref[s]
        mk = mk_ref[s].astype(BF16)
        mv = mv_ref[s].astype(BF16)
        outs = []
        for h in range(N_HEADS):
            sl = slice(h * HEAD_W, (h + 1) * HEAD_W)
            sc = _dot_nt(q[:, sl].astype(BF16), mk[:, sl])
            e = jnp.exp(sc - sc.max(axis=-1, keepdims=True))
            o = _dot(e.astype(BF16), mv[:, sl]) / jnp.sum(e, axis=-1, keepdims=True)
            outs.append(o * _silu(gate[:, sl]))
        o_ref[s] = jnp.concatenate(outs, axis=1).astype(o_ref.dtype)


def _cross_prompt(p4, mk, mv):
    nb, nt, tq, _ = p4.shape
    n_mem = mk.shape[1]
    return pl.pallas_call(
        functools.partial(_cross_kernel, nseq=1),
        grid=(nb, nt),
        in_specs=[
            pl.BlockSpec((1, None, tq, GROUP_W), lambda b, i: (b, i, 0, SEG_XQ)),
            pl.BlockSpec((1, None, tq, GROUP_W), lambda b, i: (b, i, 0, SEG_XG)),
            pl.BlockSpec((1, n_mem, GROUP_W), lambda b, i: (b, 0, 0)),
            pl.BlockSpec((1, n_mem, GROUP_W), lambda b, i: (b, 0, 0)),
        ],
        out_specs=pl.BlockSpec((1, None, tq, GROUP_W), lambda b, i: (b, i, 0, 0)),
        out_shape=jax.ShapeDtypeStruct((nb, nt, tq, GROUP_W), BF16),
        compiler_params=_cparams(("parallel", "parallel")),
        name="cross_prompt",
    )(p4, p4, mk, mv)


def _cross_sample_kernel(q_ref, g_ref, mk_ref, mv_ref, o_ref, *, nseq, ls):
    rows = N_HEADS * ls
    n_rows = mk_ref.shape[1]
    own = (_iota((1, n_rows), 1) & (N_HEADS - 1)) == _div(_iota((rows, 1), 0), ls)
    for s in range(nseq):
        q = q_ref[s] * (HEAD_W ** -0.5)
        gate = g_ref[s]
        qx = jnp.concatenate([q[:, h * HEAD_W:(h + 1) * HEAD_W] for h in range(N_HEADS)], axis=0)
        sc = jnp.where(own, _dot_nt(qx.astype(BF16), mk_ref[s].astype(BF16)), NEG_INF)
        e = jnp.exp(sc - sc.max(axis=-1, keepdims=True))
        o = _dot(e.astype(BF16), mv_ref[s].astype(BF16)) / jnp.sum(e, axis=-1, keepdims=True)
        o = jnp.concatenate([o[h * ls:(h + 1) * ls] for h in range(N_HEADS)], axis=1)
        o_ref[s] = o * _silu(gate)


def _cross_sample(p3, mk, mv, layer):
    bs, ls, _ = p3.shape
    n_rows = mk.shape[2]
    nseq = 8
    return pl.pallas_call(
        functools.partial(_cross_sample_kernel, nseq=nseq, ls=ls),
        grid=(bs // nseq,),
        in_specs=[
            pl.BlockSpec((nseq, ls, GROUP_W), lambda b: (b, 0, SEG_XQ)),
            pl.BlockSpec((nseq, ls, GROUP_W), lambda b: (b, 0, SEG_XG)),
            pl.BlockSpec((None, nseq, n_rows, HEAD_W), lambda b: (layer, b, 0, 0)),
            pl.BlockSpec((None, nseq, n_rows, HEAD_W), lambda b: (layer, b, 0, 0)),
        ],
        out_specs=pl.BlockSpec((nseq, ls, GROUP_W), lambda b: (b, 0, 0)),
        out_shape=jax.ShapeDtypeStruct((bs, ls, GROUP_W), F32),
        compiler_params=_cparams(("parallel",)),
        name="cross_sample",
    )(p3, p3, mk, mv)


def _stack_heads(a):
    return jnp.concatenate([a[:, h * HEAD_W:(h + 1) * HEAD_W] for h in range(N_HEADS)], axis=0)


def _col(x, lane_idx):
    lane = _iota(x.shape, 1)
    return jnp.sum(jnp.where(lane == lane_idx, x, 0.0), axis=-1, keepdims=True)


def _split3_dot(mat_bf16, x):
    hi = x.astype(BF16)
    r1 = x - hi.astype(F32)
    mid = r1.astype(BF16)
    lo = (r1 - mid.astype(F32)).astype(BF16)
    return _dot(mat_bf16, hi) + _dot(mat_bf16, mid) + _dot(mat_bf16, lo)


def _gdn_blocks(blocks, alog_row, dt_row, blk):
    n = N_HEADS * CHUNK
    sh = int(math.log2(blk))
    lane = _iota((1, LANES), 1)
    neg_a = jnp.where(lane < N_HEADS, -jnp.exp(alog_row), 0.0)
    ti = _iota((CHUNK, CHUNK), 0)
    tj = _iota((CHUNK, CHUNK), 1)
    same_t = (ti >> sh) == (tj >> sh)
    tri_incl = (same_t & (tj <= ti)).astype(BF16)
    tri_all = same_t.astype(BF16)
    ri = _iota((n, n), 0)
    ci = _iota((n, n), 1)
    same = (ri >> sh) == (ci >> sh)
    incl = same & (ci <= ri)
    strict = same & (ci < ri)

    st = []
    for qkv, ab in blocks:
        q = _stack_heads(qkv[0])
        k = _stack_heads(qkv[1])
        v = _stack_heads(qkv[2])
        qn = q * lax.rsqrt(jnp.sum(q * q, axis=-1, keepdims=True) + EPS) * (HEAD_W ** -0.5)
        kn = k * lax.rsqrt(jnp.sum(k * k, axis=-1, keepdims=True) + EPS)
        g = neg_a * _softplus(ab + dt_row)
        beta_full = _sigmoid(ab)
        gcum = _split3_dot(tri_incl, g)
        gtot = _split3_dot(tri_all, g)
        gc = jnp.concatenate([_col(gcum, h) for h in range(N_HEADS)], axis=0)
        gl = jnp.concatenate([_col(gtot, h) for h in range(N_HEADS)], axis=0)
        beta = jnp.concatenate([_col(beta_full, N_HEADS + h) for h in range(N_HEADS)], axis=0)
        gt = jnp.concatenate([gcum, gcum], axis=0).T
        gr = jnp.concatenate([jnp.where(lane < CHUNK, gt[0:1, :], gt[1:2, :]),
                              jnp.where(lane < CHUNK, gt[2:3, :], gt[3:4, :])], axis=1)
        decay = jnp.where(incl, jnp.exp(jnp.where(incl, gc - gr, 0.0)), 0.0)
        kb = kn * beta
        eg = jnp.exp(gc)
        st.append(dict(qg=qn * eg, kn=kn, gl=gl, gc=gc, decay=decay,
                       lhs=jnp.concatenate([kb, qn], axis=0).astype(BF16),
                       rhs=jnp.concatenate([v * beta, kb * eg], axis=1).astype(BF16)))

    for s in st:
        kq = _dot_nt(s["lhs"], s["kn"].astype(BF16))
        s["nmat"] = jnp.where(strict, kq[0:n] * s["decay"], 0.0)
        s["qkm"] = kq[n:2 * n] * s["decay"]

    pair = ((ri >> 1) == (ci >> 1)) & ((ri & 1) == 1) & ((ci & 1) == 0)
    eye = jnp.where(ri == ci, 1.0, 0.0)
    for s in st:
        s["x"] = eye - jnp.where(pair, s["nmat"], 0.0)
    b = 2
    while b < blk:
        sb = int(math.log2(2 * b))
        mask = ((ri >> sb) == (ci >> sb)) & ((ri & (2 * b - 1)) >= b) & ((ci & (2 * b - 1)) < b)
        for s in st:
            s["xb"] = s["x"].astype(BF16)
            s["t"] = _dot(s["xb"], jnp.where(mask, s["nmat"], 0.0).astype(BF16)).astype(BF16)
        for s in st:
            s["x"] = s["x"] - _dot(s["t"], s["xb"])
        b *= 2

    outs = []
    for s in st:
        sol = _dot(s["x"].astype(BF16), s["rhs"])
        kdec = s["kn"] * jnp.exp(s["gl"] - s["gc"])
        outs.append((sol[:, 0:HEAD_W], sol[:, HEAD_W:2 * HEAD_W], s["qg"], s["qkm"], kdec, s["gl"]))
    return outs


def _pad_t(a):
    return jnp.concatenate([a, jnp.zeros((LANES - CHUNK, LANES), F32)], axis=0).T


def _pad_rows(a):
    return jnp.concatenate([a, jnp.zeros((LANES - CHUNK, LANES), F32)], axis=0)


def _gdn_finish(o, gate, dnw):
    outs = []
    for h in range(N_HEADS):
        oh = o[h * CHUNK:(h + 1) * CHUNK]
        outs.append(_rms(oh, dnw) * _silu(gate[:, h * HEAD_W:(h + 1) * HEAD_W]))
    return jnp.concatenate(outs, axis=1)


def _conv_taps(xbuf, p, base, nrows, cw):
    acc = xbuf[p, pl.ds(base + 5, nrows), :] * cw[0:1, p * GROUP_W:(p + 1) * GROUP_W]
    for t in range(1, CONV_W):
        acc = acc + xbuf[p, pl.ds(base + 5 + t, nrows), :] * cw[t:t + 1, p * GROUP_W:(p + 1) * GROUP_W]
    return _silu(acc)


def _gdnp_kernel(bq_ref, bk_ref, bv_ref, bg_ref, ab_ref, cw_ref, alog_ref, dt_ref, dnw_ref,
                 o_ref, sout_ref, cout_ref, xbuf, s_scr, *, tb):
    t = pl.program_id(1)

    @pl.when(t == 0)
    def _():
        xbuf[:, 0:8, :] = jnp.zeros((3, 8, GROUP_W), F32)
        s_scr[...] = jnp.zeros(s_scr.shape, F32)

    x_refs = (bq_ref, bk_ref, bv_ref)
    for p in range(3):
        xbuf[p, 8:8 + tb, :] = x_refs[p][...]
    cw = cw_ref[...]
    qkv = [_conv_taps(xbuf, p, 0, tb, cw) for p in range(3)]
    for p in range(3):
        xbuf[p, 5:8, :] = xbuf[p, tb + 5:tb + 8, :]

    gate = bg_ref[...]
    ab = ab_ref[...]
    n_chunks = tb // CHUNK
    pre = []
    for c0 in range(0, n_chunks, GDN_GROUP):
        rows = [slice(c * CHUNK, (c + 1) * CHUNK) for c in range(c0, min(c0 + GDN_GROUP, n_chunks))]
        pre += _gdn_blocks([([a[rs] for a in qkv], ab[rs]) for rs in rows], alog_ref[...], dt_ref[...], CHUNK)
    for c in range(n_chunks):
        rs = slice(c * CHUNK, (c + 1) * CHUNK)
        u, w, qg, qkm, kdec, gl = pre[c]
        vnews, qss = [], []
        for h in range(N_HEADS):
            hs = slice(h * CHUNK, (h + 1) * CHUNK)
            s_old = s_scr[h]
            wq = jnp.concatenate([w[hs], qg[hs]], axis=0).astype(BF16)
            r = _dot(wq, s_old.astype(BF16))
            vn = u[hs] - r[0:CHUNK]
            vnews.append(vn)
            qss.append(r[CHUNK:2 * CHUNK])
            upd = _dot(_pad_t(kdec[hs]).astype(BF16), _pad_rows(vn).astype(BF16))
            s_scr[h] = s_old * jnp.exp(gl[h * CHUNK:h * CHUNK + 1, :]) + upd
        vnew = jnp.concatenate(vnews, axis=0)
        o = jnp.concatenate(qss, axis=0) + _dot(qkm.astype(BF16), vnew.astype(BF16))
        o_ref[rs, :] = _gdn_finish(o, gate[rs], dnw_ref[...]).astype(o_ref.dtype)

    @pl.when(t == pl.num_programs(1) - 1)
    def _():
        sout_ref[...] = s_scr[...]
        for p in range(3):
            cout_ref[:, p * GROUP_W:(p + 1) * GROUP_W] = xbuf[p, 5:8, :]


def _gdn_prompt(p3, g3, conv_w, alog_row, dt_row, dnw):
    b, l, _ = p3.shape
    tb = min(l, 512)
    seg = lambda s: pl.BlockSpec((None, tb, GROUP_W), lambda bi, t: (bi, t, s))
    const = lambda shape: pl.BlockSpec(shape, lambda bi, t: (0,) * len(shape))
    return pl.pallas_call(
        functools.partial(_gdnp_kernel, tb=tb),
        grid=(b, l // tb),
        in_specs=[seg(SEG_BQ), seg(SEG_BK), seg(SEG_BV), seg(SEG_BG),
                  pl.BlockSpec((None, tb, LANES), lambda bi, t: (bi, t, 0)),
                  const((CONV_W, 3 * GROUP_W)), const((1, LANES)), const((1, LANES)), const((1, HEAD_W))],
        out_specs=[
            pl.BlockSpec((None, tb, GROUP_W), lambda bi, t: (bi, t, 0)),
            pl.BlockSpec((None, N_HEADS, HEAD_W, HEAD_W), lambda bi, t: (bi, 0, 0, 0)),
            pl.BlockSpec((None, CONV_W - 1, 3 * GROUP_W), lambda bi, t: (bi, 0, 0)),
        ],
        out_shape=[
            jax.ShapeDtypeStruct((b, l, GROUP_W), BF16),
            jax.ShapeDtypeStruct((b, N_HEADS, HEAD_W, HEAD_W), F32),
            jax.ShapeDtypeStruct((b, CONV_W - 1, 3 * GROUP_W), F32),
        ],
        scratch_shapes=[pltpu.VMEM((3, tb + 8, GROUP_W), F32), pltpu.VMEM((N_HEADS, HEAD_W, HEAD_W), F32)],
        compiler_params=_cparams(("parallel", "arbitrary")),
        name="gdn_prompt",
    )(p3, p3, p3, p3, g3, conv_w, alog_row, dt_row, dnw.reshape(1, HEAD_W))


def _gdns_kernel(bq_ref, bk_ref, bv_ref, bg_ref, ab_ref, cw_ref, alog_ref, dt_ref, dnw_ref, c0_ref, s0_ref,
                 *rest, ls, nseq):
    o_ref, sout_ref, cout_ref, xbuf = rest[-4:]
    x_refs = (bq_ref, bk_ref, bv_ref)
    cw = cw_ref[...]
    qkv = []
    for p in range(3):
        for s in range(nseq):
            xbuf[p, s * 16 + 5:s * 16 + 8, :] = c0_ref[s, :, p * GROUP_W:(p + 1) * GROUP_W]
            xbuf[p, s * 16 + 8:s * 16 + 8 + ls, :] = x_refs[p][s * ls:(s + 1) * ls, :]
        qkv.append(jnp.concatenate([_conv_taps(xbuf, p, s * 16, ls, cw) for s in range(nseq)], axis=0))
        for s in range(nseq):
            cout_ref[s, :, p * GROUP_W:(p + 1) * GROUP_W] = xbuf[p, s * 16 + 5 + ls:s * 16 + 8 + ls, :]

    (u, w, qg, qkm, kdec, gl), = _gdn_blocks([(qkv, ab_ref[...])], alog_ref[...], dt_ref[...], ls)
    seq_of_row = _div(_iota((2 * CHUNK, 1), 0) & (CHUNK - 1), ls)
    vnews, qss = [], []
    for h in range(N_HEADS):
        hs = slice(h * CHUNK, (h + 1) * CHUNK)
        wq = jnp.concatenate([w[hs], qg[hs]], axis=0)
        r = jnp.zeros((2 * CHUNK, HEAD_W), F32)
        for s in range(nseq):
            r = r + _dot(jnp.where(seq_of_row == s, wq, 0.0).astype(BF16), s0_ref[s, h].astype(BF16))
        vn = u[hs] - r[0:CHUNK]
        vnews.append(vn)
        qss.append(r[CHUNK:2 * CHUNK])
        vn_pad = _pad_rows(vn).astype(BF16)
        for s in range(nseq):
            kd = jnp.where(seq_of_row[0:CHUNK] == s, kdec[hs], 0.0)
            row = h * CHUNK + s * ls
            sout_ref[s, h] = s0_ref[s, h] * jnp.exp(gl[row:row + 1, :]) + _dot(_pad_t(kd).astype(BF16), vn_pad)
    vnew = jnp.concatenate(vnews, axis=0)
    o = jnp.concatenate(qss, axis=0) + _dot(qkm.astype(BF16), vnew.astype(BF16))
    o_ref[...] = _gdn_finish(o, bg_ref[...], dnw_ref[...]).astype(o_ref.dtype)


def _gdn_sample(p2, g2, ls, conv_w, alog_row, dt_row, dnw, conv0, s0, layer, s_all):
    bs = s0.shape[1]
    nseq = CHUNK // ls
    seg = lambda s: pl.BlockSpec((CHUNK, GROUP_W), lambda i: (i, s))
    const = lambda shape: pl.BlockSpec(shape, lambda i: (0,) * len(shape))
    n_in = 11
    extra_in, extra_specs, aliases = (), [], {}
    if s_all is not None:
        extra_in, extra_specs, aliases = (s_all,), [pl.BlockSpec(memory_space=pl.ANY)], {n_in: 1}
    return pl.pallas_call(
        functools.partial(_gdns_kernel, ls=ls, nseq=nseq),
        grid=(bs // nseq,),
        in_specs=[seg(SEG_BQ), seg(SEG_BK), seg(SEG_BV), seg(SEG_BG),
                  pl.BlockSpec((CHUNK, LANES), lambda i: (i, 0)),
                  const((CONV_W, 3 * GROUP_W)), const((1, LANES)), const((1, LANES)), const((1, HEAD_W)),
                  pl.BlockSpec((None, nseq, CONV_W - 1, 3 * GROUP_W), lambda i: (layer, i, 0, 0)),
                  pl.BlockSpec((None, nseq, N_HEADS, HEAD_W, HEAD_W), lambda i: (layer, i, 0, 0, 0))] + extra_specs,
        out_specs=[
            pl.BlockSpec((CHUNK, GROUP_W), lambda i: (i, 0)),
            pl.BlockSpec((None, nseq, N_HEADS, HEAD_W, HEAD_W), lambda i: (layer, i, 0, 0, 0)),
            pl.BlockSpec((nseq, CONV_W - 1, 3 * GROUP_W), lambda i: (i, 0, 0)),
        ],
        out_shape=[
            jax.ShapeDtypeStruct((bs * ls, GROUP_W), F32),
            jax.ShapeDtypeStruct(s0.shape, F32),
            jax.ShapeDtypeStruct(conv0.shape[1:], F32),
        ],
        scratch_shapes=[pltpu.VMEM((3, nseq * 16, GROUP_W), F32)],
        input_output_aliases=aliases,
        compiler_params=_cparams(("parallel",)),
        name="gdn_sample",
    )(p2, p2, p2, p2, g2, conv_w, alog_row, dt_row, dnw.reshape(1, HEAD_W), conv0, s0, *extra_in)


def _outproj_kernel(h_ref, oa_ref, ob_ref, ox_ref, w_ref, nf_ref, o_ref, *, final):
    acc = h_ref[...]
    for n, r in enumerate((oa_ref, ob_ref, ox_ref)):
        acc = acc + _dot(r[...].astype(BF16), w_ref[n * GROUP_W:(n + 1) * GROUP_W, :])
    o_ref[...] = _rms(acc, nf_ref[...]) if final else acc


def _outproj(h2d, oa, ob, ox, w_out, norm_f, final):
    t = h2d.shape[0]
    tm = min(t, 1024)
    row = lambda w: pl.BlockSpec((tm, w), lambda i: (i, 0))
    return pl.pallas_call(
        functools.partial(_outproj_kernel, final=final),
        grid=(t // tm,),
        in_specs=[row(D_MODEL), row(GROUP_W), row(GROUP_W), row(GROUP_W),
                  pl.BlockSpec((3 * GROUP_W, D_MODEL), lambda i: (0, 0)),
                  pl.BlockSpec((1, D_MODEL), lambda i: (0, 0))],
        out_specs=row(D_MODEL),
        out_shape=jax.ShapeDtypeStruct((t, D_MODEL), F32),
        compiler_params=_cparams(("parallel",)),
        name="outproj",
    )(h2d, oa, ob, ox, w_out, norm_f.reshape(1, D_MODEL))


def _pad_lanes(v):
    return jnp.pad(v.astype(F32), (0, LANES - v.shape[0])).reshape(1, LANES)


def kernel(x_prompt, x_sample, cache_k, cache_v, state_delta, state_conv, cache_mem_k, cache_mem_v, page_table, mem_prompt, norm_w, w_in, diff_lambda, diff_norm_w, conv_w, a_log, dt_bias, delta_norm_w, norm_mem, w_mem_kv, w_out, norm_f):
    bp, lp, _ = x_prompt.shape
    bs, ls, _ = x_sample.shape
    depth = w_in.shape[0]
    n_mem = mem_prompt.shape[1]
    n_gate = 2 * N_HEADS
    main_w = N_MAIN - 2 * GROUP_W

    w_main = jnp.concatenate([w_in[:, :, :main_w], w_in[:, :, main_w + n_gate:]], axis=2).astype(BF16)
    w_gate = jnp.pad(w_in[:, :, main_w:main_w + n_gate], ((0, 0), (0, 0), (0, LANES - n_gate))).astype(BF16)
    w_out_b = w_out.astype(BF16)
    cache_k4 = cache_k.reshape(cache_k.shape[0], cache_k.shape[1], PAGE * N_HEADS, HEAD_W)
    cache_v4 = cache_v.reshape(cache_v.shape[0], cache_v.shape[1], PAGE * N_HEADS, HEAD_W)
    mem_k4 = cache_mem_k.reshape(depth, bs, n_mem * N_HEADS, HEAD_W)
    mem_v4 = cache_mem_v.reshape(depth, bs, n_mem * N_HEADS, HEAD_W)

    mk_all, mv_all = _memkv(mem_prompt.reshape(bp * n_mem, D_MODEL), norm_mem, w_mem_kv.astype(BF16))

    hp = x_prompt.reshape(bp * lp, D_MODEL)
    hs = x_sample.reshape(bs * ls, D_MODEL)
    tq_x = min(lp, 512)
    sp, cp, cs = ([] for _ in range(3))
    kv_p = kv_s = s_all = None
    for l in range(depth):
        lam_init = 0.8 - 0.6 * math.exp(-0.3 * l)
        alog_row = _pad_lanes(a_log[l])
        dt_row = _pad_lanes(dt_bias[l])
        final = l == depth - 1

        pp, gp, pa16, *kv_p = _proj(hp, norm_w[l], w_main[l], w_gate[l], l, depth, kv_p)
        ps, gs, sa16, *kv_s = _proj(hs, norm_w[l], w_main[l], w_gate[l], l, depth, kv_s)
        oa, oa_s = _attn(pa16.reshape(bp, lp, N_A), sa16.reshape(bs, ls, N_A), cache_k4, cache_v4, l, page_table,
                         diff_lambda[l], diff_norm_w[l], lam_init)

        pp3 = pp.reshape(bp, lp, N_P)
        ob, s_new, c_new = _gdn_prompt(pp3, gp.reshape(bp, lp, LANES), conv_w[l], alog_row, dt_row, delta_norm_w[l])
        ox = _cross_prompt(pp.reshape(bp, lp // tq_x, tq_x, N_P), mk_all[l].reshape(bp, n_mem, GROUP_W),
                           mv_all[l].reshape(bp, n_mem, GROUP_W))
        hp = _outproj(hp, oa.reshape(bp * lp, GROUP_W), ob.reshape(bp * lp, GROUP_W),
                      ox.reshape(bp * lp, GROUP_W), w_out_b[l], norm_f, final)
        sp.append(s_new)
        cp.append(c_new)

        ps3 = ps.reshape(bs, ls, N_P)
        ob, s_all, c_new = _gdn_sample(ps, gs, ls, conv_w[l], alog_row, dt_row, delta_norm_w[l],
                                       state_conv, state_delta, l, s_all)
        ox = _cross_sample(ps3, mem_k4, mem_v4, l)
        hs = _outproj(hs, oa_s.reshape(bs * ls, GROUP_W), ob, ox.reshape(bs * ls, GROUP_W), w_out_b[l], norm_f, final)
        cs.append(c_new)

    heads = lambda x, b, n: x.reshape(depth, b, n, N_HEADS, HEAD_W)
    return (hp.reshape(bp, lp, D_MODEL), hs.reshape(bs, ls, D_MODEL),
            heads(kv_p[0], bp, lp), heads(kv_p[1], bp, lp), jnp.stack(sp), jnp.stack(cp),
            heads(mk_all, bp, n_mem), heads(mv_all, bp, n_mem),
            heads(kv_s[0], bs, ls), heads(kv_s[1], bs, ls), s_all, jnp.stack(cs))
```

```python
import functools
import math

import jax
import jax.numpy as jnp
import numpy as np
from jax import lax
from jax.experimental import pallas as pl
from jax.experimental.pallas import tpu as pltpu

F32 = jnp.float32
BF16 = jnp.bfloat16

D_MODEL = 1024
N_HEADS = 4
HEAD_W = 128
GROUP_W = N_HEADS * HEAD_W
A_DH = 64
CONV_W = 4
CHUNK = 64
GDN_GROUP = 4
PAGE = 128
EPS = 1e-6
NEG_INF = -1e30
LOG2E = math.log2(math.e)
N_MAIN = 10 * GROUP_W
N_A = 4 * GROUP_W
N_P = N_MAIN - N_A
LANES = 128
VMEM_LIMIT = 48 * 1024 * 1024

SEG_AQ, SEG_AK, SEG_AV, SEG_AG = range(4)
SEG_BQ, SEG_BK, SEG_BV, SEG_BG, SEG_XQ, SEG_XG = range(6)


def _cparams(sem):
    return pltpu.CompilerParams(dimension_semantics=sem, vmem_limit_bytes=VMEM_LIMIT)


def _iota(shape, dim):
    return lax.broadcasted_iota(jnp.int32, shape, dim)


def _div(x, d):
    assert d & (d - 1) == 0
    return x >> (d.bit_length() - 1)


def _sigmoid(x):
    return 1.0 / (1.0 + jnp.exp(-x))


def _silu(x):
    return x * _sigmoid(x)


def _softplus(x):
    return jnp.maximum(x, 0.0) + jnp.log(1.0 + jnp.exp(-jnp.abs(x)))


def _dot(a, b):
    return jnp.dot(a, b, preferred_element_type=F32)


def _dot_nt(a, b):
    return lax.dot_general(a, b, (((1,), (1,)), ((), ())), preferred_element_type=F32)


def _rms(x, w):
    ms = jnp.mean(x * x, axis=-1, keepdims=True)
    return x * lax.rsqrt(ms + EPS) * w


def _lambda_value(p, lam_init):
    a = jnp.sum(p[0:1, :] * p[1:2, :], axis=-1, keepdims=True)
    b = jnp.sum(p[2:3, :] * p[3:4, :], axis=-1, keepdims=True)
    return jnp.exp(a) - jnp.exp(b) + lam_init


def _proj_kernel(x_ref, nw_ref, w_ref, wg_ref, *rest, tm):
    p_ref, g_ref, a16_ref, k_out, v_out = rest[-5:]
    hn = _rms(x_ref[...], nw_ref[...]).astype(BF16)
    g_ref[...] = _dot(hn, wg_ref[...])

    def scatter_heads(out_ref, cols):
        for h in range(N_HEADS):
            out_ref[pl.ds(h, tm, stride=N_HEADS), :] = cols[:, h * HEAD_W:(h + 1) * HEAD_W]

    for seg in range(N_MAIN // GROUP_W):
        res = _dot(hn, w_ref[:, seg * GROUP_W:(seg + 1) * GROUP_W])
        if seg < N_A // GROUP_W:
            a16_ref[:, seg * GROUP_W:(seg + 1) * GROUP_W] = res.astype(BF16)
            if seg == SEG_AK:
                scatter_heads(k_out, res)
            if seg == SEG_AV:
                scatter_heads(v_out, res)
        else:
            p_ref[:, seg * GROUP_W - N_A:(seg + 1) * GROUP_W - N_A] = res


def _proj(x2d, norm_w, w_main, w_gate, layer, depth, kv_all):
    t = x2d.shape[0]
    tm = min(t, 512)
    kv_spec = pl.BlockSpec((None, N_HEADS * tm, HEAD_W), lambda i: (layer, i, 0))
    kv_shape = jax.ShapeDtypeStruct((depth, N_HEADS * t, HEAD_W), F32)
    n_in = 4
    extra_in, extra_specs, aliases = (), [], {}
    if kv_all is not None:
        extra_in = tuple(kv_all)
        extra_specs = [pl.BlockSpec(memory_space=pl.ANY)] * 2
        aliases = {n_in: 3, n_in + 1: 4}
    resident = dict(pipeline_mode=pl.Buffered(1))
    return pl.pallas_call(
        functools.partial(_proj_kernel, tm=tm),
        grid=(t // tm,),
        in_specs=[
            pl.BlockSpec((tm, D_MODEL), lambda i: (i, 0)),
            pl.BlockSpec((1, D_MODEL), lambda i: (0, 0)),
            pl.BlockSpec((D_MODEL, N_MAIN), lambda i: (0, 0), **resident),
            pl.BlockSpec((D_MODEL, LANES), lambda i: (0, 0), **resident),
        ] + extra_specs,
        out_specs=[
            pl.BlockSpec((tm, N_P), lambda i: (i, 0)),
            pl.BlockSpec((tm, LANES), lambda i: (i, 0)),
            pl.BlockSpec((tm, N_A), lambda i: (i, 0)),
            kv_spec, kv_spec,
        ],
        out_shape=[jax.ShapeDtypeStruct((t, N_P), F32), jax.ShapeDtypeStruct((t, LANES), F32),
                   jax.ShapeDtypeStruct((t, N_A), BF16), kv_shape, kv_shape],
        input_output_aliases=aliases,
        compiler_params=_cparams(("parallel",)),
        name="proj",
    )(x2d, norm_w.reshape(1, D_MODEL), w_main, w_gate, *extra_in)


def _memkv_kernel(x_ref, nw_ref, w_ref, k_ref, v_ref):
    hn = _rms(x_ref[...], nw_ref[...]).astype(BF16)
    kv = _dot(hn, w_ref[...])
    k_ref[...] = kv[:, :GROUP_W]
    v_ref[...] = kv[:, GROUP_W:]


def _memkv(mem2d, norm_mem, w_mem_kv_bf16):
    depth = norm_mem.shape[0]
    t = mem2d.shape[0]
    return pl.pallas_call(
        _memkv_kernel,
        grid=(depth,),
        in_specs=[
            pl.BlockSpec((t, D_MODEL), lambda l: (0, 0)),
            pl.BlockSpec((None, 1, D_MODEL), lambda l: (l, 0, 0)),
            pl.BlockSpec((None, D_MODEL, 2 * GROUP_W), lambda l: (l, 0, 0)),
        ],
        out_specs=[
            pl.BlockSpec((None, t, GROUP_W), lambda l: (l, 0, 0)),
            pl.BlockSpec((None, t, GROUP_W), lambda l: (l, 0, 0)),
        ],
        out_shape=[jax.ShapeDtypeStruct((depth, t, GROUP_W), F32)] * 2,
        compiler_params=_cparams(("parallel",)),
        name="memkv",
    )(mem2d, norm_mem.reshape(depth, 1, D_MODEL), w_mem_kv_bf16)


def _alibi_slope(h):
    return jnp.where(h == 0, 0.25, jnp.where(h == 1, 0.0625, jnp.where(h == 2, 0.015625, 0.00390625))).astype(F32)


def _diff_epilogue(o1, o2, lam, dnw, lam_scale, gate):
    o = o1 - lam * o2
    return _rms(o, dnw) * lam_scale * _silu(gate)


def _bf16_split3(x):
    parts = []
    for _ in range(3):
        p = float(np.asarray(x, np.float32).astype(jnp.bfloat16).astype(np.float32))
        parts.append(p)
        x = x - p
    return parts


_ALIBI_PARTS = [_bf16_split3(2.0 ** (-2.0 * (h + 1)) * LOG2E) for h in range(N_HEADS)]
POS_LO = 128
LOOP_TILES = 4


def _by_head(h, values):
    out = jnp.float32(values[-1])
    for idx in range(len(values) - 2, -1, -1):
        out = jnp.where(h == idx, jnp.float32(values[idx]), out)
    return out


def _sample_attention(x_ref, k_refs, v_refs, lam, dnw, lam_scale, ls):
    n_pages = len(k_refs)
    past = n_pages * PAGE
    rows = 2 * ls * N_HEADS
    x = x_ref[...].astype(F32)
    q = x[:, 0:GROUP_W] * (A_DH ** -0.5 * LOG2E)
    k_new = x[:, GROUP_W:2 * GROUP_W]
    v_new = x[:, 2 * GROUP_W:3 * GROUP_W]
    gate = x[:, 3 * GROUP_W:4 * GROUP_W]

    lane_q = _iota((ls, HEAD_W), 1)
    parts = []
    for h in range(N_HEADS):
        qh = q[:, h * HEAD_W:(h + 1) * HEAD_W]
        parts += [jnp.where(lane_q < A_DH, qh, 0.0), jnp.where(lane_q >= A_DH, qh, 0.0)]
    qs = jnp.concatenate(parts, axis=0)
    row_head_w = _div(_iota((rows, GROUP_W), 0), 2 * ls)
    qb = jnp.where(row_head_w == _div(_iota((rows, GROUP_W), 1), HEAD_W),
                   jnp.concatenate([qs] * N_HEADS, axis=1), 0.0).astype(BF16)
    qs = qs.astype(BF16)

    row_head = _div(_iota((rows, 1), 0), 2 * ls)
    slope2 = _alibi_slope(row_head) * LOG2E
    tok = _iota((rows, 1), 0) & (ls - 1)

    col = _iota((1, N_HEADS * PAGE), 1)
    own = (col & (N_HEADS - 1)) == row_head
    key_in_page = _div(col, N_HEADS)
    s_pages = []
    for p in range(n_pages):
        s = _dot_nt(qs, k_refs[p][...].astype(BF16))
        s = s + slope2 * (key_in_page + (p * PAGE - past)).astype(F32)
        s_pages.append(jnp.where(own, s, NEG_INF))
    lane = _iota((1, PAGE), 1)
    pad = jnp.zeros((PAGE - ls, GROUP_W), F32)
    k_pad = jnp.concatenate([k_new, pad], axis=0).astype(BF16)
    v_pad = jnp.concatenate([v_new, pad], axis=0).astype(BF16)
    s_new = _dot_nt(qb, k_pad) + slope2 * lane.astype(F32)
    s_new = jnp.where(lane <= tok, s_new, NEG_INF)

    m = s_new.max(axis=-1, keepdims=True)
    for s in s_pages:
        m = jnp.maximum(m, s.max(axis=-1, keepdims=True))
    e = jnp.exp2(s_new - m)
    l = jnp.sum(e, axis=-1, keepdims=True)
    o_new = _dot(e.astype(BF16), v_pad)
    acc = jnp.zeros((rows, HEAD_W), F32)
    for h in range(N_HEADS):
        acc = acc + jnp.where(row_head == h, o_new[:, h * HEAD_W:(h + 1) * HEAD_W], 0.0)
    for p, s in enumerate(s_pages):
        e = jnp.exp2(s - m)
        l = l + jnp.sum(e, axis=-1, keepdims=True)
        acc = acc + _dot(e.astype(BF16), v_refs[p][...].astype(BF16))
    o = acc / l

    outs = []
    for h in range(N_HEADS):
        o1 = o[2 * ls * h:2 * ls * h + ls]
        o2 = o[2 * ls * h + ls:2 * ls * (h + 1)]
        outs.append(_diff_epilogue(o1, o2, lam, dnw, lam_scale, gate[:, h * HEAD_W:(h + 1) * HEAD_W]))
    return jnp.concatenate(outs, axis=1)


def _attn_kernel(pt_ref, q_ref, k_ref, v_ref, g_ref, lamp_ref, dnw_ref, xs_ref, *rest, tq, n_pages, ls, lam_init):
    del pt_ref
    k_pages = rest[:n_pages]
    v_pages = rest[n_pages:2 * n_pages]
    o_ref, os_ref, kf_ref, qs_ref, s_a, s_b, m_ref, l_ref, acc_ref = rest[2 * n_pages:]
    h = pl.program_id(1)
    i = pl.program_id(2)
    tk = tq
    n_lane_tiles = tk // LANES
    seq = k_ref.shape[0]
    lam = _lambda_value(lamp_ref[...], lam_init)

    os_ref[...] = _sample_attention(xs_ref, k_pages, v_pages, lam, dnw_ref[...], 1.0 - lam_init, ls)

    @pl.when(i == 0)
    def _():
        def fill(r, carry):
            rows = pl.ds(pl.multiple_of(r * tk, tk), tk)
            k = k_ref[rows, :].astype(F32)
            lane = _iota((tk, HEAD_W), 1)
            pos = r * tk + _iota((tk, HEAD_W), 0)
            hi = _div(pos, POS_LO).astype(F32)
            lo = (pos & (POS_LO - 1)).astype(F32)
            for mp in range(2):
                rel = lane - (A_DH if mp == 0 else 0)
                feat = jnp.where(rel < 3, hi, jnp.where(rel < 6, lo, 0.0))
                own = (lane < A_DH) if mp == 0 else (lane >= A_DH)
                kf_ref[mp, rows, :] = jnp.where(own, k, feat).astype(BF16)
            return carry

        lax.fori_loop(0, seq // tk, fill, 0)

    q = q_ref[...].astype(F32) * (A_DH ** -0.5 * LOG2E)
    lane = _iota((1, HEAD_W), 1)
    c = [_by_head(h, [_ALIBI_PARTS[hh][part] for hh in range(N_HEADS)]) for part in range(3)]
    for mp in range(2):
        rel = lane - (A_DH if mp == 0 else 0)
        qfeat = jnp.zeros((1, HEAD_W), F32)
        for part in range(3):
            qfeat = jnp.where(rel == part, c[part] * POS_LO, jnp.where(rel == 3 + part, c[part], qfeat))
        own = (lane < A_DH) if mp == 0 else (lane >= A_DH)
        qs_ref[mp] = jnp.where(own, q, qfeat).astype(BF16)
    m_ref[...] = jnp.full(m_ref.shape, NEG_INF, F32)
    l_ref[...] = jnp.zeros(l_ref.shape, F32)
    acc_ref[...] = jnp.zeros(acc_ref.shape, F32)

    def scores(mp, j):
        return _dot_nt(qs_ref[mp], kf_ref[mp, pl.ds(pl.multiple_of(j * tk, tk), tk), :])

    def update(mp, s_ref, j, diagonal):
        def tile(cidx):
            s = s_ref[:, cidx * LANES:(cidx + 1) * LANES]
            if diagonal:
                s = jnp.where(cidx * LANES + _iota((1, LANES), 1) <= _iota((tq, 1), 0), s, NEG_INF)
            return s

        mx = tile(0)
        for cidx in range(1, n_lane_tiles):
            mx = jnp.maximum(mx, tile(cidx))
        m_prev = m_ref[mp]
        m_new = jnp.maximum(m_prev, jnp.max(mx, axis=-1, keepdims=True))
        alpha = jnp.exp2(m_prev - m_new)
        lsum = alpha * l_ref[mp]
        ps = []
        for cidx in range(n_lane_tiles):
            p = jnp.exp2(tile(cidx) - m_new)
            lsum = lsum + p
            ps.append(p.astype(BF16))
        v = v_ref[pl.ds(pl.multiple_of(j * tk, tk), tk), :]
        acc_ref[mp] = alpha * acc_ref[mp] + _dot(jnp.concatenate(ps, axis=1), v)
        l_ref[mp] = lsum
        m_ref[mp] = m_new

    s_a[...] = scores(0, 0)

    def step(j):
        s_b[...] = scores(1, j)
        update(0, s_a, j, False)
        s_a[...] = scores(0, j + 1)
        update(1, s_b, j, False)

    def quad(jj, carry):
        for u in range(LOOP_TILES):
            step(LOOP_TILES * jj + u)
        return carry

    lax.fori_loop(0, _div(i, LOOP_TILES), quad, 0)
    rem = i & (LOOP_TILES - 1)
    for u in range(1, LOOP_TILES):
        @pl.when(rem >= u)
        def _():
            step(i - rem + (u - 1))

    s_b[...] = scores(1, i)
    update(0, s_a, i, True)
    update(1, s_b, i, True)

    o1 = acc_ref[0] / jnp.sum(l_ref[0], axis=-1, keepdims=True)
    o2 = acc_ref[1] / jnp.sum(l_ref[1], axis=-1, keepdims=True)
    y = _diff_epilogue(o1, o2, lam, dnw_ref[...], 1.0 - lam_init, g_ref[...].astype(F32))
    o_ref[...] = y.astype(o_ref.dtype)


def _attn(a16, xs16, cache_k4, cache_v4, layer, page_table, lam_p, dnw, lam_init):
    b, l, _ = a16.shape
    bs, ls, _ = xs16.shape
    n_pages = page_table.shape[1]
    tq = min(l, 512)
    nq = l // tq
    assert l <= POS_LO * 256 and tq % LANES == 0
    assert b * N_HEADS * nq == bs, "one sample sequence per prompt grid step"

    def step_id(bi, h, i):
        return (bi * N_HEADS + h) * nq + i

    def page_map(p):
        return lambda bi, h, i, pt: (layer, pt[step_id(bi, h, i) * n_pages + p], 0, 0)

    page_specs = lambda: [pl.BlockSpec((None, None, N_HEADS * PAGE, HEAD_W), page_map(p)) for p in range(n_pages)]
    grid_spec = pltpu.PrefetchScalarGridSpec(
        num_scalar_prefetch=1,
        grid=(b, N_HEADS, nq),
        in_specs=[
            pl.BlockSpec((None, tq, HEAD_W), lambda bi, h, i, pt: (bi, i, SEG_AQ * N_HEADS + h)),
            pl.BlockSpec((None, l, HEAD_W), lambda bi, h, i, pt: (bi, 0, SEG_AK * N_HEADS + h)),
            pl.BlockSpec((None, l, HEAD_W), lambda bi, h, i, pt: (bi, 0, SEG_AV * N_HEADS + h)),
            pl.BlockSpec((None, tq, HEAD_W), lambda bi, h, i, pt: (bi, i, SEG_AG * N_HEADS + h)),
            pl.BlockSpec((4, A_DH), lambda bi, h, i, pt: (0, 0)),
            pl.BlockSpec((1, HEAD_W), lambda bi, h, i, pt: (0, 0)),
            pl.BlockSpec((None, ls, N_A), lambda bi, h, i, pt: (step_id(bi, h, i), 0, 0)),
        ] + page_specs() + page_specs(),
        out_specs=[
            pl.BlockSpec((None, tq, HEAD_W), lambda bi, h, i, pt: (bi, i, h)),
            pl.BlockSpec((None, ls, GROUP_W), lambda bi, h, i, pt: (step_id(bi, h, i), 0, 0)),
        ],
        scratch_shapes=[
            pltpu.VMEM((2, l, HEAD_W), BF16),
            pltpu.VMEM((2, tq, HEAD_W), BF16),
            pltpu.VMEM((tq, tq), F32),
            pltpu.VMEM((tq, tq), F32),
            pltpu.VMEM((2, tq, LANES), F32),
            pltpu.VMEM((2, tq, LANES), F32),
            pltpu.VMEM((2, tq, HEAD_W), F32),
        ],
    )
    return pl.pallas_call(
        functools.partial(_attn_kernel, tq=tq, n_pages=n_pages, ls=ls, lam_init=lam_init),
        grid_spec=grid_spec,
        out_shape=[jax.ShapeDtypeStruct((b, l, GROUP_W), BF16), jax.ShapeDtypeStruct((bs, ls, GROUP_W), F32)],
        compiler_params=_cparams(("parallel", "parallel", "arbitrary")),
        name="attn",
    )(page_table.reshape(-1), a16, a16, a16, a16, lam_p, dnw.reshape(1, HEAD_W), xs16,
      *([cache_k4] * n_pages), *([cache_v4] * n_pages))


def _cross_kernel(q_ref, g_ref, mk_ref, mv_ref, o_ref, *, nseq):
    for s in range(nseq):
        q = q_ref[s] * (HEAD_W ** -0.5)
        gate = g_ref[s]
        mk = mk_ref[s].astype(BF16)
        mv = mv_ref[s].astype(BF16)
        outs = []
        for h in range(N_HEADS):
            sl = slice(h * HEAD_W, (h + 1) * HEAD_W)
            sc = _dot_nt(q[:, sl].astype(BF16), mk[:, sl])
            e = jnp.exp(sc - sc.max(axis=-1, keepdims=True))
            o = _dot(e.astype(BF16), mv[:, sl]) / jnp.sum(e, axis=-1, keepdims=True)
            outs.append(o * _silu(gate[:, sl]))
        o_ref[s] = jnp.concatenate(outs, axis=1).astype(o_ref.dtype)


def _cross_prompt(p4, mk, mv):
    nb, nt, tq, _ = p4.shape
    n_mem = mk.shape[1]
    return pl.pallas_call(
        functools.partial(_cross_kernel, nseq=1),
        grid=(nb, nt),
        in_specs=[
            pl.BlockSpec((1, None, tq, GROUP_W), lambda b, i: (b, i, 0, SEG_XQ)),
            pl.BlockSpec((1, None, tq, GROUP_W), lambda b, i: (b, i, 0, SEG_XG)),
            pl.BlockSpec((1, n_mem, GROUP_W), lambda b, i: (b, 0, 0)),
            pl.BlockSpec((1, n_mem, GROUP_W), lambda b, i: (b, 0, 0)),
        ],
        out_specs=pl.BlockSpec((1, None, tq, GROUP_W), lambda b, i: (b, i, 0, 0)),
        out_shape=jax.ShapeDtypeStruct((nb, nt, tq, GROUP_W), BF16),
        compiler_params=_cparams(("parallel", "parallel")),
        name="cross_prompt",
    )(p4, p4, mk, mv)


def _cross_sample_kernel(q_ref, g_ref, mk_ref, mv_ref, o_ref, *, nseq, ls):
    rows = N_HEADS * ls
    n_rows = mk_ref.shape[1]
    own = (_iota((1, n_rows), 1) & (N_HEADS - 1)) == _div(_iota((rows, 1), 0), ls)
    for s in range(nseq):
        q = q_ref[s] * (HEAD_W ** -0.5)
        gate = g_ref[s]
        qx = jnp.concatenate([q[:, h * HEAD_W:(h + 1) * HEAD_W] for h in range(N_HEADS)], axis=0)
        sc = jnp.where(own, _dot_nt(qx.astype(BF16), mk_ref[s].astype(BF16)), NEG_INF)
        e = jnp.exp(sc - sc.max(axis=-1, keepdims=True))
        o = _dot(e.astype(BF16), mv_ref[s].astype(BF16)) / jnp.sum(e, axis=-1, keepdims=True)
        o = jnp.concatenate([o[h * ls:(h + 1) * ls] for h in range(N_HEADS)], axis=1)
        o_ref[s] = o * _silu(gate)


def _cross_sample(p3, mk, mv, layer):
    bs, ls, _ = p3.shape
    n_rows = mk.shape[2]
    nseq = 8
    return pl.pallas_call(
        functools.partial(_cross_sample_kernel, nseq=nseq, ls=ls),
        grid=(bs // nseq,),
        in_specs=[
            pl.BlockSpec((nseq, ls, GROUP_W), lambda b: (b, 0, SEG_XQ)),
            pl.BlockSpec((nseq, ls, GROUP_W), lambda b: (b, 0, SEG_XG)),
            pl.BlockSpec((None, nseq, n_rows, HEAD_W), lambda b: (layer, b, 0, 0)),
            pl.BlockSpec((None, nseq, n_rows, HEAD_W), lambda b: (layer, b, 0, 0)),
        ],
        out_specs=pl.BlockSpec((nseq, ls, GROUP_W), lambda b: (b, 0, 0)),
        out_shape=jax.ShapeDtypeStruct((bs, ls, GROUP_W), F32),
        compiler_params=_cparams(("parallel",)),
        name="cross_sample",
    )(p3, p3, mk, mv)


def _stack_heads(a):
    return jnp.concatenate([a[:, h * HEAD_W:(h + 1) * HEAD_W] for h in range(N_HEADS)], axis=0)


def _col(x, lane_idx):
    lane = _iota(x.shape, 1)
    return jnp.sum(jnp.where(lane == lane_idx, x, 0.0), axis=-1, keepdims=True)


def _split3_dot(mat_bf16, x):
    hi = x.astype(BF16)
    r1 = x - hi.astype(F32)
    mid = r1.astype(BF16)
    lo = (r1 - mid.astype(F32)).astype(BF16)
    return _dot(mat_bf16, hi) + _dot(mat_bf16, mid) + _dot(mat_bf16, lo)


def _run_interleaved(*gens):
    results = [None] * len(gens)
    live = list(enumerate(gens))
    while live:
        still = []
        for idx, g in live:
            try:
                next(g)
                still.append((idx, g))
            except StopIteration as done:
                results[idx] = done.value
        live = still
    return results


def _gdn_blocks(blocks, alog_row, dt_row, blk):
    return _run_interleaved(_gdn_blocks_staged(blocks, alog_row, dt_row, blk))[0]


def _gdn_blocks_staged(blocks, alog_row, dt_row, blk):
    n = N_HEADS * CHUNK
    sh = int(math.log2(blk))
    lane = _iota((1, LANES), 1)
    neg_a = jnp.where(lane < N_HEADS, -jnp.exp(alog_row), 0.0)
    ti = _iota((CHUNK, CHUNK), 0)
    tj = _iota((CHUNK, CHUNK), 1)
    same_t = (ti >> sh) == (tj >> sh)
    tri_incl = (same_t & (tj <= ti)).astype(BF16)
    tri_all = same_t.astype(BF16)
    ri = _iota((n, n), 0)
    ci = _iota((n, n), 1)
    same = (ri >> sh) == (ci >> sh)
    incl = same & (ci <= ri)
    strict = same & (ci < ri)

    st = []
    for qkv, ab in blocks:
        q = _stack_heads(qkv[0])
        k = _stack_heads(qkv[1])
        v = _stack_heads(qkv[2])
        qn = q * lax.rsqrt(jnp.sum(q * q, axis=-1, keepdims=True) + EPS) * (HEAD_W ** -0.5)
        kn = k * lax.rsqrt(jnp.sum(k * k, axis=-1, keepdims=True) + EPS)
        g = neg_a * _softplus(ab + dt_row)
        beta_full = _sigmoid(ab)
        gcum = _split3_dot(tri_incl, g)
        gtot = _split3_dot(tri_all, g)
        gc = jnp.concatenate([_col(gcum, h) for h in range(N_HEADS)], axis=0)
        gl = jnp.concatenate([_col(gtot, h) for h in range(N_HEADS)], axis=0)
        beta = jnp.concatenate([_col(beta_full, N_HEADS + h) for h in range(N_HEADS)], axis=0)
        gt = jnp.concatenate([gcum, gcum], axis=0).T
        gr = jnp.concatenate([jnp.where(lane < CHUNK, gt[0:1, :], gt[1:2, :]),
                              jnp.where(lane < CHUNK, gt[2:3, :], gt[3:4, :])], axis=1)
        decay = jnp.where(incl, jnp.exp(jnp.where(incl, gc - gr, 0.0)), 0.0)
        kb = kn * beta
        eg = jnp.exp(gc)
        st.append(dict(qg=qn * eg, kn=kn, gl=gl, gc=gc, decay=decay,
                       lhs=jnp.concatenate([kb, qn], axis=0).astype(BF16),
                       rhs=jnp.concatenate([v * beta, kb * eg], axis=1).astype(BF16)))
        yield

    for s in st:
        kq = _dot_nt(s["lhs"], s["kn"].astype(BF16))
        s["nmat"] = jnp.where(strict, kq[0:n] * s["decay"], 0.0)
        s["qkm"] = kq[n:2 * n] * s["decay"]
    yield

    pair = ((ri >> 1) == (ci >> 1)) & ((ri & 1) == 1) & ((ci & 1) == 0)
    eye = jnp.where(ri == ci, 1.0, 0.0)
    for s in st:
        s["x"] = eye - jnp.where(pair, s["nmat"], 0.0)
    b = 2
    while b < blk:
        sb = int(math.log2(2 * b))
        mask = ((ri >> sb) == (ci >> sb)) & ((ri & (2 * b - 1)) >= b) & ((ci & (2 * b - 1)) < b)
        for s in st:
            s["xb"] = s["x"].astype(BF16)
            s["t"] = _dot(s["xb"], jnp.where(mask, s["nmat"], 0.0).astype(BF16)).astype(BF16)
        yield
        for s in st:
            s["x"] = s["x"] - _dot(s["t"], s["xb"])
        yield
        b *= 2

    outs = []
    for s in st:
        sol = _dot(s["x"].astype(BF16), s["rhs"])
        kdec = s["kn"] * jnp.exp(s["gl"] - s["gc"])
        outs.append((sol[:, 0:HEAD_W], sol[:, HEAD_W:2 * HEAD_W], s["qg"], s["qkm"], kdec, s["gl"]))
    return outs


def _pad_t(a):
    return jnp.concatenate([a, jnp.zeros((LANES - CHUNK, LANES), F32)], axis=0).T


def _pad_rows(a):
    return jnp.concatenate([a, jnp.zeros((LANES - CHUNK, LANES), F32)], axis=0)


def _gdn_finish(o, gate, dnw):
    outs = []
    for h in range(N_HEADS):
        oh = o[h * CHUNK:(h + 1) * CHUNK]
        outs.append(_rms(oh, dnw) * _silu(gate[:, h * HEAD_W:(h + 1) * HEAD_W]))
    return jnp.concatenate(outs, axis=1)


def _conv_taps(xbuf, p, base, nrows, cw):
    acc = xbuf[p, pl.ds(base + 5, nrows), :] * cw[0:1, p * GROUP_W:(p + 1) * GROUP_W]
    for t in range(1, CONV_W):
        acc = acc + xbuf[p, pl.ds(base + 5 + t, nrows), :] * cw[t:t + 1, p * GROUP_W:(p + 1) * GROUP_W]
    return _silu(acc)


def _gdnp_kernel(bq_ref, bk_ref, bv_ref, bg_ref, ab_ref, cw_ref, alog_ref, dt_ref, dnw_ref,
                 o_ref, sout_ref, cout_ref, xbuf, s_scr, *, tb):
    t = pl.program_id(1)

    @pl.when(t == 0)
    def _():
        xbuf[:, 0:8, :] = jnp.zeros((3, 8, GROUP_W), F32)
        s_scr[...] = jnp.zeros(s_scr.shape, F32)

    x_refs = (bq_ref, bk_ref, bv_ref)
    for p in range(3):
        xbuf[p, 8:8 + tb, :] = x_refs[p][...]
    cw = cw_ref[...]
    gate = bg_ref[...]
    ab = ab_ref[...]
    group_rows = GDN_GROUP * CHUNK
    n_groups = tb // group_rows

    def conv_staged(g):
        qkv = []
        for p in range(3):
            qkv.append(_conv_taps(xbuf, p, g * group_rows, group_rows, cw))
            yield
        return qkv

    def prepass_staged(g, qkv):
        rows = [slice(c * CHUNK, (c + 1) * CHUNK) for c in range(GDN_GROUP)]
        ab_g = ab[g * group_rows:(g + 1) * group_rows]
        return _gdn_blocks_staged([([a[rs] for a in qkv], ab_g[rs]) for rs in rows], alog_ref[...], dt_ref[...], CHUNK)

    def sequential_staged(g, pre):
        for c in range(GDN_GROUP):
            rs = slice(g * group_rows + c * CHUNK, g * group_rows + (c + 1) * CHUNK)
            u, w, qg, qkm, kdec, gl = pre[c]
            vnews, qss = [], []
            for h in range(N_HEADS):
                hs = slice(h * CHUNK, (h + 1) * CHUNK)
                s_old = s_scr[h]
                wq = jnp.concatenate([w[hs], qg[hs]], axis=0).astype(BF16)
                r = _dot(wq, s_old.astype(BF16))
                vn = u[hs] - r[0:CHUNK]
                vnews.append(vn)
                qss.append(r[CHUNK:2 * CHUNK])
                upd = _dot(_pad_t(kdec[hs]).astype(BF16), _pad_rows(vn).astype(BF16))
                s_scr[h] = s_old * jnp.exp(gl[h * CHUNK:h * CHUNK + 1, :]) + upd
            yield
            vnew = jnp.concatenate(vnews, axis=0)
            o = jnp.concatenate(qss, axis=0) + _dot(qkm.astype(BF16), vnew.astype(BF16))
            o_ref[rs, :] = _gdn_finish(o, gate[rs], dnw_ref[...]).astype(o_ref.dtype)
            yield

    def nothing():
        return
        yield

    qkv = _run_interleaved(conv_staged(0))[0]
    pre_prev = None
    for g in range(n_groups):
        conv_next = conv_staged(g + 1) if g + 1 < n_groups else nothing()
        seq_prev = sequential_staged(g - 1, pre_prev) if g > 0 else nothing()
        pre, qkv_next, _ = _run_interleaved(prepass_staged(g, qkv), conv_next, seq_prev)
        pre_prev, qkv = pre, qkv_next
    _run_interleaved(sequential_staged(n_groups - 1, pre_prev))
    for p in range(3):
        xbuf[p, 5:8, :] = xbuf[p, tb + 5:tb + 8, :]

    @pl.when(t == pl.num_programs(1) - 1)
    def _():
        sout_ref[...] = s_scr[...]
        for p in range(3):
            cout_ref[:, p * GROUP_W:(p + 1) * GROUP_W] = xbuf[p, 5:8, :]


def _gdn_prompt(p3, g3, conv_w, alog_row, dt_row, dnw):
    b, l, _ = p3.shape
    tb = min(l, 512)
    seg = lambda s: pl.BlockSpec((None, tb, GROUP_W), lambda bi, t: (bi, t, s))
    const = lambda shape: pl.BlockSpec(shape, lambda bi, t: (0,) * len(shape))
    return pl.pallas_call(
        functools.partial(_gdnp_kernel, tb=tb),
        grid=(b, l // tb),
        in_specs=[seg(SEG_BQ), seg(SEG_BK), seg(SEG_BV), seg(SEG_BG),
                  pl.BlockSpec((None, tb, LANES), lambda bi, t: (bi, t, 0)),
                  const((CONV_W, 3 * GROUP_W)), const((1, LANES)), const((1, LANES)), const((1, HEAD_W))],
        out_specs=[
            pl.BlockSpec((None, tb, GROUP_W), lambda bi, t: (bi, t, 0)),
            pl.BlockSpec((None, N_HEADS, HEAD_W, HEAD_W), lambda bi, t: (bi, 0, 0, 0)),
            pl.BlockSpec((None, CONV_W - 1, 3 * GROUP_W), lambda bi, t: (bi, 0, 0)),
        ],
        out_shape=[
            jax.ShapeDtypeStruct((b, l, GROUP_W), BF16),
            jax.ShapeDtypeStruct((b, N_HEADS, HEAD_W, HEAD_W), F32),
            jax.ShapeDtypeStruct((b, CONV_W - 1, 3 * GROUP_W), F32),
        ],
        scratch_shapes=[pltpu.VMEM((3, tb + 8, GROUP_W), F32), pltpu.VMEM((N_HEADS, HEAD_W, HEAD_W), F32)],
        compiler_params=_cparams(("parallel", "arbitrary")),
        name="gdn_prompt",
    )(p3, p3, p3, p3, g3, conv_w, alog_row, dt_row, dnw.reshape(1, HEAD_W))


def _gdns_kernel(bq_ref, bk_ref, bv_ref, bg_ref, ab_ref, cw_ref, alog_ref, dt_ref, dnw_ref, c0_ref, s0_ref,
                 *rest, ls, nseq):
    o_ref, sout_ref, cout_ref, xbuf = rest[-4:]
    x_refs = (bq_ref, bk_ref, bv_ref)
    cw = cw_ref[...]
    qkv = []
    for p in range(3):
        for s in range(nseq):
            xbuf[p, s * 16 + 5:s * 16 + 8, :] = c0_ref[s, :, p * GROUP_W:(p + 1) * GROUP_W]
            xbuf[p, s * 16 + 8:s * 16 + 8 + ls, :] = x_refs[p][s * ls:(s + 1) * ls, :]
        qkv.append(jnp.concatenate([_conv_taps(xbuf, p, s * 16, ls, cw) for s in range(nseq)], axis=0))
        for s in range(nseq):
            cout_ref[s, :, p * GROUP_W:(p + 1) * GROUP_W] = xbuf[p, s * 16 + 5 + ls:s * 16 + 8 + ls, :]

    (u, w, qg, qkm, kdec, gl), = _gdn_blocks([(qkv, ab_ref[...])], alog_ref[...], dt_ref[...], ls)
    seq_of_row = _div(_iota((2 * CHUNK, 1), 0) & (CHUNK - 1), ls)
    vnews, qss = [], []
    for h in range(N_HEADS):
        hs = slice(h * CHUNK, (h + 1) * CHUNK)
        wq = jnp.concatenate([w[hs], qg[hs]], axis=0)
        r = jnp.zeros((2 * CHUNK, HEAD_W), F32)
        for s in range(nseq):
            r = r + _dot(jnp.where(seq_of_row == s, wq, 0.0).astype(BF16), s0_ref[s, h].astype(BF16))
        vn = u[hs] - r[0:CHUNK]
        vnews.append(vn)
        qss.append(r[CHUNK:2 * CHUNK])
        vn_pad = _pad_rows(vn).astype(BF16)
        for s in range(nseq):
            kd = jnp.where(seq_of_row[0:CHUNK] == s, kdec[hs], 0.0)
            row = h * CHUNK + s * ls
            sout_ref[s, h] = s0_ref[s, h] * jnp.exp(gl[row:row + 1, :]) + _dot(_pad_t(kd).astype(BF16), vn_pad)
    vnew = jnp.concatenate(vnews, axis=0)
    o = jnp.concatenate(qss, axis=0) + _dot(qkm.astype(BF16), vnew.astype(BF16))
    o_ref[...] = _gdn_finish(o, bg_ref[...], dnw_ref[...]).astype(o_ref.dtype)


def _gdn_sample(p2, g2, ls, conv_w, alog_row, dt_row, dnw, conv0, s0, layer, s_all):
    bs = s0.shape[1]
    nseq = CHUNK // ls
    seg = lambda s: pl.BlockSpec((CHUNK, GROUP_W), lambda i: (i, s))
    const = lambda shape: pl.BlockSpec(shape, lambda i: (0,) * len(shape))
    n_in = 11
    extra_in, extra_specs, aliases = (), [], {}
    if s_all is not None:
        extra_in, extra_specs, aliases = (s_all,), [pl.BlockSpec(memory_space=pl.ANY)], {n_in: 1}
    return pl.pallas_call(
        functools.partial(_gdns_kernel, ls=ls, nseq=nseq),
        grid=(bs // nseq,),
        in_specs=[seg(SEG_BQ), seg(SEG_BK), seg(SEG_BV), seg(SEG_BG),
                  pl.BlockSpec((CHUNK, LANES), lambda i: (i, 0)),
                  const((CONV_W, 3 * GROUP_W)), const((1, LANES)), const((1, LANES)), const((1, HEAD_W)),
                  pl.BlockSpec((None, nseq, CONV_W - 1, 3 * GROUP_W), lambda i: (layer, i, 0, 0)),
                  pl.BlockSpec((None, nseq, N_HEADS, HEAD_W, HEAD_W), lambda i: (layer, i, 0, 0, 0))] + extra_specs,
        out_specs=[
            pl.BlockSpec((CHUNK, GROUP_W), lambda i: (i, 0)),
            pl.BlockSpec((None, nseq, N_HEADS, HEAD_W, HEAD_W), lambda i: (layer, i, 0, 0, 0)),
            pl.BlockSpec((nseq, CONV_W - 1, 3 * GROUP_W), lambda i: (i, 0, 0)),
        ],
        out_shape=[
            jax.ShapeDtypeStruct((bs * ls, GROUP_W), F32),
            jax.ShapeDtypeStruct(s0.shape, F32),
            jax.ShapeDtypeStruct(conv0.shape[1:], F32),
        ],
        scratch_shapes=[pltpu.VMEM((3, nseq * 16, GROUP_W), F32)],
        input_output_aliases=aliases,
        compiler_params=_cparams(("parallel",)),
        name="gdn_sample",
    )(p2, p2, p2, p2, g2, conv_w, alog_row, dt_row, dnw.reshape(1, HEAD_W), conv0, s0, *extra_in)


def _outproj_kernel(h_ref, oa_ref, ob_ref, ox_ref, w_ref, nf_ref, o_ref, *, final):
    acc = h_ref[...]
    for n, r in enumerate((oa_ref, ob_ref, ox_ref)):
        acc = acc + _dot(r[...].astype(BF16), w_ref[n * GROUP_W:(n + 1) * GROUP_W, :])
    o_ref[...] = _rms(acc, nf_ref[...]) if final else acc


def _outproj(h2d, oa, ob, ox, w_out, norm_f, final):
    t = h2d.shape[0]
    tm = min(t, 1024)
    row = lambda w: pl.BlockSpec((tm, w), lambda i: (i, 0))
    return pl.pallas_call(
        functools.partial(_outproj_kernel, final=final),
        grid=(t // tm,),
        in_specs=[row(D_MODEL), row(GROUP_W), row(GROUP_W), row(GROUP_W),
                  pl.BlockSpec((3 * GROUP_W, D_MODEL), lambda i: (0, 0)),
                  pl.BlockSpec((1, D_MODEL), lambda i: (0, 0))],
        out_specs=row(D_MODEL),
        out_shape=jax.ShapeDtypeStruct((t, D_MODEL), F32),
        compiler_params=_cparams(("parallel",)),
        name="outproj",
    )(h2d, oa, ob, ox, w_out, norm_f.reshape(1, D_MODEL))


def _pad_lanes(v):
    return jnp.pad(v.astype(F32), (0, LANES - v.shape[0])).reshape(1, LANES)


def kernel(x_prompt, x_sample, cache_k, cache_v, state_delta, state_conv, cache_mem_k, cache_mem_v, page_table, mem_prompt, norm_w, w_in, diff_lambda, diff_norm_w, conv_w, a_log, dt_bias, delta_norm_w, norm_mem, w_mem_kv, w_out, norm_f):
    bp, lp, _ = x_prompt.shape
    bs, ls, _ = x_sample.shape
    depth = w_in.shape[0]
    n_mem = mem_prompt.shape[1]
    n_gate = 2 * N_HEADS
    main_w = N_MAIN - 2 * GROUP_W

    w_main = jnp.concatenate([w_in[:, :, :main_w], w_in[:, :, main_w + n_gate:]], axis=2).astype(BF16)
    w_gate = jnp.pad(w_in[:, :, main_w:main_w + n_gate], ((0, 0), (0, 0), (0, LANES - n_gate))).astype(BF16)
    w_out_b = w_out.astype(BF16)
    cache_k4 = cache_k.reshape(cache_k.shape[0], cache_k.shape[1], PAGE * N_HEADS, HEAD_W)
    cache_v4 = cache_v.reshape(cache_v.shape[0], cache_v.shape[1], PAGE * N_HEADS, HEAD_W)
    mem_k4 = cache_mem_k.reshape(depth, bs, n_mem * N_HEADS, HEAD_W)
    mem_v4 = cache_mem_v.reshape(depth, bs, n_mem * N_HEADS, HEAD_W)

    mk_all, mv_all = _memkv(mem_prompt.reshape(bp * n_mem, D_MODEL), norm_mem, w_mem_kv.astype(BF16))

    hp = x_prompt.reshape(bp * lp, D_MODEL)
    hs = x_sample.reshape(bs * ls, D_MODEL)
    tq_x = min(lp, 512)
    sp, cp, cs = ([] for _ in range(3))
    kv_p = kv_s = s_all = None
    for l in range(depth):
        lam_init = 0.8 - 0.6 * math.exp(-0.3 * l)
        alog_row = _pad_lanes(a_log[l])
        dt_row = _pad_lanes(dt_bias[l])
        final = l == depth - 1

        pp, gp, pa16, *kv_p = _proj(hp, norm_w[l], w_main[l], w_gate[l], l, depth, kv_p)
        ps, gs, sa16, *kv_s = _proj(hs, norm_w[l], w_main[l], w_gate[l], l, depth, kv_s)
        oa, oa_s = _attn(pa16.reshape(bp, lp, N_A), sa16.reshape(bs, ls, N_A), cache_k4, cache_v4, l, page_table,
                         diff_lambda[l], diff_norm_w[l], lam_init)

        pp3 = pp.reshape(bp, lp, N_P)
        ob, s_new, c_new = _gdn_prompt(pp3, gp.reshape(bp, lp, LANES), conv_w[l], alog_row, dt_row, delta_norm_w[l])
        ox = _cross_prompt(pp.reshape(bp, lp // tq_x, tq_x, N_P), mk_all[l].reshape(bp, n_mem, GROUP_W),
                           mv_all[l].reshape(bp, n_mem, GROUP_W))
        hp = _outproj(hp, oa.reshape(bp * lp, GROUP_W), ob.reshape(bp * lp, GROUP_W),
                      ox.reshape(bp * lp, GROUP_W), w_out_b[l], norm_f, final)
        sp.append(s_new)
        cp.append(c_new)

        ps3 = ps.reshape(bs, ls, N_P)
        ob, s_all, c_new = _gdn_sample(ps, gs, ls, conv_w[l], alog_row, dt_row, delta_norm_w[l],
                                       state_conv, state_delta, l, s_all)
        ox = _cross_sample(ps3, mem_k4, mem_v4, l)
        hs = _outproj(hs, oa_s.reshape(bs * ls, GROUP_W), ob, ox.reshape(bs * ls, GROUP_W), w_out_b[l], norm_f, final)
        cs.append(c_new)

    heads = lambda x, b, n: x.reshape(depth, b, n, N_HEADS, HEAD_W)
    return (hp.reshape(bp, lp, D_MODEL), hs.reshape(bs, ls, D_MODEL),
            heads(kv_p[0], bp, lp), heads(kv_p[1], bp, lp), jnp.stack(sp), jnp.stack(cp),
            heads(mk_all, bp, n_mem), heads(mv_all, bp, n_mem),
            heads(kv_s[0], bs, ls), heads(kv_s[1], bs, ls), s_all, jnp.stack(cs))
```

```python
import functools
import math

import jax
import jax.numpy as jnp
import numpy as np
from jax import lax
from jax.experimental import pallas as pl
from jax.experimental.pallas import tpu as pltpu

F32 = jnp.float32
BF16 = jnp.bfloat16

D_MODEL = 1024
N_HEADS = 4
HEAD_W = 128
GROUP_W = N_HEADS * HEAD_W
A_DH = 64
CONV_W = 4
CHUNK = 64
GDN_GROUP = 4
PAGE = 128
EPS = 1e-6
NEG_INF = -1e30
LOG2E = math.log2(math.e)
N_MAIN = 10 * GROUP_W
N_A = 4 * GROUP_W
N_P = N_MAIN - N_A
LANES = 128
VMEM_LIMIT = 48 * 1024 * 1024

SEG_AQ, SEG_AK, SEG_AV, SEG_AG = range(4)
SEG_BQ, SEG_BK, SEG_BV, SEG_BG, SEG_XQ, SEG_XG = range(6)


def _cparams(sem):
    return pltpu.CompilerParams(dimension_semantics=sem, vmem_limit_bytes=VMEM_LIMIT)


def _iota(shape, dim):
    return lax.broadcasted_iota(jnp.int32, shape, dim)


def _div(x, d):
    assert d & (d - 1) == 0
    return x >> (d.bit_length() - 1)


def _sigmoid(x):
    return 1.0 / (1.0 + jnp.exp(-x))


def _silu(x):
    return x * _sigmoid(x)


def _softplus(x):
    return jnp.maximum(x, 0.0) + jnp.log(1.0 + jnp.exp(-jnp.abs(x)))


def _dot(a, b):
    return jnp.dot(a, b, preferred_element_type=F32)


def _dot_nt(a, b):
    return lax.dot_general(a, b, (((1,), (1,)), ((), ())), preferred_element_type=F32)


def _rms(x, w):
    ms = jnp.mean(x * x, axis=-1, keepdims=True)
    return x * lax.rsqrt(ms + EPS) * w


def _lambda_value(p, lam_init):
    a = jnp.sum(p[0:1, :] * p[1:2, :], axis=-1, keepdims=True)
    b = jnp.sum(p[2:3, :] * p[3:4, :], axis=-1, keepdims=True)
    return jnp.exp(a) - jnp.exp(b) + lam_init


def _proj_kernel(x_ref, nw_ref, w_ref, wg_ref, *rest, tm):
    p_ref, g_ref, a16_ref, k_out, v_out = rest[-5:]
    hn = _rms(x_ref[...], nw_ref[...]).astype(BF16)
    g_ref[...] = _dot(hn, wg_ref[...])

    def scatter_heads(out_ref, cols):
        for h in range(N_HEADS):
            out_ref[pl.ds(h, tm, stride=N_HEADS), :] = cols[:, h * HEAD_W:(h + 1) * HEAD_W]

    for seg in range(N_MAIN // GROUP_W):
        res = _dot(hn, w_ref[:, seg * GROUP_W:(seg + 1) * GROUP_W])
        if seg < N_A // GROUP_W:
            a16_ref[:, seg * GROUP_W:(seg + 1) * GROUP_W] = res.astype(BF16)
            if seg == SEG_AK:
                scatter_heads(k_out, res)
            if seg == SEG_AV:
                scatter_heads(v_out, res)
        else:
            p_ref[:, seg * GROUP_W - N_A:(seg + 1) * GROUP_W - N_A] = res


def _proj(x2d, norm_w, w_main, w_gate, layer, depth, kv_all):
    t = x2d.shape[0]
    tm = min(t, 512)
    kv_spec = pl.BlockSpec((None, N_HEADS * tm, HEAD_W), lambda i: (layer, i, 0))
    kv_shape = jax.ShapeDtypeStruct((depth, N_HEADS * t, HEAD_W), F32)
    n_in = 4
    extra_in, extra_specs, aliases = (), [], {}
    if kv_all is not None:
        extra_in = tuple(kv_all)
        extra_specs = [pl.BlockSpec(memory_space=pl.ANY)] * 2
        aliases = {n_in: 3, n_in + 1: 4}
    resident = dict(pipeline_mode=pl.Buffered(1))
    return pl.pallas_call(
        functools.partial(_proj_kernel, tm=tm),
        grid=(t // tm,),
        in_specs=[
            pl.BlockSpec((tm, D_MODEL), lambda i: (i, 0)),
            pl.BlockSpec((1, D_MODEL), lambda i: (0, 0)),
            pl.BlockSpec((D_MODEL, N_MAIN), lambda i: (0, 0), **resident),
            pl.BlockSpec((D_MODEL, LANES), lambda i: (0, 0), **resident),
        ] + extra_specs,
        out_specs=[
            pl.BlockSpec((tm, N_P), lambda i: (i, 0)),
            pl.BlockSpec((tm, LANES), lambda i: (i, 0)),
            pl.BlockSpec((tm, N_A), lambda i: (i, 0)),
            kv_spec, kv_spec,
        ],
        out_shape=[jax.ShapeDtypeStruct((t, N_P), F32), jax.ShapeDtypeStruct((t, LANES), F32),
                   jax.ShapeDtypeStruct((t, N_A), BF16), kv_shape, kv_shape],
        input_output_aliases=aliases,
        compiler_params=_cparams(("parallel",)),
        name="proj",
    )(x2d, norm_w.reshape(1, D_MODEL), w_main, w_gate, *extra_in)


def _memkv_kernel(x_ref, nw_ref, w_ref, k_ref, v_ref):
    hn = _rms(x_ref[...], nw_ref[...]).astype(BF16)
    kv = _dot(hn, w_ref[...])
    k_ref[...] = kv[:, :GROUP_W]
    v_ref[...] = kv[:, GROUP_W:]


def _memkv(mem2d, norm_mem, w_mem_kv_bf16):
    depth = norm_mem.shape[0]
    t = mem2d.shape[0]
    return pl.pallas_call(
        _memkv_kernel,
        grid=(depth,),
        in_specs=[
            pl.BlockSpec((t, D_MODEL), lambda l: (0, 0)),
            pl.BlockSpec((None, 1, D_MODEL), lambda l: (l, 0, 0)),
            pl.BlockSpec((None, D_MODEL, 2 * GROUP_W), lambda l: (l, 0, 0)),
        ],
        out_specs=[
            pl.BlockSpec((None, t, GROUP_W), lambda l: (l, 0, 0)),
            pl.BlockSpec((None, t, GROUP_W), lambda l: (l, 0, 0)),
        ],
        out_shape=[jax.ShapeDtypeStruct((depth, t, GROUP_W), F32)] * 2,
        compiler_params=_cparams(("parallel",)),
        name="memkv",
    )(mem2d, norm_mem.reshape(depth, 1, D_MODEL), w_mem_kv_bf16)


def _alibi_slope(h):
    return jnp.where(h == 0, 0.25, jnp.where(h == 1, 0.0625, jnp.where(h == 2, 0.015625, 0.00390625))).astype(F32)


def _diff_epilogue(o1, o2, lam, dnw, lam_scale, gate):
    o = o1 - lam * o2
    return _rms(o, dnw) * lam_scale * _silu(gate)


def _bf16_split3(x):
    parts = []
    for _ in range(3):
        p = float(np.asarray(x, np.float32).astype(jnp.bfloat16).astype(np.float32))
        parts.append(p)
        x = x - p
    return parts


_ALIBI_PARTS = [_bf16_split3(2.0 ** (-2.0 * (h + 1)) * LOG2E) for h in range(N_HEADS)]
POS_LO = 128
LOOP_TILES = 4


def _by_head(h, values):
    out = jnp.float32(values[-1])
    for idx in range(len(values) - 2, -1, -1):
        out = jnp.where(h == idx, jnp.float32(values[idx]), out)
    return out


def _sample_attention(x_ref, k_refs, v_refs, lam, dnw, lam_scale, ls):
    n_pages = len(k_refs)
    past = n_pages * PAGE
    rows = 2 * ls * N_HEADS
    x = x_ref[...].astype(F32)
    q = x[:, 0:GROUP_W] * (A_DH ** -0.5 * LOG2E)
    k_new = x[:, GROUP_W:2 * GROUP_W]
    v_new = x[:, 2 * GROUP_W:3 * GROUP_W]
    gate = x[:, 3 * GROUP_W:4 * GROUP_W]

    lane_q = _iota((ls, HEAD_W), 1)
    parts = []
    for h in range(N_HEADS):
        qh = q[:, h * HEAD_W:(h + 1) * HEAD_W]
        parts += [jnp.where(lane_q < A_DH, qh, 0.0), jnp.where(lane_q >= A_DH, qh, 0.0)]
    qs = jnp.concatenate(parts, axis=0)
    row_head_w = _div(_iota((rows, GROUP_W), 0), 2 * ls)
    qb = jnp.where(row_head_w == _div(_iota((rows, GROUP_W), 1), HEAD_W),
                   jnp.concatenate([qs] * N_HEADS, axis=1), 0.0).astype(BF16)
    qs = qs.astype(BF16)

    row_head = _div(_iota((rows, 1), 0), 2 * ls)
    slope2 = _alibi_slope(row_head) * LOG2E
    tok = _iota((rows, 1), 0) & (ls - 1)

    col = _iota((1, N_HEADS * PAGE), 1)
    own = (col & (N_HEADS - 1)) == row_head
    key_in_page = _div(col, N_HEADS)
    s_pages = []
    for p in range(n_pages):
        s = _dot_nt(qs, k_refs[p][...].astype(BF16))
        s = s + slope2 * (key_in_page + (p * PAGE - past)).astype(F32)
        s_pages.append(jnp.where(own, s, NEG_INF))
    lane = _iota((1, PAGE), 1)
    pad = jnp.zeros((PAGE - ls, GROUP_W), F32)
    k_pad = jnp.concatenate([k_new, pad], axis=0).astype(BF16)
    v_pad = jnp.concatenate([v_new, pad], axis=0).astype(BF16)
    s_new = _dot_nt(qb, k_pad) + slope2 * lane.astype(F32)
    s_new = jnp.where(lane <= tok, s_new, NEG_INF)

    m = s_new.max(axis=-1, keepdims=True)
    for s in s_pages:
        m = jnp.maximum(m, s.max(axis=-1, keepdims=True))
    e = jnp.exp2(s_new - m)
    l = jnp.sum(e, axis=-1, keepdims=True)
    o_new = _dot(e.astype(BF16), v_pad)
    acc = jnp.zeros((rows, HEAD_W), F32)
    for h in range(N_HEADS):
        acc = acc + jnp.where(row_head == h, o_new[:, h * HEAD_W:(h + 1) * HEAD_W], 0.0)
    for p, s in enumerate(s_pages):
        e = jnp.exp2(s - m)
        l = l + jnp.sum(e, axis=-1, keepdims=True)
        acc = acc + _dot(e.astype(BF16), v_refs[p][...].astype(BF16))
    o = acc / l

    outs = []
    for h in range(N_HEADS):
        o1 = o[2 * ls * h:2 * ls * h + ls]
        o2 = o[2 * ls * h + ls:2 * ls * (h + 1)]
        outs.append(_diff_epilogue(o1, o2, lam, dnw, lam_scale, gate[:, h * HEAD_W:(h + 1) * HEAD_W]))
    return jnp.concatenate(outs, axis=1)


def _attn_kernel(pt_ref, q_ref, k_ref, v_ref, g_ref, lamp_ref, dnw_ref, xs_ref, *rest, tq, n_pages, ls, lam_init):
    del pt_ref
    k_pages = rest[:n_pages]
    v_pages = rest[n_pages:2 * n_pages]
    o_ref, os_ref, kf_ref, qs_ref, s_a, s_b, m_ref, l_ref, acc_ref = rest[2 * n_pages:]
    h = pl.program_id(1)
    i = pl.program_id(2)
    tk = tq
    n_lane_tiles = tk // LANES
    seq = k_ref.shape[0]
    lam = _lambda_value(lamp_ref[...], lam_init)

    os_ref[...] = _sample_attention(xs_ref, k_pages, v_pages, lam, dnw_ref[...], 1.0 - lam_init, ls)

    @pl.when(i == 0)
    def _():
        def fill(r, carry):
            rows = pl.ds(pl.multiple_of(r * tk, tk), tk)
            k = k_ref[rows, :].astype(F32)
            lane = _iota((tk, HEAD_W), 1)
            pos = r * tk + _iota((tk, HEAD_W), 0)
            hi = _div(pos, POS_LO).astype(F32)
            lo = (pos & (POS_LO - 1)).astype(F32)
            for mp in range(2):
                rel = lane - (A_DH if mp == 0 else 0)
                feat = jnp.where(rel < 3, hi, jnp.where(rel < 6, lo, 0.0))
                own = (lane < A_DH) if mp == 0 else (lane >= A_DH)
                kf_ref[mp, rows, :] = jnp.where(own, k, feat).astype(BF16)
            return carry

        lax.fori_loop(0, seq // tk, fill, 0)

    q = q_ref[...].astype(F32) * (A_DH ** -0.5 * LOG2E)
    lane = _iota((1, HEAD_W), 1)
    c = [_by_head(h, [_ALIBI_PARTS[hh][part] for hh in range(N_HEADS)]) for part in range(3)]
    for mp in range(2):
        rel = lane - (A_DH if mp == 0 else 0)
        qfeat = jnp.zeros((1, HEAD_W), F32)
        for part in range(3):
            qfeat = jnp.where(rel == part, c[part] * POS_LO, jnp.where(rel == 3 + part, c[part], qfeat))
        own = (lane < A_DH) if mp == 0 else (lane >= A_DH)
        qs_ref[mp] = jnp.where(own, q, qfeat).astype(BF16)
    m_ref[...] = jnp.full(m_ref.shape, NEG_INF, F32)
    l_ref[...] = jnp.zeros(l_ref.shape, F32)
    acc_ref[...] = jnp.zeros(acc_ref.shape, F32)

    def scores(mp, j):
        return _dot_nt(qs_ref[mp], kf_ref[mp, pl.ds(pl.multiple_of(j * tk, tk), tk), :])

    def update(mp, s_ref, j, diagonal):
        def tile(cidx):
            s = s_ref[:, cidx * LANES:(cidx + 1) * LANES]
            if diagonal:
                s = jnp.where(cidx * LANES + _iota((1, LANES), 1) <= _iota((tq, 1), 0), s, NEG_INF)
            return s

        mx = tile(0)
        for cidx in range(1, n_lane_tiles):
            mx = jnp.maximum(mx, tile(cidx))
        m_prev = m_ref[mp]
        m_new = jnp.maximum(m_prev, jnp.max(mx, axis=-1, keepdims=True))
        alpha = jnp.exp2(m_prev - m_new)
        lsum = alpha * l_ref[mp]
        ps = []
        for cidx in range(n_lane_tiles):
            p = jnp.exp2(tile(cidx) - m_new)
            lsum = lsum + p
            ps.append(p.astype(BF16))
        v = v_ref[pl.ds(pl.multiple_of(j * tk, tk), tk), :]
        acc_ref[mp] = alpha * acc_ref[mp] + _dot(jnp.concatenate(ps, axis=1), v)
        l_ref[mp] = lsum
        m_ref[mp] = m_new

    s_a[...] = scores(0, 0)

    def step(j):
        s_b[...] = scores(1, j)
        update(0, s_a, j, False)
        s_a[...] = scores(0, j + 1)
        update(1, s_b, j, False)

    def quad(jj, carry):
        for u in range(LOOP_TILES):
            step(LOOP_TILES * jj + u)
        return carry

    lax.fori_loop(0, _div(i, LOOP_TILES), quad, 0)
    rem = i & (LOOP_TILES - 1)
    for u in range(1, LOOP_TILES):
        @pl.when(rem >= u)
        def _():
            step(i - rem + (u - 1))

    s_b[...] = scores(1, i)
    update(0, s_a, i, True)
    update(1, s_b, i, True)

    o1 = acc_ref[0] / jnp.sum(l_ref[0], axis=-1, keepdims=True)
    o2 = acc_ref[1] / jnp.sum(l_ref[1], axis=-1, keepdims=True)
    y = _diff_epilogue(o1, o2, lam, dnw_ref[...], 1.0 - lam_init, g_ref[...].astype(F32))
    o_ref[...] = y.astype(o_ref.dtype)


def _attn(a16, xs16, cache_k4, cache_v4, layer, page_table, lam_p, dnw, lam_init):
    b, l, _ = a16.shape
    bs, ls, _ = xs16.shape
    n_pages = page_table.shape[1]
    tq = min(l, 512)
    nq = l // tq
    assert l <= POS_LO * 256 and tq % LANES == 0
    assert b * N_HEADS * nq == bs, "one sample sequence per prompt grid step"

    def step_id(bi, h, i):
        return (bi * N_HEADS + h) * nq + i

    def page_map(p):
        return lambda bi, h, i, pt: (layer, pt[step_id(bi, h, i) * n_pages + p], 0, 0)

    page_specs = lambda: [pl.BlockSpec((None, None, N_HEADS * PAGE, HEAD_W), page_map(p)) for p in range(n_pages)]
    grid_spec = pltpu.PrefetchScalarGridSpec(
        num_scalar_prefetch=1,
        grid=(b, N_HEADS, nq),
        in_specs=[
            pl.BlockSpec((None, tq, HEAD_W), lambda bi, h, i, pt: (bi, i, SEG_AQ * N_HEADS + h)),
            pl.BlockSpec((None, l, HEAD_W), lambda bi, h, i, pt: (bi, 0, SEG_AK * N_HEADS + h)),
            pl.BlockSpec((None, l, HEAD_W), lambda bi, h, i, pt: (bi, 0, SEG_AV * N_HEADS + h)),
            pl.BlockSpec((None, tq, HEAD_W), lambda bi, h, i, pt: (bi, i, SEG_AG * N_HEADS + h)),
            pl.BlockSpec((4, A_DH), lambda bi, h, i, pt: (0, 0)),
            pl.BlockSpec((1, HEAD_W), lambda bi, h, i, pt: (0, 0)),
            pl.BlockSpec((None, ls, N_A), lambda bi, h, i, pt: (step_id(bi, h, i), 0, 0)),
        ] + page_specs() + page_specs(),
        out_specs=[
            pl.BlockSpec((None, tq, HEAD_W), lambda bi, h, i, pt: (bi, i, h)),
            pl.BlockSpec((None, ls, GROUP_W), lambda bi, h, i, pt: (step_id(bi, h, i), 0, 0)),
        ],
        scratch_shapes=[
            pltpu.VMEM((2, l, HEAD_W), BF16),
            pltpu.VMEM((2, tq, HEAD_W), BF16),
            pltpu.VMEM((tq, tq), F32),
            pltpu.VMEM((tq, tq), F32),
            pltpu.VMEM((2, tq, LANES), F32),
            pltpu.VMEM((2, tq, LANES), F32),
            pltpu.VMEM((2, tq, HEAD_W), F32),
        ],
    )
    return pl.pallas_call(
        functools.partial(_attn_kernel, tq=tq, n_pages=n_pages, ls=ls, lam_init=lam_init),
        grid_spec=grid_spec,
        out_shape=[jax.ShapeDtypeStruct((b, l, GROUP_W), BF16), jax.ShapeDtypeStruct((bs, ls, GROUP_W), F32)],
        compiler_params=_cparams(("parallel", "parallel", "arbitrary")),
        name="attn",
    )(page_table.reshape(-1), a16, a16, a16, a16, lam_p, dnw.reshape(1, HEAD_W), xs16,
      *([cache_k4] * n_pages), *([cache_v4] * n_pages))


def _cross_kernel(q_ref, g_ref, mk_ref, mv_ref, o_ref, *, nseq):
    for s in range(nseq):
        q = q_ref[s] * (HEAD_W ** -0.5)
        gate = g_ref[s]
        mk = mk_ref[s].astype(BF16)
        mv = mv_ref[s].astype(BF16)
        outs = []
        for h in range(N_HEADS):
            sl = slice(h * HEAD_W, (h + 1) * HEAD_W)
            sc = _dot_nt(q[:, sl].astype(BF16), mk[:, sl])
            e = jnp.exp(sc - sc.max(axis=-1, keepdims=True))
            o = _dot(e.astype(BF16), mv[:, sl]) / jnp.sum(e, axis=-1, keepdims=True)
            outs.append(o * _silu(gate[:, sl]))
        o_ref[s] = jnp.concatenate(outs, axis=1).astype(o_ref.dtype)


def _cross_prompt(p4, mk, mv):
    nb, nt, tq, _ = p4.shape
    n_mem = mk.shape[1]
    return pl.pallas_call(
        functools.partial(_cross_kernel, nseq=1),
        grid=(nb, nt),
        in_specs=[
            pl.BlockSpec((1, None, tq, GROUP_W), lambda b, i: (b, i, 0, SEG_XQ)),
            pl.BlockSpec((1, None, tq, GROUP_W), lambda b, i: (b, i, 0, SEG_XG)),
            pl.BlockSpec((1, n_mem, GROUP_W), lambda b, i: (b, 0, 0)),
            pl.BlockSpec((1, n_mem, GROUP_W), lambda b, i: (b, 0, 0)),
        ],
        out_specs=pl.BlockSpec((1, None, tq, GROUP_W), lambda b, i: (b, i, 0, 0)),
        out_shape=jax.ShapeDtypeStruct((nb, nt, tq, GROUP_W), BF16),
        compiler_params=_cparams(("parallel", "parallel")),
        name="cross_prompt",
    )(p4, p4, mk, mv)


def _cross_sample_kernel(q_ref, g_ref, mk_ref, mv_ref, o_ref, *, nseq, ls):
    rows = N_HEADS * ls
    n_rows = mk_ref.shape[1]
    own = (_iota((1, n_rows), 1) & (N_HEADS - 1)) == _div(_iota((rows, 1), 0), ls)
    for s in range(nseq):
        q = q_ref[s] * (HEAD_W ** -0.5)
        gate = g_ref[s]
        qx = jnp.concatenate([q[:, h * HEAD_W:(h + 1) * HEAD_W] for h in range(N_HEADS)], axis=0)
        sc = jnp.where(own, _dot_nt(qx.astype(BF16), mk_ref[s].astype(BF16)), NEG_INF)
        e = jnp.exp(sc - sc.max(axis=-1, keepdims=True))
        o = _dot(e.astype(BF16), mv_ref[s].astype(BF16)) / jnp.sum(e, axis=-1, keepdims=True)
        o = jnp.concatenate([o[h * ls:(h + 1) * ls] for h in range(N_HEADS)], axis=1)
        o_ref[s] = o * _silu(gate)


def _cross_sample(p3, mk, mv, layer):
    bs, ls, _ = p3.shape
    n_rows = mk.shape[2]
    nseq = 8
    return pl.pallas_call(
        functools.partial(_cross_sample_kernel, nseq=nseq, ls=ls),
        grid=(bs // nseq,),
        in_specs=[
            pl.BlockSpec((nseq, ls, GROUP_W), lambda b: (b, 0, SEG_XQ)),
            pl.BlockSpec((nseq, ls, GROUP_W), lambda b: (b, 0, SEG_XG)),
            pl.BlockSpec((None, nseq, n_rows, HEAD_W), lambda b: (layer, b, 0, 0)),
            pl.BlockSpec((None, nseq, n_rows, HEAD_W), lambda b: (layer, b, 0, 0)),
        ],
        out_specs=pl.BlockSpec((nseq, ls, GROUP_W), lambda b: (b, 0, 0)),
        out_shape=jax.ShapeDtypeStruct((bs, ls, GROUP_W), F32),
        compiler_params=_cparams(("parallel",)),
        name="cross_sample",
    )(p3, p3, mk, mv)


def _col(x, lane_idx):
    lane = _iota(x.shape, 1)
    return jnp.broadcast_to(jnp.sum(jnp.where(lane == lane_idx, x, 0.0), axis=-1, keepdims=True), x.shape)


def _l2_normalize(x, scale):
    ss = jnp.broadcast_to(jnp.sum(x * x, axis=-1, keepdims=True), x.shape)
    return x * (lax.rsqrt(ss + EPS) * scale)


def _split3_dot(mat_bf16, x):
    hi = x.astype(BF16)
    r1 = x - hi.astype(F32)
    mid = r1.astype(BF16)
    lo = (r1 - mid.astype(F32)).astype(BF16)
    return _dot(mat_bf16, hi) + _dot(mat_bf16, mid) + _dot(mat_bf16, lo)


def _run_interleaved(*gens):
    results = [None] * len(gens)
    live = list(enumerate(gens))
    while live:
        still = []
        for idx, g in live:
            try:
                next(g)
                still.append((idx, g))
            except StopIteration as done:
                results[idx] = done.value
        live = still
    return results


def _gdn_blocks(blocks, alog_row, dt_row, blk):
    return _run_interleaved(_gdn_blocks_staged(blocks, alog_row, dt_row, blk))[0]


def _gdn_blocks_staged(blocks, alog_row, dt_row, blk):
    sh = int(math.log2(blk))
    n_pairs = N_HEADS // 2
    lane = _iota((1, LANES), 1)
    neg_a = jnp.where(lane < N_HEADS, -jnp.exp(alog_row), 0.0)
    ti = _iota((CHUNK, CHUNK), 0)
    tj = _iota((CHUNK, CHUNK), 1)
    same_t = (ti >> sh) == (tj >> sh)
    tri_incl = (same_t & (tj <= ti)).astype(BF16)
    tri_all = same_t.astype(BF16)
    pi = _iota((CHUNK, LANES), 0)
    pj = _iota((CHUNK, LANES), 1) & (CHUNK - 1)
    left = _iota((CHUNK, LANES), 1) < CHUNK
    same = (pi >> sh) == (pj >> sh)
    incl = same & (pj <= pi)
    strict = same & (pj < pi)
    eye = jnp.where(pi == pj, 1.0, 0.0)
    zeros_h = jnp.zeros((CHUNK, HEAD_W), F32)

    def block_diag(m):
        return jnp.concatenate([jnp.where(left, m, 0.0), jnp.where(left, 0.0, m)], axis=0)

    st = []
    for qkv, ab in blocks:
        hsl = [slice(h * HEAD_W, (h + 1) * HEAD_W) for h in range(N_HEADS)]
        q = [qkv[0][:, sl] for sl in hsl]
        k = [qkv[1][:, sl] for sl in hsl]
        v = [qkv[2][:, sl] for sl in hsl]
        qn = [_l2_normalize(x, HEAD_W ** -0.5) for x in q]
        kn = [_l2_normalize(x, 1.0) for x in k]
        g = neg_a * _softplus(ab + dt_row)
        beta_full = _sigmoid(ab)
        gcum = _split3_dot(tri_incl, g)
        gtot = _split3_dot(tri_all, g)
        gc = [_col(gcum, h) for h in range(N_HEADS)]
        gl = [_col(gtot, h) for h in range(N_HEADS)]
        beta = [_col(beta_full, N_HEADS + h) for h in range(N_HEADS)]
        gt = jnp.concatenate([gcum, gcum], axis=0).T
        kb = [kn[h] * beta[h] for h in range(N_HEADS)]
        eg = [jnp.exp(gc[h]) for h in range(N_HEADS)]
        blk_state = dict(gl=gl, qg=[qn[h] * eg[h] for h in range(N_HEADS)],
                         kdec=[kn[h] * jnp.exp(gl[h] - gc[h]) for h in range(N_HEADS)],
                         decay=[], lhs=[], rnt=[], rhs=[])
        for p in range(n_pairs):
            a, b = 2 * p, 2 * p + 1
            gr = jnp.where(lane < CHUNK, gt[a:a + 1, :], gt[b:b + 1, :])
            gcp = jnp.where(left, gc[a], gc[b])
            blk_state["decay"].append(jnp.where(incl, jnp.exp(jnp.where(incl, gcp - gr, 0.0)), 0.0))
            blk_state["lhs"].append(jnp.concatenate(
                [jnp.concatenate([kb[a], kb[b]], axis=1), jnp.concatenate([qn[a], qn[b]], axis=1)],
                axis=0).astype(BF16))
            blk_state["rnt"].append(jnp.concatenate(
                [jnp.concatenate([kn[a], zeros_h], axis=1), jnp.concatenate([zeros_h, kn[b]], axis=1)],
                axis=0).astype(BF16))
            blk_state["rhs"].append(jnp.concatenate(
                [jnp.concatenate([v[a] * beta[a], kb[a] * eg[a]], axis=1),
                 jnp.concatenate([v[b] * beta[b], kb[b] * eg[b]], axis=1)], axis=0).astype(BF16))
        st.append(blk_state)
        yield

    for s in st:
        s["nmat"], s["qkm"] = [], []
        for p in range(n_pairs):
            kq = _dot_nt(s["lhs"][p], s["rnt"][p])
            s["nmat"].append(jnp.where(strict, kq[0:CHUNK] * s["decay"][p], 0.0))
            s["qkm"].append(kq[CHUNK:2 * CHUNK] * s["decay"][p])
    yield

    first = ((pi >> 1) == (pj >> 1)) & ((pi & 1) == 1) & ((pj & 1) == 0)
    for s in st:
        s["x"] = [eye - jnp.where(first, nm, 0.0) for nm in s["nmat"]]
    b = 2
    while b < blk:
        sb = int(math.log2(2 * b))
        mask = ((pi >> sb) == (pj >> sb)) & ((pi & (2 * b - 1)) >= b) & ((pj & (2 * b - 1)) < b)
        for s in st:
            s["xbd"] = [block_diag(x).astype(BF16) for x in s["x"]]
            s["t"] = [_dot(s["x"][p].astype(BF16), block_diag(jnp.where(mask, s["nmat"][p], 0.0)).astype(BF16))
                      for p in range(n_pairs)]
        yield
        for s in st:
            s["x"] = [s["x"][p] - _dot(s["t"][p].astype(BF16), s["xbd"][p]) for p in range(n_pairs)]
        yield
        b *= 2

    outs = []
    for s in st:
        u, w = [], []
        for p in range(n_pairs):
            sol = _dot(block_diag(s["x"][p]).astype(BF16), s["rhs"][p])
            for half in range(2):
                rows = slice(half * CHUNK, (half + 1) * CHUNK)
                u.append(sol[rows, 0:HEAD_W])
                w.append(sol[rows, HEAD_W:2 * HEAD_W])
        outs.append(dict(u=u, w=w, qg=s["qg"], kdec=s["kdec"], gl=s["gl"],
                         qkbd=[block_diag(m).astype(BF16) for m in s["qkm"]]))
    return outs


def _gdn_outputs(pre, vnew, qs, gate, dnw):
    outs = [None] * N_HEADS
    for p in range(N_HEADS // 2):
        a, b = 2 * p, 2 * p + 1
        o = jnp.concatenate([qs[a], qs[b]], axis=0) + _dot(
            pre["qkbd"][p], jnp.concatenate([vnew[a], vnew[b]], axis=0).astype(BF16))
        outs[a], outs[b] = o[0:CHUNK], o[CHUNK:2 * CHUNK]
    return jnp.concatenate([_rms(outs[h], dnw) * _silu(gate[:, h * HEAD_W:(h + 1) * HEAD_W])
                            for h in range(N_HEADS)], axis=1)


def _pad_t(a):
    return jnp.concatenate([a, jnp.zeros((LANES - CHUNK, LANES), F32)], axis=0).T


def _pad_rows(a):
    return jnp.concatenate([a, jnp.zeros((LANES - CHUNK, LANES), F32)], axis=0)


def _conv_taps(xbuf, p, base, nrows, cw):
    acc = xbuf[p, pl.ds(base + 5, nrows), :] * cw[0:1, p * GROUP_W:(p + 1) * GROUP_W]
    for t in range(1, CONV_W):
        acc = acc + xbuf[p, pl.ds(base + 5 + t, nrows), :] * cw[t:t + 1, p * GROUP_W:(p + 1) * GROUP_W]
    return _silu(acc)


def _gdnp_kernel(bq_ref, bk_ref, bv_ref, bg_ref, ab_ref, cw_ref, alog_ref, dt_ref, dnw_ref,
                 o_ref, sout_ref, cout_ref, xbuf, ybuf, s_scr, *, tb):
    t = pl.program_id(1)

    @pl.when(t == 0)
    def _():
        xbuf[:, 0:8, :] = jnp.zeros((3, 8, GROUP_W), F32)
        s_scr[...] = jnp.zeros(s_scr.shape, F32)

    x_refs = (bq_ref, bk_ref, bv_ref)
    for p in range(3):
        xbuf[p, 8:8 + tb, :] = x_refs[p][...]
    cw = cw_ref[...]
    gate = bg_ref[...]
    ab = ab_ref[...]
    group_rows = GDN_GROUP * CHUNK
    n_groups = tb // group_rows

    def conv_staged(g):
        for p in range(3):
            ybuf[p, g * group_rows:(g + 1) * group_rows, :] = _conv_taps(xbuf, p, g * group_rows, group_rows, cw)
            yield

    def prepass_staged(g):
        rows = [slice(g * group_rows + c * CHUNK, g * group_rows + (c + 1) * CHUNK) for c in range(GDN_GROUP)]
        return _gdn_blocks_staged([([ybuf[p, rs, :] for p in range(3)], ab[rs]) for rs in rows],
                                  alog_ref[...], dt_ref[...], CHUNK)

    def sequential_staged(g, pre):
        for c in range(GDN_GROUP):
            rs = slice(g * group_rows + c * CHUNK, g * group_rows + (c + 1) * CHUNK)
            blk = pre[c]
            vnews, qss = [], []
            for h in range(N_HEADS):
                s_old = s_scr[h]
                wq = jnp.concatenate([blk["w"][h], blk["qg"][h]], axis=0).astype(BF16)
                r = _dot(wq, s_old.astype(BF16))
                vn = blk["u"][h] - r[0:CHUNK]
                vnews.append(vn)
                qss.append(r[CHUNK:2 * CHUNK])
                upd = _dot(_pad_t(blk["kdec"][h]).astype(BF16), _pad_rows(vn).astype(BF16))
                s_scr[h] = s_old * jnp.exp(blk["gl"][h][0:1, :]) + upd
            yield
            o_ref[rs, :] = _gdn_outputs(blk, vnews, qss, gate[rs], dnw_ref[...]).astype(o_ref.dtype)
            yield

    def nothing():
        return
        yield

    _run_interleaved(conv_staged(0))
    pre_prev = None
    for g in range(n_groups):
        conv_next = conv_staged(g + 1) if g + 1 < n_groups else nothing()
        seq_prev = sequential_staged(g - 1, pre_prev) if g > 0 else nothing()
        pre_prev = _run_interleaved(prepass_staged(g), conv_next, seq_prev)[0]
    _run_interleaved(sequential_staged(n_groups - 1, pre_prev))
    for p in range(3):
        xbuf[p, 5:8, :] = xbuf[p, tb + 5:tb + 8, :]

    @pl.when(t == pl.num_programs(1) - 1)
    def _():
        sout_ref[...] = s_scr[...]
        for p in range(3):
            cout_ref[:, p * GROUP_W:(p + 1) * GROUP_W] = xbuf[p, 5:8, :]


def _gdn_prompt(p3, g3, conv_w, alog_row, dt_row, dnw):
    b, l, _ = p3.shape
    tb = min(l, 1024)
    seg = lambda s: pl.BlockSpec((None, tb, GROUP_W), lambda bi, t: (bi, t, s))
    const = lambda shape: pl.BlockSpec(shape, lambda bi, t: (0,) * len(shape))
    return pl.pallas_call(
        functools.partial(_gdnp_kernel, tb=tb),
        grid=(b, l // tb),
        in_specs=[seg(SEG_BQ), seg(SEG_BK), seg(SEG_BV), seg(SEG_BG),
                  pl.BlockSpec((None, tb, LANES), lambda bi, t: (bi, t, 0)),
                  const((CONV_W, 3 * GROUP_W)), const((1, LANES)), const((1, LANES)), const((1, HEAD_W))],
        out_specs=[
            pl.BlockSpec((None, tb, GROUP_W), lambda bi, t: (bi, t, 0)),
            pl.BlockSpec((None, N_HEADS, HEAD_W, HEAD_W), lambda bi, t: (bi, 0, 0, 0)),
            pl.BlockSpec((None, CONV_W - 1, 3 * GROUP_W), lambda bi, t: (bi, 0, 0)),
        ],
        out_shape=[
            jax.ShapeDtypeStruct((b, l, GROUP_W), BF16),
            jax.ShapeDtypeStruct((b, N_HEADS, HEAD_W, HEAD_W), F32),
            jax.ShapeDtypeStruct((b, CONV_W - 1, 3 * GROUP_W), F32),
        ],
        scratch_shapes=[pltpu.VMEM((3, tb + 8, GROUP_W), F32), pltpu.VMEM((3, tb, GROUP_W), F32),
                        pltpu.VMEM((N_HEADS, HEAD_W, HEAD_W), F32)],
        compiler_params=_cparams(("parallel", "arbitrary")),
        name="gdn_prompt",
    )(p3, p3, p3, p3, g3, conv_w, alog_row, dt_row, dnw.reshape(1, HEAD_W))


def _gdns_kernel(bq_ref, bk_ref, bv_ref, bg_ref, ab_ref, cw_ref, alog_ref, dt_ref, dnw_ref, c0_ref, s0_ref,
                 *rest, ls, nseq):
    o_ref, sout_ref, cout_ref, xbuf = rest[-4:]
    x_refs = (bq_ref, bk_ref, bv_ref)
    cw = cw_ref[...]
    qkv = []
    for p in range(3):
        for s in range(nseq):
            xbuf[p, s * 16 + 5:s * 16 + 8, :] = c0_ref[s, :, p * GROUP_W:(p + 1) * GROUP_W]
            xbuf[p, s * 16 + 8:s * 16 + 8 + ls, :] = x_refs[p][s * ls:(s + 1) * ls, :]
        qkv.append(jnp.concatenate([_conv_taps(xbuf, p, s * 16, ls, cw) for s in range(nseq)], axis=0))
        for s in range(nseq):
            cout_ref[s, :, p * GROUP_W:(p + 1) * GROUP_W] = xbuf[p, s * 16 + 5 + ls:s * 16 + 8 + ls, :]

    blk, = _gdn_blocks([(qkv, ab_ref[...])], alog_ref[...], dt_ref[...], ls)
    seq_of_row = _div(_iota((2 * CHUNK, 1), 0) & (CHUNK - 1), ls)
    vnews, qss = [], []
    for h in range(N_HEADS):
        wq = jnp.concatenate([blk["w"][h], blk["qg"][h]], axis=0)
        r = jnp.zeros((2 * CHUNK, HEAD_W), F32)
        for s in range(nseq):
            r = r + _dot(jnp.where(seq_of_row == s, wq, 0.0).astype(BF16), s0_ref[s, h].astype(BF16))
        vn = blk["u"][h] - r[0:CHUNK]
        vnews.append(vn)
        qss.append(r[CHUNK:2 * CHUNK])
        vn_pad = _pad_rows(vn).astype(BF16)
        for s in range(nseq):
            kd = jnp.where(seq_of_row[0:CHUNK] == s, blk["kdec"][h], 0.0)
            decay = jnp.exp(blk["gl"][h][s * ls:s * ls + 1, :])
            sout_ref[s, h] = s0_ref[s, h] * decay + _dot(_pad_t(kd).astype(BF16), vn_pad)
    o_ref[...] = _gdn_outputs(blk, vnews, qss, bg_ref[...], dnw_ref[...]).astype(o_ref.dtype)


def _gdn_sample(p2, g2, ls, conv_w, alog_row, dt_row, dnw, conv0, s0, layer, s_all):
    bs = s0.shape[1]
    nseq = CHUNK // ls
    seg = lambda s: pl.BlockSpec((CHUNK, GROUP_W), lambda i: (i, s))
    const = lambda shape: pl.BlockSpec(shape, lambda i: (0,) * len(shape))
    n_in = 11
    extra_in, extra_specs, aliases = (), [], {}
    if s_all is not None:
        extra_in, extra_specs, aliases = (s_all,), [pl.BlockSpec(memory_space=pl.ANY)], {n_in: 1}
    return pl.pallas_call(
        functools.partial(_gdns_kernel, ls=ls, nseq=nseq),
        grid=(bs // nseq,),
        in_specs=[seg(SEG_BQ), seg(SEG_BK), seg(SEG_BV), seg(SEG_BG),
                  pl.BlockSpec((CHUNK, LANES), lambda i: (i, 0)),
                  const((CONV_W, 3 * GROUP_W)), const((1, LANES)), const((1, LANES)), const((1, HEAD_W)),
                  pl.BlockSpec((None, nseq, CONV_W - 1, 3 * GROUP_W), lambda i: (layer, i, 0, 0)),
                  pl.BlockSpec((None, nseq, N_HEADS, HEAD_W, HEAD_W), lambda i: (layer, i, 0, 0, 0))] + extra_specs,
        out_specs=[
            pl.BlockSpec((CHUNK, GROUP_W), lambda i: (i, 0)),
            pl.BlockSpec((None, nseq, N_HEADS, HEAD_W, HEAD_W), lambda i: (layer, i, 0, 0, 0)),
            pl.BlockSpec((nseq, CONV_W - 1, 3 * GROUP_W), lambda i: (i, 0, 0)),
        ],
        out_shape=[
            jax.ShapeDtypeStruct((bs * ls, GROUP_W), F32),
            jax.ShapeDtypeStruct(s0.shape, F32),
            jax.ShapeDtypeStruct(conv0.shape[1:], F32),
        ],
        scratch_shapes=[pltpu.VMEM((3, nseq * 16, GROUP_W), F32)],
        input_output_aliases=aliases,
        compiler_params=_cparams(("parallel",)),
        name="gdn_sample",
    )(p2, p2, p2, p2, g2, conv_w, alog_row, dt_row, dnw.reshape(1, HEAD_W), conv0, s0, *extra_in)


def _outproj_kernel(h_ref, oa_ref, ob_ref, ox_ref, w_ref, nf_ref, o_ref, *, final):
    acc = h_ref[...]
    for n, r in enumerate((oa_ref, ob_ref, ox_ref)):
        acc = acc + _dot(r[...].astype(BF16), w_ref[n * GROUP_W:(n + 1) * GROUP_W, :])
    o_ref[...] = _rms(acc, nf_ref[...]) if final else acc


def _outproj(h2d, oa, ob, ox, w_out, norm_f, final):
    t = h2d.shape[0]
    tm = min(t, 1024)
    row = lambda w: pl.BlockSpec((tm, w), lambda i: (i, 0))
    return pl.pallas_call(
        functools.partial(_outproj_kernel, final=final),
        grid=(t // tm,),
        in_specs=[row(D_MODEL), row(GROUP_W), row(GROUP_W), row(GROUP_W),
                  pl.BlockSpec((3 * GROUP_W, D_MODEL), lambda i: (0, 0)),
                  pl.BlockSpec((1, D_MODEL), lambda i: (0, 0))],
        out_specs=row(D_MODEL),
        out_shape=jax.ShapeDtypeStruct((t, D_MODEL), F32),
        compiler_params=_cparams(("parallel",)),
        name="outproj",
    )(h2d, oa, ob, ox, w_out, norm_f.reshape(1, D_MODEL))


def _pad_lanes(v):
    return jnp.pad(v.astype(F32), (0, LANES - v.shape[0])).reshape(1, LANES)


def kernel(x_prompt, x_sample, cache_k, cache_v, state_delta, state_conv, cache_mem_k, cache_mem_v, page_table, mem_prompt, norm_w, w_in, diff_lambda, diff_norm_w, conv_w, a_log, dt_bias, delta_norm_w, norm_mem, w_mem_kv, w_out, norm_f):
    bp, lp, _ = x_prompt.shape
    bs, ls, _ = x_sample.shape
    depth = w_in.shape[0]
    n_mem = mem_prompt.shape[1]
    n_gate = 2 * N_HEADS
    main_w = N_MAIN - 2 * GROUP_W

    w_main = jnp.concatenate([w_in[:, :, :main_w], w_in[:, :, main_w + n_gate:]], axis=2).astype(BF16)
    w_gate = jnp.pad(w_in[:, :, main_w:main_w + n_gate], ((0, 0), (0, 0), (0, LANES - n_gate))).astype(BF16)
    w_out_b = w_out.astype(BF16)
    cache_k4 = cache_k.reshape(cache_k.shape[0], cache_k.shape[1], PAGE * N_HEADS, HEAD_W)
    cache_v4 = cache_v.reshape(cache_v.shape[0], cache_v.shape[1], PAGE * N_HEADS, HEAD_W)
    mem_k4 = cache_mem_k.reshape(depth, bs, n_mem * N_HEADS, HEAD_W)
    mem_v4 = cache_mem_v.reshape(depth, bs, n_mem * N_HEADS, HEAD_W)

    mk_all, mv_all = _memkv(mem_prompt.reshape(bp * n_mem, D_MODEL), norm_mem, w_mem_kv.astype(BF16))

    hp = x_prompt.reshape(bp * lp, D_MODEL)
    hs = x_sample.reshape(bs * ls, D_MODEL)
    tq_x = min(lp, 512)
    sp, cp, cs = ([] for _ in range(3))
    kv_p = kv_s = s_all = None
    for l in range(depth):
        lam_init = 0.8 - 0.6 * math.exp(-0.3 * l)
        alog_row = _pad_lanes(a_log[l])
        dt_row = _pad_lanes(dt_bias[l])
        final = l == depth - 1

        pp, gp, pa16, *kv_p = _proj(hp, norm_w[l], w_main[l], w_gate[l], l, depth, kv_p)
        ps, gs, sa16, *kv_s = _proj(hs, norm_w[l], w_main[l], w_gate[l], l, depth, kv_s)
        oa, oa_s = _attn(pa16.reshape(bp, lp, N_A), sa16.reshape(bs, ls, N_A), cache_k4, cache_v4, l, page_table,
                         diff_lambda[l], diff_norm_w[l], lam_init)

        pp3 = pp.reshape(bp, lp, N_P)
        ob, s_new, c_new = _gdn_prompt(pp3, gp.reshape(bp, lp, LANES), conv_w[l], alog_row, dt_row, delta_norm_w[l])
        ox = _cross_prompt(pp.reshape(bp, lp // tq_x, tq_x, N_P), mk_all[l].reshape(bp, n_mem, GROUP_W),
                           mv_all[l].reshape(bp, n_mem, GROUP_W))
        hp = _outproj(hp, oa.reshape(bp * lp, GROUP_W), ob.reshape(bp * lp, GROUP_W),
                      ox.reshape(bp * lp, GROUP_W), w_out_b[l], norm_f, final)
        sp.append(s_new)
        cp.append(c_new)

        ps3 = ps.reshape(bs, ls, N_P)
        ob, s_all, c_new = _gdn_sample(ps, gs, ls, conv_w[l], alog_row, dt_row, delta_norm_w[l],
                                       state_conv, state_delta, l, s_all)
        ox = _cross_sample(ps3, mem_k4, mem_v4, l)
        hs = _outproj(hs, oa_s.reshape(bs * ls, GROUP_W), ob, ox.reshape(bs * ls, GROUP_W), w_out_b[l], norm_f, final)
        cs.append(c_new)

    heads = lambda x, b, n: x.reshape(depth, b, n, N_HEADS, HEAD_W)
    return (hp.reshape(bp, lp, D_MODEL), hs.reshape(bs, ls, D_MODEL),
            heads(kv_p[0], bp, lp), heads(kv_p[1], bp, lp), jnp.stack(sp), jnp.stack(cp),
            heads(mk_all, bp, n_mem), heads(mv_all, bp, n_mem),
            heads(kv_s[0], bs, ls), heads(kv_s[1], bs, ls), s_all, jnp.stack(cs))
```

```python
import functools
import math

import jax
import jax.numpy as jnp
import numpy as np
from jax import lax
from jax.experimental import pallas as pl
from jax.experimental.pallas import tpu as pltpu

F32 = jnp.float32
BF16 = jnp.bfloat16

D_MODEL = 1024
N_HEADS = 4
HEAD_W = 128
GROUP_W = N_HEADS * HEAD_W
A_DH = 64
CONV_W = 4
CHUNK = 64
GDN_GROUP = 4
PAGE = 128
EPS = 1e-6
NEG_INF = -1e30
LOG2E = math.log2(math.e)
N_MAIN = 10 * GROUP_W
N_A = 4 * GROUP_W
N_P = N_MAIN - N_A
LANES = 128
VMEM_LIMIT = 48 * 1024 * 1024

SEG_AQ, SEG_AK, SEG_AV, SEG_AG = range(4)
SEG_BQ, SEG_BK, SEG_BV, SEG_BG, SEG_XQ, SEG_XG = range(6)


def _cparams(sem):
    return pltpu.CompilerParams(dimension_semantics=sem, vmem_limit_bytes=VMEM_LIMIT)


def _iota(shape, dim):
    return lax.broadcasted_iota(jnp.int32, shape, dim)


def _div(x, d):
    assert d & (d - 1) == 0
    return x >> (d.bit_length() - 1)


def _sigmoid(x):
    return 1.0 / (1.0 + jnp.exp(-x))


def _silu(x):
    return x * _sigmoid(x)


def _softplus(x):
    return jnp.maximum(x, 0.0) + jnp.log(1.0 + jnp.exp(-jnp.abs(x)))


def _dot(a, b):
    return jnp.dot(a, b, preferred_element_type=F32)


def _dot_nt(a, b):
    return lax.dot_general(a, b, (((1,), (1,)), ((), ())), preferred_element_type=F32)


def _rms(x, w):
    ms = jnp.mean(x * x, axis=-1, keepdims=True)
    return x * lax.rsqrt(ms + EPS) * w


def _lambda_value(p, lam_init):
    a = jnp.sum(p[0:1, :] * p[1:2, :], axis=-1, keepdims=True)
    b = jnp.sum(p[2:3, :] * p[3:4, :], axis=-1, keepdims=True)
    return jnp.exp(a) - jnp.exp(b) + lam_init


def _proj_kernel(x_ref, nw_ref, w_ref, wg_ref, *rest, tm):
    p_ref, g_ref, a16_ref, k_out, v_out = rest[-5:]
    hn = _rms(x_ref[...], nw_ref[...]).astype(BF16)
    g_ref[...] = _dot(hn, wg_ref[...])

    def scatter_heads(out_ref, cols):
        for h in range(N_HEADS):
            out_ref[pl.ds(h, tm, stride=N_HEADS), :] = cols[:, h * HEAD_W:(h + 1) * HEAD_W]

    for seg in range(N_MAIN // GROUP_W):
        res = _dot(hn, w_ref[:, seg * GROUP_W:(seg + 1) * GROUP_W])
        if seg < N_A // GROUP_W:
            a16_ref[:, seg * GROUP_W:(seg + 1) * GROUP_W] = res.astype(BF16)
            if seg == SEG_AK:
                scatter_heads(k_out, res)
            if seg == SEG_AV:
                scatter_heads(v_out, res)
        else:
            p_ref[:, seg * GROUP_W - N_A:(seg + 1) * GROUP_W - N_A] = res


def _proj(x2d, norm_w, w_main, w_gate, layer, depth, kv_all):
    t = x2d.shape[0]
    tm = min(t, 512)
    kv_spec = pl.BlockSpec((None, N_HEADS * tm, HEAD_W), lambda i: (layer, i, 0))
    kv_shape = jax.ShapeDtypeStruct((depth, N_HEADS * t, HEAD_W), F32)
    n_in = 4
    extra_in, extra_specs, aliases = (), [], {}
    if kv_all is not None:
        extra_in = tuple(kv_all)
        extra_specs = [pl.BlockSpec(memory_space=pl.ANY)] * 2
        aliases = {n_in: 3, n_in + 1: 4}
    resident = dict(pipeline_mode=pl.Buffered(1))
    return pl.pallas_call(
        functools.partial(_proj_kernel, tm=tm),
        grid=(t // tm,),
        in_specs=[
            pl.BlockSpec((tm, D_MODEL), lambda i: (i, 0)),
            pl.BlockSpec((1, D_MODEL), lambda i: (0, 0)),
            pl.BlockSpec((D_MODEL, N_MAIN), lambda i: (0, 0), **resident),
            pl.BlockSpec((D_MODEL, LANES), lambda i: (0, 0), **resident),
        ] + extra_specs,
        out_specs=[
            pl.BlockSpec((tm, N_P), lambda i: (i, 0)),
            pl.BlockSpec((tm, LANES), lambda i: (i, 0)),
            pl.BlockSpec((tm, N_A), lambda i: (i, 0)),
            kv_spec, kv_spec,
        ],
        out_shape=[jax.ShapeDtypeStruct((t, N_P), F32), jax.ShapeDtypeStruct((t, LANES), F32),
                   jax.ShapeDtypeStruct((t, N_A), BF16), kv_shape, kv_shape],
        input_output_aliases=aliases,
        compiler_params=_cparams(("parallel",)),
        name="proj",
    )(x2d, norm_w.reshape(1, D_MODEL), w_main, w_gate, *extra_in)


def _memkv_kernel(x_ref, nw_ref, w_ref, k_ref, v_ref):
    hn = _rms(x_ref[...], nw_ref[...]).astype(BF16)
    kv = _dot(hn, w_ref[...])
    k_ref[...] = kv[:, :GROUP_W]
    v_ref[...] = kv[:, GROUP_W:]


def _memkv(mem2d, norm_mem, w_mem_kv_bf16):
    depth = norm_mem.shape[0]
    t = mem2d.shape[0]
    return pl.pallas_call(
        _memkv_kernel,
        grid=(depth,),
        in_specs=[
            pl.BlockSpec((t, D_MODEL), lambda l: (0, 0)),
            pl.BlockSpec((None, 1, D_MODEL), lambda l: (l, 0, 0)),
            pl.BlockSpec((None, D_MODEL, 2 * GROUP_W), lambda l: (l, 0, 0)),
        ],
        out_specs=[
            pl.BlockSpec((None, t, GROUP_W), lambda l: (l, 0, 0)),
            pl.BlockSpec((None, t, GROUP_W), lambda l: (l, 0, 0)),
        ],
        out_shape=[jax.ShapeDtypeStruct((depth, t, GROUP_W), F32)] * 2,
        compiler_params=_cparams(("parallel",)),
        name="memkv",
    )(mem2d, norm_mem.reshape(depth, 1, D_MODEL), w_mem_kv_bf16)


def _alibi_slope(h):
    return jnp.where(h == 0, 0.25, jnp.where(h == 1, 0.0625, jnp.where(h == 2, 0.015625, 0.00390625))).astype(F32)


def _diff_epilogue(o1, o2, lam, dnw, lam_scale, gate):
    o = o1 - lam * o2
    return _rms(o, dnw) * lam_scale * _silu(gate)


def _bf16_split3(x):
    parts = []
    for _ in range(3):
        p = float(np.asarray(x, np.float32).astype(jnp.bfloat16).astype(np.float32))
        parts.append(p)
        x = x - p
    return parts


_ALIBI_PARTS = [_bf16_split3(2.0 ** (-2.0 * (h + 1)) * LOG2E) for h in range(N_HEADS)]
POS_LO = 128
LOOP_TILES = 4


def _by_head(h, values):
    out = jnp.float32(values[-1])
    for idx in range(len(values) - 2, -1, -1):
        out = jnp.where(h == idx, jnp.float32(values[idx]), out)
    return out


def _sample_attention(x_ref, k_refs, v_refs, lam, dnw, lam_scale, ls):
    n_pages = len(k_refs)
    past = n_pages * PAGE
    rows = 2 * ls * N_HEADS
    x = x_ref[...].astype(F32)
    q = x[:, 0:GROUP_W] * (A_DH ** -0.5 * LOG2E)
    k_new = x[:, GROUP_W:2 * GROUP_W]
    v_new = x[:, 2 * GROUP_W:3 * GROUP_W]
    gate = x[:, 3 * GROUP_W:4 * GROUP_W]

    lane_q = _iota((ls, HEAD_W), 1)
    parts = []
    for h in range(N_HEADS):
        qh = q[:, h * HEAD_W:(h + 1) * HEAD_W]
        parts += [jnp.where(lane_q < A_DH, qh, 0.0), jnp.where(lane_q >= A_DH, qh, 0.0)]
    qs = jnp.concatenate(parts, axis=0)
    row_head_w = _div(_iota((rows, GROUP_W), 0), 2 * ls)
    qb = jnp.where(row_head_w == _div(_iota((rows, GROUP_W), 1), HEAD_W),
                   jnp.concatenate([qs] * N_HEADS, axis=1), 0.0).astype(BF16)
    qs = qs.astype(BF16)

    row_head = _div(_iota((rows, 1), 0), 2 * ls)
    slope2 = _alibi_slope(row_head) * LOG2E
    tok = _iota((rows, 1), 0) & (ls - 1)

    col = _iota((1, N_HEADS * PAGE), 1)
    own = (col & (N_HEADS - 1)) == row_head
    key_in_page = _div(col, N_HEADS)
    s_pages = []
    for p in range(n_pages):
        s = _dot_nt(qs, k_refs[p][...].astype(BF16))
        s = s + slope2 * (key_in_page + (p * PAGE - past)).astype(F32)
        s_pages.append(jnp.where(own, s, NEG_INF))
    lane = _iota((1, PAGE), 1)
    pad = jnp.zeros((PAGE - ls, GROUP_W), F32)
    k_pad = jnp.concatenate([k_new, pad], axis=0).astype(BF16)
    v_pad = jnp.concatenate([v_new, pad], axis=0).astype(BF16)
    s_new = _dot_nt(qb, k_pad) + slope2 * lane.astype(F32)
    s_new = jnp.where(lane <= tok, s_new, NEG_INF)

    m = s_new.max(axis=-1, keepdims=True)
    for s in s_pages:
        m = jnp.maximum(m, s.max(axis=-1, keepdims=True))
    e = jnp.exp2(s_new - m)
    l = jnp.sum(e, axis=-1, keepdims=True)
    o_new = _dot(e.astype(BF16), v_pad)
    acc = jnp.zeros((rows, HEAD_W), F32)
    for h in range(N_HEADS):
        acc = acc + jnp.where(row_head == h, o_new[:, h * HEAD_W:(h + 1) * HEAD_W], 0.0)
    for p, s in enumerate(s_pages):
        e = jnp.exp2(s - m)
        l = l + jnp.sum(e, axis=-1, keepdims=True)
        acc = acc + _dot(e.astype(BF16), v_refs[p][...].astype(BF16))
    o = acc / l

    outs = []
    for h in range(N_HEADS):
        o1 = o[2 * ls * h:2 * ls * h + ls]
        o2 = o[2 * ls * h + ls:2 * ls * (h + 1)]
        outs.append(_diff_epilogue(o1, o2, lam, dnw, lam_scale, gate[:, h * HEAD_W:(h + 1) * HEAD_W]))
    return jnp.concatenate(outs, axis=1)


def _attn_kernel(pt_ref, q_ref, k_ref, v_ref, g_ref, lamp_ref, dnw_ref, xs_ref, *rest, tq, n_pages, ls, lam_init):
    del pt_ref
    k_pages = rest[:n_pages]
    v_pages = rest[n_pages:2 * n_pages]
    o_ref, os_ref, kf_ref, qs_ref, s_a, s_b, m_ref, l_ref, acc_ref = rest[2 * n_pages:]
    h = pl.program_id(1)
    i = pl.program_id(2)
    tk = tq
    n_lane_tiles = tk // LANES
    seq = k_ref.shape[0]
    lam = _lambda_value(lamp_ref[...], lam_init)

    @pl.when(i == 0)
    def _():
        def fill(r, carry):
            rows = pl.ds(pl.multiple_of(r * tk, tk), tk)
            k = k_ref[rows, :].astype(F32)
            lane = _iota((tk, HEAD_W), 1)
            pos = r * tk + _iota((tk, HEAD_W), 0)
            hi = _div(pos, POS_LO).astype(F32)
            lo = (pos & (POS_LO - 1)).astype(F32)
            for mp in range(2):
                rel = lane - (A_DH if mp == 0 else 0)
                feat = jnp.where(rel < 3, hi, jnp.where(rel < 6, lo, 0.0))
                own = (lane < A_DH) if mp == 0 else (lane >= A_DH)
                kf_ref[mp, rows, :] = jnp.where(own, k, feat).astype(BF16)
            return carry

        lax.fori_loop(0, seq // tk, fill, 0)

    q = q_ref[...].astype(F32) * (A_DH ** -0.5 * LOG2E)
    lane = _iota((1, HEAD_W), 1)
    c = [_by_head(h, [_ALIBI_PARTS[hh][part] for hh in range(N_HEADS)]) for part in range(3)]
    for mp in range(2):
        rel = lane - (A_DH if mp == 0 else 0)
        qfeat = jnp.zeros((1, HEAD_W), F32)
        for part in range(3):
            qfeat = jnp.where(rel == part, c[part] * POS_LO, jnp.where(rel == 3 + part, c[part], qfeat))
        own = (lane < A_DH) if mp == 0 else (lane >= A_DH)
        qs_ref[mp] = jnp.where(own, q, qfeat).astype(BF16)
    m_ref[...] = jnp.full(m_ref.shape, NEG_INF, F32)
    l_ref[...] = jnp.zeros(l_ref.shape, F32)
    acc_ref[...] = jnp.zeros(acc_ref.shape, F32)

    def scores(mp, j):
        return _dot_nt(qs_ref[mp], kf_ref[mp, pl.ds(pl.multiple_of(j * tk, tk), tk), :])

    def update(mp, s_ref, j, diagonal):
        def tile(cidx):
            s = s_ref[:, cidx * LANES:(cidx + 1) * LANES]
            if diagonal:
                s = jnp.where(cidx * LANES + _iota((1, LANES), 1) <= _iota((tq, 1), 0), s, NEG_INF)
            return s

        mx = tile(0)
        for cidx in range(1, n_lane_tiles):
            mx = jnp.maximum(mx, tile(cidx))
        m_prev = m_ref[mp]
        m_new = jnp.maximum(m_prev, jnp.max(mx, axis=-1, keepdims=True))
        alpha = jnp.exp2(m_prev - m_new)
        lsum = alpha * l_ref[mp]
        ps = []
        for cidx in range(n_lane_tiles):
            p = jnp.exp2(tile(cidx) - m_new)
            lsum = lsum + p
            ps.append(p.astype(BF16))
        v = v_ref[pl.ds(pl.multiple_of(j * tk, tk), tk), :]
        acc_ref[mp] = alpha * acc_ref[mp] + _dot(jnp.concatenate(ps, axis=1), v)
        l_ref[mp] = lsum
        m_ref[mp] = m_new

    s_a[...] = scores(0, 0)
    os_ref[...] = _sample_attention(xs_ref, k_pages, v_pages, lam, dnw_ref[...], 1.0 - lam_init, ls)

    def step(j):
        s_b[...] = scores(1, j)
        update(0, s_a, j, False)
        s_a[...] = scores(0, j + 1)
        update(1, s_b, j, False)

    def quad(jj, carry):
        for u in range(LOOP_TILES):
            step(LOOP_TILES * jj + u)
        return carry

    lax.fori_loop(0, _div(i, LOOP_TILES), quad, 0)
    rem = i & (LOOP_TILES - 1)

    @pl.when(rem >= 2)
    def _():
        step(i - rem)
        step(i - rem + 1)

    @pl.when((rem & 1) == 1)
    def _():
        step(i - 1)

    s_b[...] = scores(1, i)
    update(0, s_a, i, True)
    update(1, s_b, i, True)

    o1 = acc_ref[0] / jnp.sum(l_ref[0], axis=-1, keepdims=True)
    o2 = acc_ref[1] / jnp.sum(l_ref[1], axis=-1, keepdims=True)
    y = _diff_epilogue(o1, o2, lam, dnw_ref[...], 1.0 - lam_init, g_ref[...].astype(F32))
    o_ref[...] = y.astype(o_ref.dtype)


def _attn(a16, xs16, cache_k4, cache_v4, layer, page_table, lam_p, dnw, lam_init):
    b, l, _ = a16.shape
    bs, ls, _ = xs16.shape
    n_pages = page_table.shape[1]
    tq = min(l, 512)
    nq = l // tq
    assert l <= POS_LO * 256 and tq % LANES == 0
    assert b * N_HEADS * nq == bs, "one sample sequence per prompt grid step"

    def step_id(bi, h, i):
        return (bi * N_HEADS + h) * nq + i

    def page_map(p):
        return lambda bi, h, i, pt: (layer, pt[step_id(bi, h, i) * n_pages + p], 0, 0)

    page_specs = lambda: [pl.BlockSpec((None, None, N_HEADS * PAGE, HEAD_W), page_map(p)) for p in range(n_pages)]
    grid_spec = pltpu.PrefetchScalarGridSpec(
        num_scalar_prefetch=1,
        grid=(b, N_HEADS, nq),
        in_specs=[
            pl.BlockSpec((None, tq, HEAD_W), lambda bi, h, i, pt: (bi, i, SEG_AQ * N_HEADS + h)),
            pl.BlockSpec((None, l, HEAD_W), lambda bi, h, i, pt: (bi, 0, SEG_AK * N_HEADS + h)),
            pl.BlockSpec((None, l, HEAD_W), lambda bi, h, i, pt: (bi, 0, SEG_AV * N_HEADS + h)),
            pl.BlockSpec((None, tq, HEAD_W), lambda bi, h, i, pt: (bi, i, SEG_AG * N_HEADS + h)),
            pl.BlockSpec((4, A_DH), lambda bi, h, i, pt: (0, 0)),
            pl.BlockSpec((1, HEAD_W), lambda bi, h, i, pt: (0, 0)),
            pl.BlockSpec((None, ls, N_A), lambda bi, h, i, pt: (step_id(bi, h, i), 0, 0)),
        ] + page_specs() + page_specs(),
        out_specs=[
            pl.BlockSpec((None, tq, HEAD_W), lambda bi, h, i, pt: (bi, i, h)),
            pl.BlockSpec((None, ls, GROUP_W), lambda bi, h, i, pt: (step_id(bi, h, i), 0, 0)),
        ],
        scratch_shapes=[
            pltpu.VMEM((2, l, HEAD_W), BF16),
            pltpu.VMEM((2, tq, HEAD_W), BF16),
            pltpu.VMEM((tq, tq), F32),
            pltpu.VMEM((tq, tq), F32),
            pltpu.VMEM((2, tq, LANES), F32),
            pltpu.VMEM((2, tq, LANES), F32),
            pltpu.VMEM((2, tq, HEAD_W), F32),
        ],
    )
    return pl.pallas_call(
        functools.partial(_attn_kernel, tq=tq, n_pages=n_pages, ls=ls, lam_init=lam_init),
        grid_spec=grid_spec,
        out_shape=[jax.ShapeDtypeStruct((b, l, GROUP_W), BF16), jax.ShapeDtypeStruct((bs, ls, GROUP_W), F32)],
        compiler_params=_cparams(("parallel", "parallel", "arbitrary")),
        name="attn",
    )(page_table.reshape(-1), a16, a16, a16, a16, lam_p, dnw.reshape(1, HEAD_W), xs16,
      *([cache_k4] * n_pages), *([cache_v4] * n_pages))


def _cross_kernel(q_ref, g_ref, mk_ref, mv_ref, o_ref, *, nseq):
    for s in range(nseq):
        q = q_ref[s] * (HEAD_W ** -0.5)
        gate = g_ref[s]
        mk = mk_ref[s].astype(BF16)
        mv = mv_ref[s].astype(BF16)
        outs = []
        for h in range(N_HEADS):
            sl = slice(h * HEAD_W, (h + 1) * HEAD_W)
            sc = _dot_nt(q[:, sl].astype(BF16), mk[:, sl])
            e = jnp.exp(sc - sc.max(axis=-1, keepdims=True))
            o = _dot(e.astype(BF16), mv[:, sl]) / jnp.sum(e, axis=-1, keepdims=True)
            outs.append(o * _silu(gate[:, sl]))
        o_ref[s] = jnp.concatenate(outs, axis=1).astype(o_ref.dtype)


def _cross_prompt(p4, mk, mv):
    nb, nt, tq, _ = p4.shape
    n_mem = mk.shape[1]
    return pl.pallas_call(
        functools.partial(_cross_kernel, nseq=1),
        grid=(nb, nt),
        in_specs=[
            pl.BlockSpec((1, None, tq, GROUP_W), lambda b, i: (b, i, 0, SEG_XQ)),
            pl.BlockSpec((1, None, tq, GROUP_W), lambda b, i: (b, i, 0, SEG_XG)),
            pl.BlockSpec((1, n_mem, GROUP_W), lambda b, i: (b, 0, 0)),
            pl.BlockSpec((1, n_mem, GROUP_W), lambda b, i: (b, 0, 0)),
        ],
        out_specs=pl.BlockSpec((1, None, tq, GROUP_W), lambda b, i: (b, i, 0, 0)),
        out_shape=jax.ShapeDtypeStruct((nb, nt, tq, GROUP_W), BF16),
        compiler_params=_cparams(("parallel", "parallel")),
        name="cross_prompt",
    )(p4, p4, mk, mv)


def _cross_sample_kernel(q_ref, g_ref, mk_ref, mv_ref, o_ref, *, nseq, ls):
    rows = N_HEADS * ls
    n_rows = mk_ref.shape[1]
    own = (_iota((1, n_rows), 1) & (N_HEADS - 1)) == _div(_iota((rows, 1), 0), ls)
    for s in range(nseq):
        q = q_ref[s] * (HEAD_W ** -0.5)
        gate = g_ref[s]
        qx = jnp.concatenate([q[:, h * HEAD_W:(h + 1) * HEAD_W] for h in range(N_HEADS)], axis=0)
        sc = jnp.where(own, _dot_nt(qx.astype(BF16), mk_ref[s].astype(BF16)), NEG_INF)
        e = jnp.exp(sc - sc.max(axis=-1, keepdims=True))
        o = _dot(e.astype(BF16), mv_ref[s].astype(BF16)) / jnp.sum(e, axis=-1, keepdims=True)
        o = jnp.concatenate([o[h * ls:(h + 1) * ls] for h in range(N_HEADS)], axis=1)
        o_ref[s] = o * _silu(gate)


def _cross_sample(p3, mk, mv, layer):
    bs, ls, _ = p3.shape
    n_rows = mk.shape[2]
    nseq = 8
    return pl.pallas_call(
        functools.partial(_cross_sample_kernel, nseq=nseq, ls=ls),
        grid=(bs // nseq,),
        in_specs=[
            pl.BlockSpec((nseq, ls, GROUP_W), lambda b: (b, 0, SEG_XQ)),
            pl.BlockSpec((nseq, ls, GROUP_W), lambda b: (b, 0, SEG_XG)),
            pl.BlockSpec((None, nseq, n_rows, HEAD_W), lambda b: (layer, b, 0, 0)),
            pl.BlockSpec((None, nseq, n_rows, HEAD_W), lambda b: (layer, b, 0, 0)),
        ],
        out_specs=pl.BlockSpec((nseq, ls, GROUP_W), lambda b: (b, 0, 0)),
        out_shape=jax.ShapeDtypeStruct((bs, ls, GROUP_W), F32),
        compiler_params=_cparams(("parallel",)),
        name="cross_sample",
    )(p3, p3, mk, mv)


def _col(x, lane_idx):
    lane = _iota(x.shape, 1)
    return jnp.broadcast_to(jnp.sum(jnp.where(lane == lane_idx, x, 0.0), axis=-1, keepdims=True), x.shape)


def _l2_normalize(x, scale):
    ss = jnp.broadcast_to(jnp.sum(x * x, axis=-1, keepdims=True), x.shape)
    return x * (lax.rsqrt(ss + EPS) * scale)


def _split3_dot(mat_bf16, x):
    hi = x.astype(BF16)
    r1 = x - hi.astype(F32)
    mid = r1.astype(BF16)
    lo = (r1 - mid.astype(F32)).astype(BF16)
    return _dot(mat_bf16, hi) + _dot(mat_bf16, mid) + _dot(mat_bf16, lo)


def _run_interleaved(*gens):
    results = [None] * len(gens)
    live = list(enumerate(gens))
    while live:
        still = []
        for idx, g in live:
            try:
                next(g)
                still.append((idx, g))
            except StopIteration as done:
                results[idx] = done.value
        live = still
    return results


def _gdn_blocks(blocks, alog_row, dt_row, blk):
    return _run_interleaved(_gdn_blocks_staged(blocks, alog_row, dt_row, blk))[0]


def _gdn_blocks_staged(blocks, alog_row, dt_row, blk):
    sh = int(math.log2(blk))
    n_pairs = N_HEADS // 2
    lane = _iota((1, LANES), 1)
    neg_a = jnp.where(lane < N_HEADS, -jnp.exp(alog_row), 0.0)
    ti = _iota((CHUNK, CHUNK), 0)
    tj = _iota((CHUNK, CHUNK), 1)
    same_t = (ti >> sh) == (tj >> sh)
    tri_incl = (same_t & (tj <= ti)).astype(BF16)
    tri_all = same_t.astype(BF16)
    pi = _iota((CHUNK, LANES), 0)
    pj = _iota((CHUNK, LANES), 1) & (CHUNK - 1)
    left = _iota((CHUNK, LANES), 1) < CHUNK
    same = (pi >> sh) == (pj >> sh)
    incl = same & (pj <= pi)
    strict = same & (pj < pi)
    eye = jnp.where(pi == pj, 1.0, 0.0)
    zeros_h = jnp.zeros((CHUNK, HEAD_W), F32)

    def block_diag(m):
        return jnp.concatenate([jnp.where(left, m, 0.0), jnp.where(left, 0.0, m)], axis=0)

    st = []
    for qkv, ab in blocks:
        hsl = [slice(h * HEAD_W, (h + 1) * HEAD_W) for h in range(N_HEADS)]
        q = [qkv[0][:, sl] for sl in hsl]
        k = [qkv[1][:, sl] for sl in hsl]
        v = [qkv[2][:, sl] for sl in hsl]
        qn = [_l2_normalize(x, HEAD_W ** -0.5) for x in q]
        kn = [_l2_normalize(x, 1.0) for x in k]
        g = neg_a * _softplus(ab + dt_row)
        beta_full = _sigmoid(ab)
        gcum = _split3_dot(tri_incl, g)
        gtot = _split3_dot(tri_all, g)
        gc = [_col(gcum, h) for h in range(N_HEADS)]
        gl = [_col(gtot, h) for h in range(N_HEADS)]
        beta = [_col(beta_full, N_HEADS + h) for h in range(N_HEADS)]
        gt = jnp.concatenate([gcum, gcum], axis=0).T
        kb = [kn[h] * beta[h] for h in range(N_HEADS)]
        eg = [jnp.exp(gc[h]) for h in range(N_HEADS)]
        blk_state = dict(gl=gl, qg=[qn[h] * eg[h] for h in range(N_HEADS)],
                         kdec=[kn[h] * jnp.exp(gl[h] - gc[h]) for h in range(N_HEADS)],
                         decay=[], lhs=[], rnt=[], rhs=[])
        for p in range(n_pairs):
            a, b = 2 * p, 2 * p + 1
            gr = jnp.where(lane < CHUNK, gt[a:a + 1, :], gt[b:b + 1, :])
            gcp = jnp.where(left, gc[a], gc[b])
            blk_state["decay"].append(jnp.where(incl, jnp.exp(jnp.where(incl, gcp - gr, 0.0)), 0.0))
            blk_state["lhs"].append(jnp.concatenate(
                [jnp.concatenate([kb[a], kb[b]], axis=1), jnp.concatenate([qn[a], qn[b]], axis=1)],
                axis=0).astype(BF16))
            blk_state["rnt"].append(jnp.concatenate(
                [jnp.concatenate([kn[a], zeros_h], axis=1), jnp.concatenate([zeros_h, kn[b]], axis=1)],
                axis=0).astype(BF16))
            blk_state["rhs"].append(jnp.concatenate(
                [jnp.concatenate([v[a] * beta[a], kb[a] * eg[a]], axis=1),
                 jnp.concatenate([v[b] * beta[b], kb[b] * eg[b]], axis=1)], axis=0).astype(BF16))
        st.append(blk_state)
        yield

    for s in st:
        s["nmat"], s["qkm"] = [], []
        for p in range(n_pairs):
            kq = _dot_nt(s["lhs"][p], s["rnt"][p])
            s["nmat"].append(jnp.where(strict, kq[0:CHUNK] * s["decay"][p], 0.0))
            s["qkm"].append(kq[CHUNK:2 * CHUNK] * s["decay"][p])
    yield

    first = ((pi >> 1) == (pj >> 1)) & ((pi & 1) == 1) & ((pj & 1) == 0)
    for s in st:
        s["x"] = [eye - jnp.where(first, nm, 0.0) for nm in s["nmat"]]
    b = 2
    while b < blk:
        sb = int(math.log2(2 * b))
        mask = ((pi >> sb) == (pj >> sb)) & ((pi & (2 * b - 1)) >= b) & ((pj & (2 * b - 1)) < b)
        for s in st:
            s["xbd"] = [block_diag(x).astype(BF16) for x in s["x"]]
            s["t"] = [_dot(s["x"][p].astype(BF16), block_diag(jnp.where(mask, s["nmat"][p], 0.0)).astype(BF16))
                      for p in range(n_pairs)]
        yield
        for s in st:
            s["x"] = [s["x"][p] - _dot(s["t"][p].astype(BF16), s["xbd"][p]) for p in range(n_pairs)]
        yield
        b *= 2

    outs = []
    for s in st:
        u, w = [], []
        for p in range(n_pairs):
            sol = _dot(block_diag(s["x"][p]).astype(BF16), s["rhs"][p])
            for half in range(2):
                rows = slice(half * CHUNK, (half + 1) * CHUNK)
                u.append(sol[rows, 0:HEAD_W])
                w.append(sol[rows, HEAD_W:2 * HEAD_W])
        outs.append(dict(u=u, w=w, qg=s["qg"], kdec=s["kdec"], gl=s["gl"],
                         qkbd=[block_diag(m).astype(BF16) for m in s["qkm"]]))
    return outs


def _gdn_outputs(pre, vnew, qs, gate, dnw):
    outs = [None] * N_HEADS
    for p in range(N_HEADS // 2):
        a, b = 2 * p, 2 * p + 1
        o = jnp.concatenate([qs[a], qs[b]], axis=0) + _dot(
            pre["qkbd"][p], jnp.concatenate([vnew[a], vnew[b]], axis=0).astype(BF16))
        outs[a], outs[b] = o[0:CHUNK], o[CHUNK:2 * CHUNK]
    return jnp.concatenate([_rms(outs[h], dnw) * _silu(gate[:, h * HEAD_W:(h + 1) * HEAD_W])
                            for h in range(N_HEADS)], axis=1)


def _pad_t(a):
    return jnp.concatenate([a, jnp.zeros((LANES - CHUNK, LANES), F32)], axis=0).T


def _pad_rows(a):
    return jnp.concatenate([a, jnp.zeros((LANES - CHUNK, LANES), F32)], axis=0)


def _conv_taps(xbuf, p, base, nrows, cw):
    acc = xbuf[p, pl.ds(base + 5, nrows), :] * cw[0:1, p * GROUP_W:(p + 1) * GROUP_W]
    for t in range(1, CONV_W):
        acc = acc + xbuf[p, pl.ds(base + 5 + t, nrows), :] * cw[t:t + 1, p * GROUP_W:(p + 1) * GROUP_W]
    return _silu(acc)


def _gdnp_kernel(bq_ref, bk_ref, bv_ref, bg_ref, ab_ref, cw_ref, alog_ref, dt_ref, dnw_ref,
                 o_ref, sout_ref, cout_ref, xbuf, ybuf, s_scr, *, tb):
    t = pl.program_id(1)

    @pl.when(t == 0)
    def _():
        xbuf[:, 0:8, :] = jnp.zeros((3, 8, GROUP_W), F32)
        s_scr[...] = jnp.zeros(s_scr.shape, F32)

    x_refs = (bq_ref, bk_ref, bv_ref)
    for p in range(3):
        xbuf[p, 8:8 + tb, :] = x_refs[p][...]
    cw = cw_ref[...]
    gate = bg_ref[...]
    ab = ab_ref[...]
    group_rows = GDN_GROUP * CHUNK
    n_groups = tb // group_rows

    def conv_staged(g):
        for p in range(3):
            ybuf[p, g * group_rows:(g + 1) * group_rows, :] = _conv_taps(xbuf, p, g * group_rows, group_rows, cw)
            yield

    def prepass_staged(g):
        rows = [slice(g * group_rows + c * CHUNK, g * group_rows + (c + 1) * CHUNK) for c in range(GDN_GROUP)]
        return _gdn_blocks_staged([([ybuf[p, rs, :] for p in range(3)], ab[rs]) for rs in rows],
                                  alog_ref[...], dt_ref[...], CHUNK)

    def sequential_staged(g, pre):
        for c in range(GDN_GROUP):
            rs = slice(g * group_rows + c * CHUNK, g * group_rows + (c + 1) * CHUNK)
            blk = pre[c]
            vnews, qss = [], []
            for h in range(N_HEADS):
                s_old = s_scr[h]
                wq = jnp.concatenate([blk["w"][h], blk["qg"][h]], axis=0).astype(BF16)
                r = _dot(wq, s_old.astype(BF16))
                vn = blk["u"][h] - r[0:CHUNK]
                vnews.append(vn)
                qss.append(r[CHUNK:2 * CHUNK])
                upd = _dot(_pad_t(blk["kdec"][h]).astype(BF16), _pad_rows(vn).astype(BF16))
                s_scr[h] = s_old * jnp.exp(blk["gl"][h][0:1, :]) + upd
            yield
            o_ref[rs, :] = _gdn_outputs(blk, vnews, qss, gate[rs], dnw_ref[...]).astype(o_ref.dtype)
            yield

    def nothing():
        return
        yield

    _run_interleaved(conv_staged(0))
    pre_prev = None
    for g in range(n_groups):
        conv_next = conv_staged(g + 1) if g + 1 < n_groups else nothing()
        seq_prev = sequential_staged(g - 1, pre_prev) if g > 0 else nothing()
        pre_prev = _run_interleaved(prepass_staged(g), conv_next, seq_prev)[0]
    _run_interleaved(sequential_staged(n_groups - 1, pre_prev))
    for p in range(3):
        xbuf[p, 5:8, :] = xbuf[p, tb + 5:tb + 8, :]

    @pl.when(t == pl.num_programs(1) - 1)
    def _():
        sout_ref[...] = s_scr[...]
        for p in range(3):
            cout_ref[:, p * GROUP_W:(p + 1) * GROUP_W] = xbuf[p, 5:8, :]


def _gdn_prompt(p3, g3, conv_w, alog_row, dt_row, dnw):
    b, l, _ = p3.shape
    tb = min(l, 1024)
    seg = lambda s: pl.BlockSpec((None, tb, GROUP_W), lambda bi, t: (bi, t, s))
    const = lambda shape: pl.BlockSpec(shape, lambda bi, t: (0,) * len(shape))
    return pl.pallas_call(
        functools.partial(_gdnp_kernel, tb=tb),
        grid=(b, l // tb),
        in_specs=[seg(SEG_BQ), seg(SEG_BK), seg(SEG_BV), seg(SEG_BG),
                  pl.BlockSpec((None, tb, LANES), lambda bi, t: (bi, t, 0)),
                  const((CONV_W, 3 * GROUP_W)), const((1, LANES)), const((1, LANES)), const((1, HEAD_W))],
        out_specs=[
            pl.BlockSpec((None, tb, GROUP_W), lambda bi, t: (bi, t, 0)),
            pl.BlockSpec((None, N_HEADS, HEAD_W, HEAD_W), lambda bi, t: (bi, 0, 0, 0)),
            pl.BlockSpec((None, CONV_W - 1, 3 * GROUP_W), lambda bi, t: (bi, 0, 0)),
        ],
        out_shape=[
            jax.ShapeDtypeStruct((b, l, GROUP_W), BF16),
            jax.ShapeDtypeStruct((b, N_HEADS, HEAD_W, HEAD_W), F32),
            jax.ShapeDtypeStruct((b, CONV_W - 1, 3 * GROUP_W), F32),
        ],
        scratch_shapes=[pltpu.VMEM((3, tb + 8, GROUP_W), F32), pltpu.VMEM((3, tb, GROUP_W), F32),
                        pltpu.VMEM((N_HEADS, HEAD_W, HEAD_W), F32)],
        compiler_params=_cparams(("parallel", "arbitrary")),
        name="gdn_prompt",
    )(p3, p3, p3, p3, g3, conv_w, alog_row, dt_row, dnw.reshape(1, HEAD_W))


def _gdns_kernel(bq_ref, bk_ref, bv_ref, bg_ref, ab_ref, cw_ref, alog_ref, dt_ref, dnw_ref, c0_ref, s0_ref,
                 *rest, ls, nseq):
    o_ref, sout_ref, cout_ref, xbuf = rest[-4:]
    x_refs = (bq_ref, bk_ref, bv_ref)
    cw = cw_ref[...]
    qkv = []
    for p in range(3):
        for s in range(nseq):
            xbuf[p, s * 16 + 5:s * 16 + 8, :] = c0_ref[s, :, p * GROUP_W:(p + 1) * GROUP_W]
            xbuf[p, s * 16 + 8:s * 16 + 8 + ls, :] = x_refs[p][s * ls:(s + 1) * ls, :]
        qkv.append(jnp.concatenate([_conv_taps(xbuf, p, s * 16, ls, cw) for s in range(nseq)], axis=0))
        for s in range(nseq):
            cout_ref[s, :, p * GROUP_W:(p + 1) * GROUP_W] = xbuf[p, s * 16 + 5 + ls:s * 16 + 8 + ls, :]

    blk, = _gdn_blocks([(qkv, ab_ref[...])], alog_ref[...], dt_ref[...], ls)
    seq_of_row = _div(_iota((2 * CHUNK, 1), 0) & (CHUNK - 1), ls)
    vnews, qss = [], []
    for h in range(N_HEADS):
        wq = jnp.concatenate([blk["w"][h], blk["qg"][h]], axis=0)
        r = jnp.zeros((2 * CHUNK, HEAD_W), F32)
        for s in range(nseq):
            r = r + _dot(jnp.where(seq_of_row == s, wq, 0.0).astype(BF16), s0_ref[s, h].astype(BF16))
        vn = blk["u"][h] - r[0:CHUNK]
        vnews.append(vn)
        qss.append(r[CHUNK:2 * CHUNK])
        vn_pad = _pad_rows(vn).astype(BF16)
        for s in range(nseq):
            kd = jnp.where(seq_of_row[0:CHUNK] == s, blk["kdec"][h], 0.0)
            decay = jnp.exp(blk["gl"][h][s * ls:s * ls + 1, :])
            sout_ref[s, h] = s0_ref[s, h] * decay + _dot(_pad_t(kd).astype(BF16), vn_pad)
    o_ref[...] = _gdn_outputs(blk, vnews, qss, bg_ref[...], dnw_ref[...]).astype(o_ref.dtype)


def _gdn_sample(p2, g2, ls, conv_w, alog_row, dt_row, dnw, conv0, s0, layer, s_all):
    bs = s0.shape[1]
    nseq = CHUNK // ls
    seg = lambda s: pl.BlockSpec((CHUNK, GROUP_W), lambda i: (i, s))
    const = lambda shape: pl.BlockSpec(shape, lambda i: (0,) * len(shape))
    n_in = 11
    extra_in, extra_specs, aliases = (), [], {}
    if s_all is not None:
        extra_in, extra_specs, aliases = (s_all,), [pl.BlockSpec(memory_space=pl.ANY)], {n_in: 1}
    return pl.pallas_call(
        functools.partial(_gdns_kernel, ls=ls, nseq=nseq),
        grid=(bs // nseq,),
        in_specs=[seg(SEG_BQ), seg(SEG_BK), seg(SEG_BV), seg(SEG_BG),
                  pl.BlockSpec((CHUNK, LANES), lambda i: (i, 0)),
                  const((CONV_W, 3 * GROUP_W)), const((1, LANES)), const((1, LANES)), const((1, HEAD_W)),
                  pl.BlockSpec((None, nseq, CONV_W - 1, 3 * GROUP_W), lambda i: (layer, i, 0, 0)),
                  pl.BlockSpec((None, nseq, N_HEADS, HEAD_W, HEAD_W), lambda i: (layer, i, 0, 0, 0))] + extra_specs,
        out_specs=[
            pl.BlockSpec((CHUNK, GROUP_W), lambda i: (i, 0)),
            pl.BlockSpec((None, nseq, N_HEADS, HEAD_W, HEAD_W), lambda i: (layer, i, 0, 0, 0)),
            pl.BlockSpec((nseq, CONV_W - 1, 3 * GROUP_W), lambda i: (i, 0, 0)),
        ],
        out_shape=[
            jax.ShapeDtypeStruct((bs * ls, GROUP_W), F32),
            jax.ShapeDtypeStruct(s0.shape, F32),
            jax.ShapeDtypeStruct(conv0.shape[1:], F32),
        ],
        scratch_shapes=[pltpu.VMEM((3, nseq * 16, GROUP_W), F32)],
        input_output_aliases=aliases,
        compiler_params=_cparams(("parallel",)),
        name="gdn_sample",
    )(p2, p2, p2, p2, g2, conv_w, alog_row, dt_row, dnw.reshape(1, HEAD_W), conv0, s0, *extra_in)


def _outproj_kernel(h_ref, oa_ref, ob_ref, ox_ref, w_ref, nf_ref, o_ref, *, final):
    acc = h_ref[...]
    for n, r in enumerate((oa_ref, ob_ref, ox_ref)):
        acc = acc + _dot(r[...].astype(BF16), w_ref[n * GROUP_W:(n + 1) * GROUP_W, :])
    o_ref[...] = _rms(acc, nf_ref[...]) if final else acc


def _outproj(h2d, oa, ob, ox, w_out, norm_f, final):
    t = h2d.shape[0]
    tm = min(t, 1024)
    row = lambda w: pl.BlockSpec((tm, w), lambda i: (i, 0))
    return pl.pallas_call(
        functools.partial(_outproj_kernel, final=final),
        grid=(t // tm,),
        in_specs=[row(D_MODEL), row(GROUP_W), row(GROUP_W), row(GROUP_W),
                  pl.BlockSpec((3 * GROUP_W, D_MODEL), lambda i: (0, 0)),
                  pl.BlockSpec((1, D_MODEL), lambda i: (0, 0))],
        out_specs=row(D_MODEL),
        out_shape=jax.ShapeDtypeStruct((t, D_MODEL), F32),
        compiler_params=_cparams(("parallel",)),
        name="outproj",
    )(h2d, oa, ob, ox, w_out, norm_f.reshape(1, D_MODEL))


def _pad_lanes(v):
    return jnp.pad(v.astype(F32), (0, LANES - v.shape[0])).reshape(1, LANES)


def kernel(x_prompt, x_sample, cache_k, cache_v, state_delta, state_conv, cache_mem_k, cache_mem_v, page_table, mem_prompt, norm_w, w_in, diff_lambda, diff_norm_w, conv_w, a_log, dt_bias, delta_norm_w, norm_mem, w_mem_kv, w_out, norm_f):
    bp, lp, _ = x_prompt.shape
    bs, ls, _ = x_sample.shape
    depth = w_in.shape[0]
    n_mem = mem_prompt.shape[1]
    n_gate = 2 * N_HEADS
    main_w = N_MAIN - 2 * GROUP_W

    w_main = jnp.concatenate([w_in[:, :, :main_w], w_in[:, :, main_w + n_gate:]], axis=2).astype(BF16)
    w_gate = jnp.pad(w_in[:, :, main_w:main_w + n_gate], ((0, 0), (0, 0), (0, LANES - n_gate))).astype(BF16)
    w_out_b = w_out.astype(BF16)
    cache_k4 = cache_k.reshape(cache_k.shape[0], cache_k.shape[1], PAGE * N_HEADS, HEAD_W)
    cache_v4 = cache_v.reshape(cache_v.shape[0], cache_v.shape[1], PAGE * N_HEADS, HEAD_W)
    mem_k4 = cache_mem_k.reshape(depth, bs, n_mem * N_HEADS, HEAD_W)
    mem_v4 = cache_mem_v.reshape(depth, bs, n_mem * N_HEADS, HEAD_W)

    mk_all, mv_all = _memkv(mem_prompt.reshape(bp * n_mem, D_MODEL), norm_mem, w_mem_kv.astype(BF16))

    hp = x_prompt.reshape(bp * lp, D_MODEL)
    hs = x_sample.reshape(bs * ls, D_MODEL)
    tq_x = min(lp, 512)
    sp, cp, cs = ([] for _ in range(3))
    kv_p = kv_s = s_all = None
    for l in range(depth):
        lam_init = 0.8 - 0.6 * math.exp(-0.3 * l)
        alog_row = _pad_lanes(a_log[l])
        dt_row = _pad_lanes(dt_bias[l])
        final = l == depth - 1

        pp, gp, pa16, *kv_p = _proj(hp, norm_w[l], w_main[l], w_gate[l], l, depth, kv_p)
        ps, gs, sa16, *kv_s = _proj(hs, norm_w[l], w_main[l], w_gate[l], l, depth, kv_s)
        oa, oa_s = _attn(pa16.reshape(bp, lp, N_A), sa16.reshape(bs, ls, N_A), cache_k4, cache_v4, l, page_table,
                         diff_lambda[l], diff_norm_w[l], lam_init)

        pp3 = pp.reshape(bp, lp, N_P)
        ob, s_new, c_new = _gdn_prompt(pp3, gp.reshape(bp, lp, LANES), conv_w[l], alog_row, dt_row, delta_norm_w[l])
        ox = _cross_prompt(pp.reshape(bp, lp // tq_x, tq_x, N_P), mk_all[l].reshape(bp, n_mem, GROUP_W),
                           mv_all[l].reshape(bp, n_mem, GROUP_W))
        hp = _outproj(hp, oa.reshape(bp * lp, GROUP_W), ob.reshape(bp * lp, GROUP_W),
                      ox.reshape(bp * lp, GROUP_W), w_out_b[l], norm_f, final)
        sp.append(s_new)
        cp.append(c_new)

        ps3 = ps.reshape(bs, ls, N_P)
        ob, s_all, c_new = _gdn_sample(ps, gs, ls, conv_w[l], alog_row, dt_row, delta_norm_w[l],
                                       state_conv, state_delta, l, s_all)
        ox = _cross_sample(ps3, mem_k4, mem_v4, l)
        hs = _outproj(hs, oa_s.reshape(bs * ls, GROUP_W), ob, ox.reshape(bs * ls, GROUP_W), w_out_b[l], norm_f, final)
        cs.append(c_new)

    heads = lambda x, b, n: x.reshape(depth, b, n, N_HEADS, HEAD_W)
    return (hp.reshape(bp, lp, D_MODEL), hs.reshape(bs, ls, D_MODEL),
            heads(kv_p[0], bp, lp), heads(kv_p[1], bp, lp), jnp.stack(sp), jnp.stack(cp),
            heads(mk_all, bp, n_mem), heads(mv_all, bp, n_mem),
            heads(kv_s[0], bs, ls), heads(kv_s[1], bs, ls), s_all, jnp.stack(cs))
```

```python
import functools
import math

import jax
import jax.numpy as jnp
import numpy as np
from jax import lax
from jax.experimental import pallas as pl
from jax.experimental.pallas import tpu as pltpu

F32 = jnp.float32
BF16 = jnp.bfloat16

D_MODEL = 1024
N_HEADS = 4
HEAD_W = 128
GROUP_W = N_HEADS * HEAD_W
A_DH = 64
CONV_W = 4
CHUNK = 64
GDN_GROUP = 4
PAGE = 128
EPS = 1e-6
NEG_INF = -1e30
LOG2E = math.log2(math.e)
N_MAIN = 10 * GROUP_W
N_A = 4 * GROUP_W
N_P = N_MAIN - N_A
LANES = 128
VMEM_LIMIT = 48 * 1024 * 1024

SEG_AQ, SEG_AK, SEG_AV, SEG_AG = range(4)
SEG_BQ, SEG_BK, SEG_BV, SEG_BG, SEG_XQ, SEG_XG = range(6)


def _cparams(sem):
    return pltpu.CompilerParams(dimension_semantics=sem, vmem_limit_bytes=VMEM_LIMIT)


def _iota(shape, dim):
    return lax.broadcasted_iota(jnp.int32, shape, dim)


def _div(x, d):
    assert d & (d - 1) == 0
    return x >> (d.bit_length() - 1)


def _sigmoid(x):
    return 1.0 / (1.0 + jnp.exp(-x))


def _silu(x):
    return x * _sigmoid(x)


def _softplus(x):
    return jnp.maximum(x, 0.0) + jnp.log(1.0 + jnp.exp(-jnp.abs(x)))


def _dot(a, b):
    return jnp.dot(a, b, preferred_element_type=F32)


def _dot_nt(a, b):
    return lax.dot_general(a, b, (((1,), (1,)), ((), ())), preferred_element_type=F32)


def _rms(x, w):
    ms = jnp.mean(x * x, axis=-1, keepdims=True)
    return x * lax.rsqrt(ms + EPS) * w


def _lambda_value(p, lam_init):
    a = jnp.sum(p[0:1, :] * p[1:2, :], axis=-1, keepdims=True)
    b = jnp.sum(p[2:3, :] * p[3:4, :], axis=-1, keepdims=True)
    return jnp.exp(a) - jnp.exp(b) + lam_init


def _proj_kernel(x_ref, nw_ref, w_ref, wx_ref, wg_ref, *rest, tm):
    p_ref, g_ref, a16_ref, k_out, v_out = rest[-5:]
    hn = _rms(x_ref[...], nw_ref[...]).astype(BF16)
    g_ref[...] = _dot(hn, wg_ref[...])

    def scatter_heads(out_ref, cols):
        for h in range(N_HEADS):
            out_ref[pl.ds(h, tm, stride=N_HEADS), :] = cols[:, h * HEAD_W:(h + 1) * HEAD_W]

    n_ab = w_ref.shape[1] // GROUP_W
    for seg in range(N_MAIN // GROUP_W):
        w_seg = (w_ref[:, seg * GROUP_W:(seg + 1) * GROUP_W] if seg < n_ab
                 else wx_ref[:, (seg - n_ab) * GROUP_W:(seg - n_ab + 1) * GROUP_W])
        res = _dot(hn, w_seg)
        if seg < N_A // GROUP_W:
            a16_ref[:, seg * GROUP_W:(seg + 1) * GROUP_W] = res.astype(BF16)
            if seg == SEG_AK:
                scatter_heads(k_out, res)
            if seg == SEG_AV:
                scatter_heads(v_out, res)
        else:
            p_ref[:, seg * GROUP_W - N_A:(seg + 1) * GROUP_W - N_A] = res


def _proj(x2d, norm_w, w_ab, w_x, w_gate, layer, depth, kv_all):
    t = x2d.shape[0]
    tm = min(t, 512)
    kv_spec = pl.BlockSpec((None, N_HEADS * tm, HEAD_W), lambda i: (layer, i, 0))
    kv_shape = jax.ShapeDtypeStruct((depth, N_HEADS * t, HEAD_W), F32)
    n_in = 5
    extra_in, extra_specs, aliases = (), [], {}
    if kv_all is not None:
        extra_in = tuple(kv_all)
        extra_specs = [pl.BlockSpec(memory_space=pl.ANY)] * 2
        aliases = {n_in: 3, n_in + 1: 4}
    resident = dict(pipeline_mode=pl.Buffered(1))
    return pl.pallas_call(
        functools.partial(_proj_kernel, tm=tm),
        grid=(t // tm,),
        in_specs=[
            pl.BlockSpec((tm, D_MODEL), lambda i: (i, 0)),
            pl.BlockSpec((1, D_MODEL), lambda i: (0, 0)),
            pl.BlockSpec((D_MODEL, w_ab.shape[1]), lambda i: (0, 0), **resident),
            pl.BlockSpec((D_MODEL, w_x.shape[1]), lambda i: (0, 0), **resident),
            pl.BlockSpec((D_MODEL, LANES), lambda i: (0, 0), **resident),
        ] + extra_specs,
        out_specs=[
            pl.BlockSpec((tm, N_P), lambda i: (i, 0)),
            pl.BlockSpec((tm, LANES), lambda i: (i, 0)),
            pl.BlockSpec((tm, N_A), lambda i: (i, 0)),
            kv_spec, kv_spec,
        ],
        out_shape=[jax.ShapeDtypeStruct((t, N_P), F32), jax.ShapeDtypeStruct((t, LANES), F32),
                   jax.ShapeDtypeStruct((t, N_A), BF16), kv_shape, kv_shape],
        input_output_aliases=aliases,
        compiler_params=_cparams(("parallel",)),
        name="proj",
    )(x2d, norm_w.reshape(1, D_MODEL), w_ab, w_x, w_gate, *extra_in)


def _memkv_kernel(x_ref, nw_ref, w_ref, k_ref, v_ref):
    hn = _rms(x_ref[...], nw_ref[...]).astype(BF16)
    kv = _dot(hn, w_ref[...])
    k_ref[...] = kv[:, :GROUP_W]
    v_ref[...] = kv[:, GROUP_W:]


def _memkv(mem2d, norm_mem, w_mem_kv_bf16):
    depth = norm_mem.shape[0]
    t = mem2d.shape[0]
    return pl.pallas_call(
        _memkv_kernel,
        grid=(depth,),
        in_specs=[
            pl.BlockSpec((t, D_MODEL), lambda l: (0, 0)),
            pl.BlockSpec((None, 1, D_MODEL), lambda l: (l, 0, 0)),
            pl.BlockSpec((None, D_MODEL, 2 * GROUP_W), lambda l: (l, 0, 0)),
        ],
        out_specs=[
            pl.BlockSpec((None, t, GROUP_W), lambda l: (l, 0, 0)),
            pl.BlockSpec((None, t, GROUP_W), lambda l: (l, 0, 0)),
        ],
        out_shape=[jax.ShapeDtypeStruct((depth, t, GROUP_W), F32)] * 2,
        compiler_params=_cparams(("parallel",)),
        name="memkv",
    )(mem2d, norm_mem.reshape(depth, 1, D_MODEL), w_mem_kv_bf16)


def _alibi_slope(h):
    return jnp.where(h == 0, 0.25, jnp.where(h == 1, 0.0625, jnp.where(h == 2, 0.015625, 0.00390625))).astype(F32)


def _diff_epilogue(o1, o2, lam, dnw, lam_scale, gate):
    o = o1 - lam * o2
    return _rms(o, dnw) * lam_scale * _silu(gate)


def _bf16_split3(x):
    parts = []
    for _ in range(3):
        p = float(np.asarray(x, np.float32).astype(jnp.bfloat16).astype(np.float32))
        parts.append(p)
        x = x - p
    return parts


_ALIBI_PARTS = [_bf16_split3(2.0 ** (-2.0 * (h + 1)) * LOG2E) for h in range(N_HEADS)]
POS_LO = 128
LOOP_TILES = 4


def _by_head(h, values):
    out = jnp.float32(values[-1])
    for idx in range(len(values) - 2, -1, -1):
        out = jnp.where(h == idx, jnp.float32(values[idx]), out)
    return out


def _sample_attention(x_ref, k_refs, v_refs, lam, dnw, lam_scale, ls):
    n_pages = len(k_refs)
    past = n_pages * PAGE
    rows = 2 * ls * N_HEADS
    x = x_ref[...].astype(F32)
    q = x[:, 0:GROUP_W] * (A_DH ** -0.5 * LOG2E)
    k_new = x[:, GROUP_W:2 * GROUP_W]
    v_new = x[:, 2 * GROUP_W:3 * GROUP_W]
    gate = x[:, 3 * GROUP_W:4 * GROUP_W]

    lane_q = _iota((ls, HEAD_W), 1)
    parts = []
    for h in range(N_HEADS):
        qh = q[:, h * HEAD_W:(h + 1) * HEAD_W]
        parts += [jnp.where(lane_q < A_DH, qh, 0.0), jnp.where(lane_q >= A_DH, qh, 0.0)]
    qs = jnp.concatenate(parts, axis=0)
    row_head_w = _div(_iota((rows, GROUP_W), 0), 2 * ls)
    qb = jnp.where(row_head_w == _div(_iota((rows, GROUP_W), 1), HEAD_W),
                   jnp.concatenate([qs] * N_HEADS, axis=1), 0.0).astype(BF16)
    qs = qs.astype(BF16)

    row_head = _div(_iota((rows, 1), 0), 2 * ls)
    slope2 = _alibi_slope(row_head) * LOG2E
    tok = _iota((rows, 1), 0) & (ls - 1)

    col = _iota((1, N_HEADS * PAGE), 1)
    own = (col & (N_HEADS - 1)) == row_head
    key_in_page = _div(col, N_HEADS)
    s_pages = []
    for p in range(n_pages):
        s = _dot_nt(qs, k_refs[p][...].astype(BF16))
        s = s + slope2 * (key_in_page + (p * PAGE - past)).astype(F32)
        s_pages.append(jnp.where(own, s, NEG_INF))
    lane = _iota((1, PAGE), 1)
    pad = jnp.zeros((PAGE - ls, GROUP_W), F32)
    k_pad = jnp.concatenate([k_new, pad], axis=0).astype(BF16)
    v_pad = jnp.concatenate([v_new, pad], axis=0).astype(BF16)
    s_new = _dot_nt(qb, k_pad) + slope2 * lane.astype(F32)
    s_new = jnp.where(lane <= tok, s_new, NEG_INF)

    m = s_new.max(axis=-1, keepdims=True)
    for s in s_pages:
        m = jnp.maximum(m, s.max(axis=-1, keepdims=True))
    e = jnp.exp2(s_new - m)
    l = jnp.sum(e, axis=-1, keepdims=True)
    o_new = _dot(e.astype(BF16), v_pad)
    acc = jnp.zeros((rows, HEAD_W), F32)
    for h in range(N_HEADS):
        acc = acc + jnp.where(row_head == h, o_new[:, h * HEAD_W:(h + 1) * HEAD_W], 0.0)
    for p, s in enumerate(s_pages):
        e = jnp.exp2(s - m)
        l = l + jnp.sum(e, axis=-1, keepdims=True)
        acc = acc + _dot(e.astype(BF16), v_refs[p][...].astype(BF16))
    o = acc / l

    outs = []
    for h in range(N_HEADS):
        o1 = o[2 * ls * h:2 * ls * h + ls]
        o2 = o[2 * ls * h + ls:2 * ls * (h + 1)]
        outs.append(_diff_epilogue(o1, o2, lam, dnw, lam_scale, gate[:, h * HEAD_W:(h + 1) * HEAD_W]))
    return jnp.concatenate(outs, axis=1)


def _attn_kernel(pt_ref, q_ref, k_ref, v_ref, g_ref, lamp_ref, dnw_ref, xs_ref, *rest, tq, n_pages, ls, lam_init):
    del pt_ref
    k_pages = rest[:n_pages]
    v_pages = rest[n_pages:2 * n_pages]
    o_ref, os_ref, kf_ref, qs_ref, s_a, s_b, m_ref, l_ref, acc_ref = rest[2 * n_pages:]
    h = pl.program_id(1)
    i = pl.program_id(2)
    tk = tq
    n_lane_tiles = tk // LANES
    seq = k_ref.shape[0]
    lam = _lambda_value(lamp_ref[...], lam_init)

    @pl.when(i == 0)
    def _():
        def fill(r, carry):
            rows = pl.ds(pl.multiple_of(r * tk, tk), tk)
            k = k_ref[rows, :].astype(F32)
            lane = _iota((tk, HEAD_W), 1)
            pos = r * tk + _iota((tk, HEAD_W), 0)
            hi = _div(pos, POS_LO).astype(F32)
            lo = (pos & (POS_LO - 1)).astype(F32)
            for mp in range(2):
                rel = lane - (A_DH if mp == 0 else 0)
                feat = jnp.where(rel < 3, hi, jnp.where(rel < 6, lo, 0.0))
                own = (lane < A_DH) if mp == 0 else (lane >= A_DH)
                kf_ref[mp, rows, :] = jnp.where(own, k, feat).astype(BF16)
            return carry

        lax.fori_loop(0, seq // tk, fill, 0)

    q = q_ref[...].astype(F32) * (A_DH ** -0.5 * LOG2E)
    lane = _iota((1, HEAD_W), 1)
    c = [_by_head(h, [_ALIBI_PARTS[hh][part] for hh in range(N_HEADS)]) for part in range(3)]
    for mp in range(2):
        rel = lane - (A_DH if mp == 0 else 0)
        qfeat = jnp.zeros((1, HEAD_W), F32)
        for part in range(3):
            qfeat = jnp.where(rel == part, c[part] * POS_LO, jnp.where(rel == 3 + part, c[part], qfeat))
        own = (lane < A_DH) if mp == 0 else (lane >= A_DH)
        qs_ref[mp] = jnp.where(own, q, qfeat).astype(BF16)
    m_ref[...] = jnp.full(m_ref.shape, NEG_INF, F32)
    l_ref[...] = jnp.zeros(l_ref.shape, F32)
    acc_ref[...] = jnp.zeros(acc_ref.shape, F32)

    def scores(mp, j):
        return _dot_nt(qs_ref[mp], kf_ref[mp, pl.ds(pl.multiple_of(j * tk, tk), tk), :])

    def update(mp, s_ref, j, diagonal):
        def tile(cidx):
            s = s_ref[:, cidx * LANES:(cidx + 1) * LANES]
            if diagonal:
                s = jnp.where(cidx * LANES + _iota((1, LANES), 1) <= _iota((tq, 1), 0), s, NEG_INF)
            return s

        mx = tile(0)
        for cidx in range(1, n_lane_tiles):
            mx = jnp.maximum(mx, tile(cidx))
        m_prev = m_ref[mp]
        m_new = jnp.maximum(m_prev, jnp.max(mx, axis=-1, keepdims=True))
        alpha = jnp.exp2(m_prev - m_new)
        lsum = alpha * l_ref[mp]
        ps = []
        for cidx in range(n_lane_tiles):
            p = jnp.exp2(tile(cidx) - m_new)
            lsum = lsum + p
            ps.append(p.astype(BF16))
        v = v_ref[pl.ds(pl.multiple_of(j * tk, tk), tk), :]
        acc_ref[mp] = alpha * acc_ref[mp] + _dot(jnp.concatenate(ps, axis=1), v)
        l_ref[mp] = lsum
        m_ref[mp] = m_new

    s_a[...] = scores(0, 0)
    os_ref[...] = _sample_attention(xs_ref, k_pages, v_pages, lam, dnw_ref[...], 1.0 - lam_init, ls)

    def step(j):
        s_b[...] = scores(1, j)
        update(0, s_a, j, False)
        s_a[...] = scores(0, j + 1)
        update(1, s_b, j, False)

    def quad(jj, carry):
        for u in range(LOOP_TILES):
            step(LOOP_TILES * jj + u)
        return carry

    lax.fori_loop(0, _div(i, LOOP_TILES), quad, 0)
    rem = i & (LOOP_TILES - 1)

    @pl.when(rem >= 2)
    def _():
        step(i - rem)
        step(i - rem + 1)

    @pl.when((rem & 1) == 1)
    def _():
        step(i - 1)

    s_b[...] = scores(1, i)
    update(0, s_a, i, True)
    update(1, s_b, i, True)

    o1 = acc_ref[0] / jnp.sum(l_ref[0], axis=-1, keepdims=True)
    o2 = acc_ref[1] / jnp.sum(l_ref[1], axis=-1, keepdims=True)
    y = _diff_epilogue(o1, o2, lam, dnw_ref[...], 1.0 - lam_init, g_ref[...].astype(F32))
    o_ref[...] = y.astype(o_ref.dtype)


def _attn(a16, xs16, cache_k4, cache_v4, layer, page_table, lam_p, dnw, lam_init):
    b, l, _ = a16.shape
    bs, ls, _ = xs16.shape
    n_pages = page_table.shape[1]
    tq = min(l, 512)
    nq = l // tq
    assert l <= POS_LO * 256 and tq % LANES == 0
    assert b * N_HEADS * nq == bs, "one sample sequence per prompt grid step"

    def step_id(bi, h, i):
        return (bi * N_HEADS + h) * nq + i

    def page_map(p):
        return lambda bi, h, i, pt: (layer, pt[step_id(bi, h, i) * n_pages + p], 0, 0)

    page_specs = lambda: [pl.BlockSpec((None, None, N_HEADS * PAGE, HEAD_W), page_map(p)) for p in range(n_pages)]
    grid_spec = pltpu.PrefetchScalarGridSpec(
        num_scalar_prefetch=1,
        grid=(b, N_HEADS, nq),
        in_specs=[
            pl.BlockSpec((None, tq, HEAD_W), lambda bi, h, i, pt: (bi, i, SEG_AQ * N_HEADS + h)),
            pl.BlockSpec((None, l, HEAD_W), lambda bi, h, i, pt: (bi, 0, SEG_AK * N_HEADS + h)),
            pl.BlockSpec((None, l, HEAD_W), lambda bi, h, i, pt: (bi, 0, SEG_AV * N_HEADS + h)),
            pl.BlockSpec((None, tq, HEAD_W), lambda bi, h, i, pt: (bi, i, SEG_AG * N_HEADS + h)),
            pl.BlockSpec((4, A_DH), lambda bi, h, i, pt: (0, 0)),
            pl.BlockSpec((1, HEAD_W), lambda bi, h, i, pt: (0, 0)),
            pl.BlockSpec((None, ls, N_A), lambda bi, h, i, pt: (step_id(bi, h, i), 0, 0)),
        ] + page_specs() + page_specs(),
        out_specs=[
            pl.BlockSpec((None, tq, HEAD_W), lambda bi, h, i, pt: (bi, i, h)),
            pl.BlockSpec((None, ls, GROUP_W), lambda bi, h, i, pt: (step_id(bi, h, i), 0, 0)),
        ],
        scratch_shapes=[
            pltpu.VMEM((2, l, HEAD_W), BF16),
            pltpu.VMEM((2, tq, HEAD_W), BF16),
            pltpu.VMEM((tq, tq), F32),
            pltpu.VMEM((tq, tq), F32),
            pltpu.VMEM((2, tq, LANES), F32),
            pltpu.VMEM((2, tq, LANES), F32),
            pltpu.VMEM((2, tq, HEAD_W), F32),
        ],
    )
    return pl.pallas_call(
        functools.partial(_attn_kernel, tq=tq, n_pages=n_pages, ls=ls, lam_init=lam_init),
        grid_spec=grid_spec,
        out_shape=[jax.ShapeDtypeStruct((b, l, GROUP_W), BF16), jax.ShapeDtypeStruct((bs, ls, GROUP_W), F32)],
        compiler_params=_cparams(("parallel", "parallel", "arbitrary")),
        name="attn",
    )(page_table.reshape(-1), a16, a16, a16, a16, lam_p, dnw.reshape(1, HEAD_W), xs16,
      *([cache_k4] * n_pages), *([cache_v4] * n_pages))


def _cross_kernel(q_ref, g_ref, mk_ref, mv_ref, o_ref, *, nseq):
    for s in range(nseq):
        q = q_ref[s] * (HEAD_W ** -0.5)
        gate = g_ref[s]
        mk = mk_ref[s].astype(BF16)
        mv = mv_ref[s].astype(BF16)
        outs = []
        for h in range(N_HEADS):
            sl = slice(h * HEAD_W, (h + 1) * HEAD_W)
            sc = _dot_nt(q[:, sl].astype(BF16), mk[:, sl])
            e = jnp.exp(sc - sc.max(axis=-1, keepdims=True))
            o = _dot(e.astype(BF16), mv[:, sl]) / jnp.sum(e, axis=-1, keepdims=True)
            outs.append(o * _silu(gate[:, sl]))
        o_ref[s] = jnp.concatenate(outs, axis=1).astype(o_ref.dtype)


def _cross_prompt(p4, mk, mv):
    nb, nt, tq, _ = p4.shape
    n_mem = mk.shape[1]
    return pl.pallas_call(
        functools.partial(_cross_kernel, nseq=1),
        grid=(nb, nt),
        in_specs=[
            pl.BlockSpec((1, None, tq, GROUP_W), lambda b, i: (b, i, 0, SEG_XQ)),
            pl.BlockSpec((1, None, tq, GROUP_W), lambda b, i: (b, i, 0, SEG_XG)),
            pl.BlockSpec((1, n_mem, GROUP_W), lambda b, i: (b, 0, 0)),
            pl.BlockSpec((1, n_mem, GROUP_W), lambda b, i: (b, 0, 0)),
        ],
        out_specs=pl.BlockSpec((1, None, tq, GROUP_W), lambda b, i: (b, i, 0, 0)),
        out_shape=jax.ShapeDtypeStruct((nb, nt, tq, GROUP_W), BF16),
        compiler_params=_cparams(("parallel", "parallel")),
        name="cross_prompt",
    )(p4, p4, mk, mv)


def _cross_sample_kernel(q_ref, g_ref, mk_ref, mv_ref, o_ref, *, nseq, ls):
    rows = N_HEADS * ls
    n_rows = mk_ref.shape[1]
    own = (_iota((1, n_rows), 1) & (N_HEADS - 1)) == _div(_iota((rows, 1), 0), ls)
    for s in range(nseq):
        q = q_ref[s] * (HEAD_W ** -0.5)
        gate = g_ref[s]
        qx = jnp.concatenate([q[:, h * HEAD_W:(h + 1) * HEAD_W] for h in range(N_HEADS)], axis=0)
        sc = jnp.where(own, _dot_nt(qx.astype(BF16), mk_ref[s].astype(BF16)), NEG_INF)
        e = jnp.exp(sc - sc.max(axis=-1, keepdims=True))
        o = _dot(e.astype(BF16), mv_ref[s].astype(BF16)) / jnp.sum(e, axis=-1, keepdims=True)
        o = jnp.concatenate([o[h * ls:(h + 1) * ls] for h in range(N_HEADS)], axis=1)
        o_ref[s] = o * _silu(gate)


def _cross_sample(p3, mk, mv, layer):
    bs, ls, _ = p3.shape
    n_rows = mk.shape[2]
    nseq = 8
    return pl.pallas_call(
        functools.partial(_cross_sample_kernel, nseq=nseq, ls=ls),
        grid=(bs // nseq,),
        in_specs=[
            pl.BlockSpec((nseq, ls, GROUP_W), lambda b: (b, 0, SEG_XQ)),
            pl.BlockSpec((nseq, ls, GROUP_W), lambda b: (b, 0, SEG_XG)),
            pl.BlockSpec((None, nseq, n_rows, HEAD_W), lambda b: (layer, b, 0, 0)),
            pl.BlockSpec((None, nseq, n_rows, HEAD_W), lambda b: (layer, b, 0, 0)),
        ],
        out_specs=pl.BlockSpec((nseq, ls, GROUP_W), lambda b: (b, 0, 0)),
        out_shape=jax.ShapeDtypeStruct((bs, ls, GROUP_W), F32),
        compiler_params=_cparams(("parallel",)),
        name="cross_sample",
    )(p3, p3, mk, mv)


def _col(x, lane_idx):
    lane = _iota(x.shape, 1)
    return jnp.broadcast_to(jnp.sum(jnp.where(lane == lane_idx, x, 0.0), axis=-1, keepdims=True), x.shape)


def _l2_normalize(x, scale):
    ss = jnp.broadcast_to(jnp.sum(x * x, axis=-1, keepdims=True), x.shape)
    return x * (lax.rsqrt(ss + EPS) * scale)


def _split3_dot(mat_bf16, x):
    hi = x.astype(BF16)
    r1 = x - hi.astype(F32)
    mid = r1.astype(BF16)
    lo = (r1 - mid.astype(F32)).astype(BF16)
    return _dot(mat_bf16, hi) + _dot(mat_bf16, mid) + _dot(mat_bf16, lo)


def _run_interleaved(*gens):
    results = [None] * len(gens)
    live = list(enumerate(gens))
    while live:
        still = []
        for idx, g in live:
            try:
                next(g)
                still.append((idx, g))
            except StopIteration as done:
                results[idx] = done.value
        live = still
    return results


def _gdn_blocks(blocks, alog_row, dt_row, blk):
    return _run_interleaved(_gdn_blocks_staged(blocks, alog_row, dt_row, blk))[0]


def _gdn_blocks_staged(blocks, alog_row, dt_row, blk):
    sh = int(math.log2(blk))
    n_pairs = N_HEADS // 2
    lane = _iota((1, LANES), 1)
    neg_a = jnp.where(lane < N_HEADS, -jnp.exp(alog_row), 0.0)
    ti = _iota((CHUNK, CHUNK), 0)
    tj = _iota((CHUNK, CHUNK), 1)
    same_t = (ti >> sh) == (tj >> sh)
    tri_incl = (same_t & (tj <= ti)).astype(BF16)
    tri_all = same_t.astype(BF16)
    pi = _iota((CHUNK, LANES), 0)
    pj = _iota((CHUNK, LANES), 1) & (CHUNK - 1)
    left = _iota((CHUNK, LANES), 1) < CHUNK
    same = (pi >> sh) == (pj >> sh)
    incl = same & (pj <= pi)
    strict = same & (pj < pi)
    eye = jnp.where(pi == pj, 1.0, 0.0)
    zeros_h = jnp.zeros((CHUNK, HEAD_W), F32)

    def block_diag(m):
        return jnp.concatenate([jnp.where(left, m, 0.0), jnp.where(left, 0.0, m)], axis=0)

    st = []
    for qkv, ab in blocks:
        hsl = [slice(h * HEAD_W, (h + 1) * HEAD_W) for h in range(N_HEADS)]
        q = [qkv[0][:, sl] for sl in hsl]
        k = [qkv[1][:, sl] for sl in hsl]
        v = [qkv[2][:, sl] for sl in hsl]
        qn = [_l2_normalize(x, HEAD_W ** -0.5) for x in q]
        kn = [_l2_normalize(x, 1.0) for x in k]
        g = neg_a * _softplus(ab + dt_row)
        beta_full = _sigmoid(ab)
        gcum = _split3_dot(tri_incl, g)
        gtot = _split3_dot(tri_all, g)
        gc = [_col(gcum, h) for h in range(N_HEADS)]
        gl = [_col(gtot, h) for h in range(N_HEADS)]
        beta = [_col(beta_full, N_HEADS + h) for h in range(N_HEADS)]
        gt = jnp.concatenate([gcum, gcum], axis=0).T
        kb = [kn[h] * beta[h] for h in range(N_HEADS)]
        eg = [jnp.exp(gc[h]) for h in range(N_HEADS)]
        blk_state = dict(gl=gl, qg=[qn[h] * eg[h] for h in range(N_HEADS)],
                         kdec=[kn[h] * jnp.exp(gl[h] - gc[h]) for h in range(N_HEADS)],
                         decay=[], lhs=[], rnt=[], rhs=[])
        for p in range(n_pairs):
            a, b = 2 * p, 2 * p + 1
            gr = jnp.where(lane < CHUNK, gt[a:a + 1, :], gt[b:b + 1, :])
            gcp = jnp.where(left, gc[a], gc[b])
            blk_state["decay"].append(jnp.where(incl, jnp.exp(jnp.where(incl, gcp - gr, 0.0)), 0.0))
            blk_state["lhs"].append(jnp.concatenate(
                [jnp.concatenate([kb[a], kb[b]], axis=1), jnp.concatenate([qn[a], qn[b]], axis=1)],
                axis=0).astype(BF16))
            blk_state["rnt"].append(jnp.concatenate(
                [jnp.concatenate([kn[a], zeros_h], axis=1), jnp.concatenate([zeros_h, kn[b]], axis=1)],
                axis=0).astype(BF16))
            blk_state["rhs"].append(jnp.concatenate(
                [jnp.concatenate([v[a] * beta[a], kb[a] * eg[a]], axis=1),
                 jnp.concatenate([v[b] * beta[b], kb[b] * eg[b]], axis=1)], axis=0).astype(BF16))
        st.append(blk_state)
        yield

    for s in st:
        s["nmat"], s["qkm"] = [], []
        for p in range(n_pairs):
            kq = _dot_nt(s["lhs"][p], s["rnt"][p])
            s["nmat"].append(jnp.where(strict, kq[0:CHUNK] * s["decay"][p], 0.0))
            s["qkm"].append(kq[CHUNK:2 * CHUNK] * s["decay"][p])
    yield

    first = ((pi >> 1) == (pj >> 1)) & ((pi & 1) == 1) & ((pj & 1) == 0)
    for s in st:
        s["x"] = [eye - jnp.where(first, nm, 0.0) for nm in s["nmat"]]
    b = 2
    while b < blk:
        sb = int(math.log2(2 * b))
        mask = ((pi >> sb) == (pj >> sb)) & ((pi & (2 * b - 1)) >= b) & ((pj & (2 * b - 1)) < b)
        for s in st:
            s["xbd"] = [block_diag(x).astype(BF16) for x in s["x"]]
            s["t"] = [_dot(s["x"][p].astype(BF16), block_diag(jnp.where(mask, s["nmat"][p], 0.0)).astype(BF16))
                      for p in range(n_pairs)]
        yield
        for s in st:
            s["x"] = [s["x"][p] - _dot(s["t"][p].astype(BF16), s["xbd"][p]) for p in range(n_pairs)]
        yield
        b *= 2

    outs = []
    for s in st:
        u, w = [], []
        for p in range(n_pairs):
            sol = _dot(block_diag(s["x"][p]).astype(BF16), s["rhs"][p])
            for half in range(2):
                rows = slice(half * CHUNK, (half + 1) * CHUNK)
                u.append(sol[rows, 0:HEAD_W])
                w.append(sol[rows, HEAD_W:2 * HEAD_W])
        outs.append(dict(u=u, w=w, qg=s["qg"], kdec=s["kdec"], gl=s["gl"],
                         qkbd=[block_diag(m).astype(BF16) for m in s["qkm"]]))
    return outs


def _gdn_outputs(pre, vnew, qs, gate, dnw):
    outs = [None] * N_HEADS
    for p in range(N_HEADS // 2):
        a, b = 2 * p, 2 * p + 1
        o = jnp.concatenate([qs[a], qs[b]], axis=0) + _dot(
            pre["qkbd"][p], jnp.concatenate([vnew[a], vnew[b]], axis=0).astype(BF16))
        outs[a], outs[b] = o[0:CHUNK], o[CHUNK:2 * CHUNK]
    return jnp.concatenate([_rms(outs[h], dnw) * _silu(gate[:, h * HEAD_W:(h + 1) * HEAD_W])
                            for h in range(N_HEADS)], axis=1)


def _pad_t(a):
    return jnp.concatenate([a, jnp.zeros((LANES - CHUNK, LANES), F32)], axis=0).T


def _pad_rows(a):
    return jnp.concatenate([a, jnp.zeros((LANES - CHUNK, LANES), F32)], axis=0)


def _conv_taps(xbuf, p, base, nrows, cw):
    cur = xbuf[p, pl.ds(base + 8, nrows), :]
    row = _iota((nrows, GROUP_W), 0)
    acc = cur * cw[CONV_W - 1:CONV_W, p * GROUP_W:(p + 1) * GROUP_W]
    for k in range(1, CONV_W):
        head = xbuf[p, pl.ds(base + 8 - k, 8), :]
        if nrows > 8:
            head = jnp.concatenate([head, jnp.zeros((nrows - 8, GROUP_W), F32)], axis=0)
        shifted = jnp.where(row < k, head, pltpu.roll(cur, k, axis=0))
        acc = acc + shifted * cw[CONV_W - 1 - k:CONV_W - k, p * GROUP_W:(p + 1) * GROUP_W]
    return _silu(acc)


def _gdnp_kernel(bq_ref, bk_ref, bv_ref, bg_ref, ab_ref, cw_ref, alog_ref, dt_ref, dnw_ref,
                 o_ref, sout_ref, cout_ref, xbuf, ybuf, s_scr, *, tb):
    t = pl.program_id(1)

    @pl.when(t == 0)
    def _():
        xbuf[:, 0:8, :] = jnp.zeros((3, 8, GROUP_W), F32)
        s_scr[...] = jnp.zeros(s_scr.shape, F32)

    x_refs = (bq_ref, bk_ref, bv_ref)
    for p in range(3):
        xbuf[p, 8:8 + tb, :] = x_refs[p][...]
    cw = cw_ref[...]
    gate = bg_ref[...]
    ab = ab_ref[...]
    group_rows = GDN_GROUP * CHUNK
    n_groups = tb // group_rows

    def conv_staged(g):
        for p in range(3):
            ybuf[p, g * group_rows:(g + 1) * group_rows, :] = _conv_taps(xbuf, p, g * group_rows, group_rows, cw)
            yield

    def prepass_staged(g):
        rows = [slice(g * group_rows + c * CHUNK, g * group_rows + (c + 1) * CHUNK) for c in range(GDN_GROUP)]
        return _gdn_blocks_staged([([ybuf[p, rs, :] for p in range(3)], ab[rs]) for rs in rows],
                                  alog_ref[...], dt_ref[...], CHUNK)

    def sequential_staged(g, pre):
        for c in range(GDN_GROUP):
            rs = slice(g * group_rows + c * CHUNK, g * group_rows + (c + 1) * CHUNK)
            blk = pre[c]
            vnews, qss = [], []
            for h in range(N_HEADS):
                s_old = s_scr[h]
                wq = jnp.concatenate([blk["w"][h], blk["qg"][h]], axis=0).astype(BF16)
                r = _dot(wq, s_old.astype(BF16))
                vn = blk["u"][h] - r[0:CHUNK]
                vnews.append(vn)
                qss.append(r[CHUNK:2 * CHUNK])
                upd = _dot(_pad_t(blk["kdec"][h]).astype(BF16), _pad_rows(vn).astype(BF16))
                s_scr[h] = s_old * jnp.exp(blk["gl"][h][0:1, :]) + upd
            yield
            o_ref[rs, :] = _gdn_outputs(blk, vnews, qss, gate[rs], dnw_ref[...]).astype(o_ref.dtype)
            yield

    def nothing():
        return
        yield

    _run_interleaved(conv_staged(0))
    pre_prev = None
    for g in range(n_groups):
        conv_next = conv_staged(g + 1) if g + 1 < n_groups else nothing()
        seq_prev = sequential_staged(g - 1, pre_prev) if g > 0 else nothing()
        pre_prev = _run_interleaved(prepass_staged(g), conv_next, seq_prev)[0]
    _run_interleaved(sequential_staged(n_groups - 1, pre_prev))
    for p in range(3):
        xbuf[p, 5:8, :] = xbuf[p, tb + 5:tb + 8, :]

    @pl.when(t == pl.num_programs(1) - 1)
    def _():
        sout_ref[...] = s_scr[...]
        for p in range(3):
            cout_ref[:, p * GROUP_W:(p + 1) * GROUP_W] = xbuf[p, 5:8, :]


def _gdn_prompt(p3, g3, conv_w, alog_row, dt_row, dnw):
    b, l, _ = p3.shape
    tb = min(l, 1024)
    seg = lambda s: pl.BlockSpec((None, tb, GROUP_W), lambda bi, t: (bi, t, s))
    const = lambda shape: pl.BlockSpec(shape, lambda bi, t: (0,) * len(shape))
    return pl.pallas_call(
        functools.partial(_gdnp_kernel, tb=tb),
        grid=(b, l // tb),
        in_specs=[seg(SEG_BQ), seg(SEG_BK), seg(SEG_BV), seg(SEG_BG),
                  pl.BlockSpec((None, tb, LANES), lambda bi, t: (bi, t, 0)),
                  const((CONV_W, 3 * GROUP_W)), const((1, LANES)), const((1, LANES)), const((1, HEAD_W))],
        out_specs=[
            pl.BlockSpec((None, tb, GROUP_W), lambda bi, t: (bi, t, 0)),
            pl.BlockSpec((None, N_HEADS, HEAD_W, HEAD_W), lambda bi, t: (bi, 0, 0, 0)),
            pl.BlockSpec((None, CONV_W - 1, 3 * GROUP_W), lambda bi, t: (bi, 0, 0)),
        ],
        out_shape=[
            jax.ShapeDtypeStruct((b, l, GROUP_W), BF16),
            jax.ShapeDtypeStruct((b, N_HEADS, HEAD_W, HEAD_W), F32),
            jax.ShapeDtypeStruct((b, CONV_W - 1, 3 * GROUP_W), F32),
        ],
        scratch_shapes=[pltpu.VMEM((3, tb + 8, GROUP_W), F32), pltpu.VMEM((3, tb, GROUP_W), F32),
                        pltpu.VMEM((N_HEADS, HEAD_W, HEAD_W), F32)],
        compiler_params=_cparams(("parallel", "arbitrary")),
        name="gdn_prompt",
    )(p3, p3, p3, p3, g3, conv_w, alog_row, dt_row, dnw.reshape(1, HEAD_W))


def _gdns_kernel(bq_ref, bk_ref, bv_ref, bg_ref, ab_ref, cw_ref, alog_ref, dt_ref, dnw_ref, c0_ref, s0_ref,
                 *rest, ls, nseq):
    o_ref, sout_ref, cout_ref, xbuf = rest[-4:]
    x_refs = (bq_ref, bk_ref, bv_ref)
    cw = cw_ref[...]
    qkv = []
    for p in range(3):
        for s in range(nseq):
            xbuf[p, s * 16 + 5:s * 16 + 8, :] = c0_ref[s, :, p * GROUP_W:(p + 1) * GROUP_W]
            xbuf[p, s * 16 + 8:s * 16 + 8 + ls, :] = x_refs[p][s * ls:(s + 1) * ls, :]
        qkv.append(jnp.concatenate([_conv_taps(xbuf, p, s * 16, ls, cw) for s in range(nseq)], axis=0))
        for s in range(nseq):
            cout_ref[s, :, p * GROUP_W:(p + 1) * GROUP_W] = xbuf[p, s * 16 + 5 + ls:s * 16 + 8 + ls, :]

    blk, = _gdn_blocks([(qkv, ab_ref[...])], alog_ref[...], dt_ref[...], ls)
    seq_of_row = _div(_iota((2 * CHUNK, 1), 0) & (CHUNK - 1), ls)
    vnews, qss = [], []
    for h in range(N_HEADS):
        wq = jnp.concatenate([blk["w"][h], blk["qg"][h]], axis=0)
        r = jnp.zeros((2 * CHUNK, HEAD_W), F32)
        for s in range(nseq):
            r = r + _dot(jnp.where(seq_of_row == s, wq, 0.0).astype(BF16), s0_ref[s, h].astype(BF16))
        vn = blk["u"][h] - r[0:CHUNK]
        vnews.append(vn)
        qss.append(r[CHUNK:2 * CHUNK])
        vn_pad = _pad_rows(vn).astype(BF16)
        for s in range(nseq):
            kd = jnp.where(seq_of_row[0:CHUNK] == s, blk["kdec"][h], 0.0)
            decay = jnp.exp(blk["gl"][h][s * ls:s * ls + 1, :])
            sout_ref[s, h] = s0_ref[s, h] * decay + _dot(_pad_t(kd).astype(BF16), vn_pad)
    o_ref[...] = _gdn_outputs(blk, vnews, qss, bg_ref[...], dnw_ref[...]).astype(o_ref.dtype)


def _gdn_sample(p2, g2, ls, conv_w, alog_row, dt_row, dnw, conv0, s0, layer, s_all):
    bs = s0.shape[1]
    nseq = CHUNK // ls
    seg = lambda s: pl.BlockSpec((CHUNK, GROUP_W), lambda i: (i, s))
    const = lambda shape: pl.BlockSpec(shape, lambda i: (0,) * len(shape))
    n_in = 11
    extra_in, extra_specs, aliases = (), [], {}
    if s_all is not None:
        extra_in, extra_specs, aliases = (s_all,), [pl.BlockSpec(memory_space=pl.ANY)], {n_in: 1}
    return pl.pallas_call(
        functools.partial(_gdns_kernel, ls=ls, nseq=nseq),
        grid=(bs // nseq,),
        in_specs=[seg(SEG_BQ), seg(SEG_BK), seg(SEG_BV), seg(SEG_BG),
                  pl.BlockSpec((CHUNK, LANES), lambda i: (i, 0)),
                  const((CONV_W, 3 * GROUP_W)), const((1, LANES)), const((1, LANES)), const((1, HEAD_W)),
                  pl.BlockSpec((None, nseq, CONV_W - 1, 3 * GROUP_W), lambda i: (layer, i, 0, 0)),
                  pl.BlockSpec((None, nseq, N_HEADS, HEAD_W, HEAD_W), lambda i: (layer, i, 0, 0, 0))] + extra_specs,
        out_specs=[
            pl.BlockSpec((CHUNK, GROUP_W), lambda i: (i, 0)),
            pl.BlockSpec((None, nseq, N_HEADS, HEAD_W, HEAD_W), lambda i: (layer, i, 0, 0, 0)),
            pl.BlockSpec((nseq, CONV_W - 1, 3 * GROUP_W), lambda i: (i, 0, 0)),
        ],
        out_shape=[
            jax.ShapeDtypeStruct((bs * ls, GROUP_W), F32),
            jax.ShapeDtypeStruct(s0.shape, F32),
            jax.ShapeDtypeStruct(conv0.shape[1:], F32),
        ],
        scratch_shapes=[pltpu.VMEM((3, nseq * 16, GROUP_W), F32)],
        input_output_aliases=aliases,
        compiler_params=_cparams(("parallel",)),
        name="gdn_sample",
    )(p2, p2, p2, p2, g2, conv_w, alog_row, dt_row, dnw.reshape(1, HEAD_W), conv0, s0, *extra_in)


def _outproj_kernel(h_ref, oa_ref, ob_ref, ox_ref, w_ref, nf_ref, o_ref, *, final):
    acc = h_ref[...]
    for n, r in enumerate((oa_ref, ob_ref, ox_ref)):
        acc = acc + _dot(r[...].astype(BF16), w_ref[n * GROUP_W:(n + 1) * GROUP_W, :])
    o_ref[...] = _rms(acc, nf_ref[...]) if final else acc


def _outproj(h2d, oa, ob, ox, w_out, norm_f, final):
    t = h2d.shape[0]
    tm = min(t, 1024)
    row = lambda w: pl.BlockSpec((tm, w), lambda i: (i, 0))
    return pl.pallas_call(
        functools.partial(_outproj_kernel, final=final),
        grid=(t // tm,),
        in_specs=[row(D_MODEL), row(GROUP_W), row(GROUP_W), row(GROUP_W),
                  pl.BlockSpec((3 * GROUP_W, D_MODEL), lambda i: (0, 0)),
                  pl.BlockSpec((1, D_MODEL), lambda i: (0, 0))],
        out_specs=row(D_MODEL),
        out_shape=jax.ShapeDtypeStruct((t, D_MODEL), F32),
        compiler_params=_cparams(("parallel",)),
        name="outproj",
    )(h2d, oa, ob, ox, w_out, norm_f.reshape(1, D_MODEL))


def _pad_lanes(v):
    return jnp.pad(v.astype(F32), (0, LANES - v.shape[0])).reshape(1, LANES)


def kernel(x_prompt, x_sample, cache_k, cache_v, state_delta, state_conv, cache_mem_k, cache_mem_v, page_table, mem_prompt, norm_w, w_in, diff_lambda, diff_norm_w, conv_w, a_log, dt_bias, delta_norm_w, norm_mem, w_mem_kv, w_out, norm_f):
    bp, lp, _ = x_prompt.shape
    bs, ls, _ = x_sample.shape
    depth = w_in.shape[0]
    n_mem = mem_prompt.shape[1]
    n_gate = 2 * N_HEADS
    main_w = N_MAIN - 2 * GROUP_W

    w_ab = w_in[:, :, :main_w].astype(BF16)
    w_x = w_in[:, :, main_w + n_gate:].astype(BF16)
    w_gate = jnp.pad(w_in[:, :, main_w:main_w + n_gate], ((0, 0), (0, 0), (0, LANES - n_gate))).astype(BF16)
    w_out_b = w_out.astype(BF16)
    cache_k4 = cache_k.reshape(cache_k.shape[0], cache_k.shape[1], PAGE * N_HEADS, HEAD_W)
    cache_v4 = cache_v.reshape(cache_v.shape[0], cache_v.shape[1], PAGE * N_HEADS, HEAD_W)
    mem_k4 = cache_mem_k.reshape(depth, bs, n_mem * N_HEADS, HEAD_W)
    mem_v4 = cache_mem_v.reshape(depth, bs, n_mem * N_HEADS, HEAD_W)

    mk_all, mv_all = _memkv(mem_prompt.reshape(bp * n_mem, D_MODEL), norm_mem, w_mem_kv.astype(BF16))

    hp = x_prompt.reshape(bp * lp, D_MODEL)
    hs = x_sample.reshape(bs * ls, D_MODEL)
    tq_x = min(lp, 512)
    sp, cp, cs = ([] for _ in range(3))
    kv_p = kv_s = s_all = None
    for l in range(depth):
        lam_init = 0.8 - 0.6 * math.exp(-0.3 * l)
        alog_row = _pad_lanes(a_log[l])
        dt_row = _pad_lanes(dt_bias[l])
        final = l == depth - 1

        pp, gp, pa16, *kv_p = _proj(hp, norm_w[l], w_ab[l], w_x[l], w_gate[l], l, depth, kv_p)
        ps, gs, sa16, *kv_s = _proj(hs, norm_w[l], w_ab[l], w_x[l], w_gate[l], l, depth, kv_s)
        oa, oa_s = _attn(pa16.reshape(bp, lp, N_A), sa16.reshape(bs, ls, N_A), cache_k4, cache_v4, l, page_table,
                         diff_lambda[l], diff_norm_w[l], lam_init)

        pp3 = pp.reshape(bp, lp, N_P)
        ob, s_new, c_new = _gdn_prompt(pp3, gp.reshape(bp, lp, LANES), conv_w[l], alog_row, dt_row, delta_norm_w[l])
        ox = _cross_prompt(pp.reshape(bp, lp // tq_x, tq_x, N_P), mk_all[l].reshape(bp, n_mem, GROUP_W),
                           mv_all[l].reshape(bp, n_mem, GROUP_W))
        hp = _outproj(hp, oa.reshape(bp * lp, GROUP_W), ob.reshape(bp * lp, GROUP_W),
                      ox.reshape(bp * lp, GROUP_W), w_out_b[l], norm_f, final)
        sp.append(s_new)
        cp.append(c_new)

        ps3 = ps.reshape(bs, ls, N_P)
        ob, s_all, c_new = _gdn_sample(ps, gs, ls, conv_w[l], alog_row, dt_row, delta_norm_w[l],
                                       state_conv, state_delta, l, s_all)
        ox = _cross_sample(ps3, mem_k4, mem_v4, l)
        hs = _outproj(hs, oa_s.reshape(bs * ls, GROUP_W), ob, ox.reshape(bs * ls, GROUP_W), w_out_b[l], norm_f, final)
        cs.append(c_new)

    heads = lambda x, b, n: x.reshape(depth, b, n, N_HEADS, HEAD_W)
    return (hp.reshape(bp, lp, D_MODEL), hs.reshape(bs, ls, D_MODEL),
            heads(kv_p[0], bp, lp), heads(kv_p[1], bp, lp), jnp.stack(sp), jnp.stack(cp),
            heads(mk_all, bp, n_mem), heads(mv_all, bp, n_mem),
            heads(kv_s[0], bs, ls), heads(kv_s[1], bs, ls), s_all, jnp.stack(cs))
```

```python
import functools
import math

import jax
import jax.numpy as jnp
import numpy as np
from jax import lax
from jax.experimental import pallas as pl
from jax.experimental.pallas import tpu as pltpu

F32 = jnp.float32
BF16 = jnp.bfloat16

D_MODEL = 1024
N_HEADS = 4
HEAD_W = 128
GROUP_W = N_HEADS * HEAD_W
A_DH = 64
CONV_W = 4
CHUNK = 64
GDN_GROUP = 4
PAGE = 128
EPS = 1e-6
NEG_INF = -1e30
LOG2E = math.log2(math.e)
N_MAIN = 10 * GROUP_W
N_A = 4 * GROUP_W
N_P = N_MAIN - N_A
LANES = 128
VMEM_LIMIT = 48 * 1024 * 1024

SEG_AQ, SEG_AK, SEG_AV, SEG_AG = range(4)
SEG_BQ, SEG_BK, SEG_BV, SEG_BG, SEG_XQ, SEG_XG = range(6)


def _cparams(sem):
    return pltpu.CompilerParams(dimension_semantics=sem, vmem_limit_bytes=VMEM_LIMIT)


def _iota(shape, dim):
    return lax.broadcasted_iota(jnp.int32, shape, dim)


def _div(x, d):
    assert d & (d - 1) == 0
    return x >> (d.bit_length() - 1)


def _sigmoid(x):
    return 1.0 / (1.0 + jnp.exp(-x))


def _silu(x):
    return x * _sigmoid(x)


def _softplus(x):
    return jnp.maximum(x, 0.0) + jnp.log(1.0 + jnp.exp(-jnp.abs(x)))


def _dot(a, b):
    return jnp.dot(a, b, preferred_element_type=F32)


def _dot_nt(a, b):
    return lax.dot_general(a, b, (((1,), (1,)), ((), ())), preferred_element_type=F32)


def _rms(x, w):
    ms = jnp.mean(x * x, axis=-1, keepdims=True)
    return x * lax.rsqrt(ms + EPS) * w


def _lambda_value(p, lam_init):
    a = jnp.sum(p[0:1, :] * p[1:2, :], axis=-1, keepdims=True)
    b = jnp.sum(p[2:3, :] * p[3:4, :], axis=-1, keepdims=True)
    return jnp.exp(a) - jnp.exp(b) + lam_init


def _proj_kernel(x_ref, nw_ref, w_ref, wx_ref, wg_ref, *rest, tm):
    p_ref, g_ref, a16_ref, k_out, v_out = rest[-5:]
    hn = _rms(x_ref[...], nw_ref[...]).astype(BF16)
    g_ref[...] = _dot(hn, wg_ref[...])

    def scatter_heads(out_ref, cols):
        for h in range(N_HEADS):
            out_ref[pl.ds(h, tm, stride=N_HEADS), :] = cols[:, h * HEAD_W:(h + 1) * HEAD_W]

    n_ab = w_ref.shape[1] // GROUP_W
    for seg in range(N_MAIN // GROUP_W):
        w_seg = (w_ref[:, seg * GROUP_W:(seg + 1) * GROUP_W] if seg < n_ab
                 else wx_ref[:, (seg - n_ab) * GROUP_W:(seg - n_ab + 1) * GROUP_W])
        res = _dot(hn, w_seg)
        if seg < N_A // GROUP_W:
            a16_ref[:, seg * GROUP_W:(seg + 1) * GROUP_W] = res.astype(BF16)
            if seg == SEG_AK:
                scatter_heads(k_out, res)
            if seg == SEG_AV:
                scatter_heads(v_out, res)
        else:
            p_ref[:, seg * GROUP_W - N_A:(seg + 1) * GROUP_W - N_A] = res


def _proj(x2d, norm_w, w_ab, w_x, w_gate, layer, depth, kv_all):
    t = x2d.shape[0]
    tm = min(t, 512)
    kv_spec = pl.BlockSpec((None, N_HEADS * tm, HEAD_W), lambda i: (layer, i, 0))
    kv_shape = jax.ShapeDtypeStruct((depth, N_HEADS * t, HEAD_W), F32)
    n_in = 5
    extra_in, extra_specs, aliases = (), [], {}
    if kv_all is not None:
        extra_in = tuple(kv_all)
        extra_specs = [pl.BlockSpec(memory_space=pl.ANY)] * 2
        aliases = {n_in: 3, n_in + 1: 4}
    resident = dict(pipeline_mode=pl.Buffered(1))
    return pl.pallas_call(
        functools.partial(_proj_kernel, tm=tm),
        grid=(t // tm,),
        in_specs=[
            pl.BlockSpec((tm, D_MODEL), lambda i: (i, 0)),
            pl.BlockSpec((1, D_MODEL), lambda i: (0, 0)),
            pl.BlockSpec((D_MODEL, w_ab.shape[1]), lambda i: (0, 0), **resident),
            pl.BlockSpec((D_MODEL, w_x.shape[1]), lambda i: (0, 0), **resident),
            pl.BlockSpec((D_MODEL, LANES), lambda i: (0, 0), **resident),
        ] + extra_specs,
        out_specs=[
            pl.BlockSpec((tm, N_P), lambda i: (i, 0)),
            pl.BlockSpec((tm, LANES), lambda i: (i, 0)),
            pl.BlockSpec((tm, N_A), lambda i: (i, 0)),
            kv_spec, kv_spec,
        ],
        out_shape=[jax.ShapeDtypeStruct((t, N_P), F32), jax.ShapeDtypeStruct((t, LANES), F32),
                   jax.ShapeDtypeStruct((t, N_A), BF16), kv_shape, kv_shape],
        input_output_aliases=aliases,
        compiler_params=_cparams(("parallel",)),
        name="proj",
    )(x2d, norm_w.reshape(1, D_MODEL), w_ab, w_x, w_gate, *extra_in)


def _memkv_kernel(x_ref, nw_ref, w_ref, k_ref, v_ref):
    hn = _rms(x_ref[...], nw_ref[...]).astype(BF16)
    kv = _dot(hn, w_ref[...])
    k_ref[...] = kv[:, :GROUP_W]
    v_ref[...] = kv[:, GROUP_W:]


def _memkv(mem2d, norm_mem, w_mem_kv_bf16):
    depth = norm_mem.shape[0]
    t = mem2d.shape[0]
    return pl.pallas_call(
        _memkv_kernel,
        grid=(depth,),
        in_specs=[
            pl.BlockSpec((t, D_MODEL), lambda l: (0, 0)),
            pl.BlockSpec((None, 1, D_MODEL), lambda l: (l, 0, 0)),
            pl.BlockSpec((None, D_MODEL, 2 * GROUP_W), lambda l: (l, 0, 0)),
        ],
        out_specs=[
            pl.BlockSpec((None, t, GROUP_W), lambda l: (l, 0, 0)),
            pl.BlockSpec((None, t, GROUP_W), lambda l: (l, 0, 0)),
        ],
        out_shape=[jax.ShapeDtypeStruct((depth, t, GROUP_W), F32)] * 2,
        compiler_params=_cparams(("parallel",)),
        name="memkv",
    )(mem2d, norm_mem.reshape(depth, 1, D_MODEL), w_mem_kv_bf16)


def _alibi_slope(h):
    return jnp.where(h == 0, 0.25, jnp.where(h == 1, 0.0625, jnp.where(h == 2, 0.015625, 0.00390625))).astype(F32)


def _diff_epilogue(o1, o2, lam, dnw, lam_scale, gate):
    o = o1 - lam * o2
    return _rms(o, dnw) * lam_scale * _silu(gate)


def _bf16_split3(x):
    parts = []
    for _ in range(3):
        p = float(np.asarray(x, np.float32).astype(jnp.bfloat16).astype(np.float32))
        parts.append(p)
        x = x - p
    return parts


_ALIBI_PARTS = [_bf16_split3(2.0 ** (-2.0 * (h + 1)) * LOG2E) for h in range(N_HEADS)]
POS_LO = 128
LOOP_TILES = 4


def _by_head(h, values):
    out = jnp.float32(values[-1])
    for idx in range(len(values) - 2, -1, -1):
        out = jnp.where(h == idx, jnp.float32(values[idx]), out)
    return out


def _sample_attention(x_ref, k_refs, v_refs, lam, dnw, lam_scale, ls):
    n_pages = len(k_refs)
    past = n_pages * PAGE
    rows = 2 * ls * N_HEADS
    x = x_ref[...].astype(F32)
    q = x[:, 0:GROUP_W] * (A_DH ** -0.5 * LOG2E)
    k_new = x[:, GROUP_W:2 * GROUP_W]
    v_new = x[:, 2 * GROUP_W:3 * GROUP_W]
    gate = x[:, 3 * GROUP_W:4 * GROUP_W]

    lane_q = _iota((ls, HEAD_W), 1)
    parts = []
    for h in range(N_HEADS):
        qh = q[:, h * HEAD_W:(h + 1) * HEAD_W]
        parts += [jnp.where(lane_q < A_DH, qh, 0.0), jnp.where(lane_q >= A_DH, qh, 0.0)]
    qs = jnp.concatenate(parts, axis=0)
    row_head_w = _div(_iota((rows, GROUP_W), 0), 2 * ls)
    qb = jnp.where(row_head_w == _div(_iota((rows, GROUP_W), 1), HEAD_W),
                   jnp.concatenate([qs] * N_HEADS, axis=1), 0.0).astype(BF16)
    qs = qs.astype(BF16)

    row_head = _div(_iota((rows, 1), 0), 2 * ls)
    slope2 = _alibi_slope(row_head) * LOG2E
    tok = _iota((rows, 1), 0) & (ls - 1)

    col = _iota((1, N_HEADS * PAGE), 1)
    own = (col & (N_HEADS - 1)) == row_head
    key_in_page = _div(col, N_HEADS)
    s_pages = []
    for p in range(n_pages):
        s = _dot_nt(qs, k_refs[p][...].astype(BF16))
        s = s + slope2 * (key_in_page + (p * PAGE - past)).astype(F32)
        s_pages.append(jnp.where(own, s, NEG_INF))
    lane = _iota((1, PAGE), 1)
    pad = jnp.zeros((PAGE - ls, GROUP_W), F32)
    k_pad = jnp.concatenate([k_new, pad], axis=0).astype(BF16)
    v_pad = jnp.concatenate([v_new, pad], axis=0).astype(BF16)
    s_new = _dot_nt(qb, k_pad) + slope2 * lane.astype(F32)
    s_new = jnp.where(lane <= tok, s_new, NEG_INF)

    m = s_new.max(axis=-1, keepdims=True)
    for s in s_pages:
        m = jnp.maximum(m, s.max(axis=-1, keepdims=True))
    e = jnp.exp2(s_new - m)
    l = jnp.sum(e, axis=-1, keepdims=True)
    o_new = _dot(e.astype(BF16), v_pad)
    acc = jnp.zeros((rows, HEAD_W), F32)
    for h in range(N_HEADS):
        acc = acc + jnp.where(row_head == h, o_new[:, h * HEAD_W:(h + 1) * HEAD_W], 0.0)
    for p, s in enumerate(s_pages):
        e = jnp.exp2(s - m)
        l = l + jnp.sum(e, axis=-1, keepdims=True)
        acc = acc + _dot(e.astype(BF16), v_refs[p][...].astype(BF16))
    o = acc / l

    outs = []
    for h in range(N_HEADS):
        o1 = o[2 * ls * h:2 * ls * h + ls]
        o2 = o[2 * ls * h + ls:2 * ls * (h + 1)]
        outs.append(_diff_epilogue(o1, o2, lam, dnw, lam_scale, gate[:, h * HEAD_W:(h + 1) * HEAD_W]))
    return jnp.concatenate(outs, axis=1)


def _attn_kernel(pt_ref, q_ref, k_ref, v_ref, g_ref, lamp_ref, dnw_ref, xs_ref, *rest, tq, n_pages, ls, lam_init):
    del pt_ref
    k_pages = rest[:n_pages]
    v_pages = rest[n_pages:2 * n_pages]
    o_ref, os_ref, kf_ref, qs_ref, s_a, s_b, m_ref, l_ref, acc_ref = rest[2 * n_pages:]
    h = pl.program_id(1)
    i = pl.program_id(2)
    tk = tq
    n_lane_tiles = tk // LANES
    seq = k_ref.shape[0]
    lam = _lambda_value(lamp_ref[...], lam_init)

    @pl.when(i == 0)
    def _():
        def fill(r, carry):
            rows = pl.ds(pl.multiple_of(r * tk, tk), tk)
            k = k_ref[rows, :].astype(F32)
            lane = _iota((tk, HEAD_W), 1)
            pos = r * tk + _iota((tk, HEAD_W), 0)
            hi = _div(pos, POS_LO).astype(F32)
            lo = (pos & (POS_LO - 1)).astype(F32)
            for mp in range(2):
                rel = lane - (A_DH if mp == 0 else 0)
                feat = jnp.where(rel < 3, hi, jnp.where(rel < 6, lo, 0.0))
                own = (lane < A_DH) if mp == 0 else (lane >= A_DH)
                kf_ref[mp, rows, :] = jnp.where(own, k, feat).astype(BF16)
            return carry

        lax.fori_loop(0, seq // tk, fill, 0)

    q = q_ref[...].astype(F32) * (A_DH ** -0.5 * LOG2E)
    lane = _iota((1, HEAD_W), 1)
    c = [_by_head(h, [_ALIBI_PARTS[hh][part] for hh in range(N_HEADS)]) for part in range(3)]
    for mp in range(2):
        rel = lane - (A_DH if mp == 0 else 0)
        qfeat = jnp.zeros((1, HEAD_W), F32)
        for part in range(3):
            qfeat = jnp.where(rel == part, c[part] * POS_LO, jnp.where(rel == 3 + part, c[part], qfeat))
        own = (lane < A_DH) if mp == 0 else (lane >= A_DH)
        qs_ref[mp] = jnp.where(own, q, qfeat).astype(BF16)
    m_ref[...] = jnp.full(m_ref.shape, NEG_INF, F32)
    l_ref[...] = jnp.zeros(l_ref.shape, F32)
    acc_ref[...] = jnp.zeros(acc_ref.shape, F32)

    def scores(mp, j):
        return _dot_nt(qs_ref[mp], kf_ref[mp, pl.ds(pl.multiple_of(j * tk, tk), tk), :])

    def update(mp, s_ref, j, diagonal):
        def tile(cidx):
            s = s_ref[:, cidx * LANES:(cidx + 1) * LANES]
            if diagonal:
                s = jnp.where(cidx * LANES + _iota((1, LANES), 1) <= _iota((tq, 1), 0), s, NEG_INF)
            return s

        mx = tile(0)
        for cidx in range(1, n_lane_tiles):
            mx = jnp.maximum(mx, tile(cidx))
        m_prev = m_ref[mp]
        m_new = jnp.maximum(m_prev, jnp.max(mx, axis=-1, keepdims=True))
        alpha = jnp.exp2(m_prev - m_new)
        lsum = alpha * l_ref[mp]
        ps = []
        for cidx in range(n_lane_tiles):
            p = jnp.exp2(tile(cidx) - m_new)
            lsum = lsum + p
            ps.append(p.astype(BF16))
        v = v_ref[pl.ds(pl.multiple_of(j * tk, tk), tk), :]
        acc_ref[mp] = alpha * acc_ref[mp] + _dot(jnp.concatenate(ps, axis=1), v)
        l_ref[mp] = lsum
        m_ref[mp] = m_new

    s_a[...] = scores(0, 0)
    os_ref[...] = _sample_attention(xs_ref, k_pages, v_pages, lam, dnw_ref[...], 1.0 - lam_init, ls)

    def step(j):
        s_b[...] = scores(1, j)
        update(0, s_a, j, False)
        s_a[...] = scores(0, j + 1)
        update(1, s_b, j, False)

    def quad(jj, carry):
        for u in range(LOOP_TILES):
            step(LOOP_TILES * jj + u)
        return carry

    lax.fori_loop(0, _div(i, LOOP_TILES), quad, 0)
    rem = i & (LOOP_TILES - 1)

    @pl.when(rem >= 2)
    def _():
        step(i - rem)
        step(i - rem + 1)

    @pl.when((rem & 1) == 1)
    def _():
        step(i - 1)

    s_b[...] = scores(1, i)
    update(0, s_a, i, True)
    update(1, s_b, i, True)

    o1 = acc_ref[0] / jnp.sum(l_ref[0], axis=-1, keepdims=True)
    o2 = acc_ref[1] / jnp.sum(l_ref[1], axis=-1, keepdims=True)
    y = _diff_epilogue(o1, o2, lam, dnw_ref[...], 1.0 - lam_init, g_ref[...].astype(F32))
    o_ref[...] = y.astype(o_ref.dtype)


def _attn(a16, xs16, cache_k4, cache_v4, layer, page_table, lam_p, dnw, lam_init):
    b, l, _ = a16.shape
    bs, ls, _ = xs16.shape
    n_pages = page_table.shape[1]
    tq = min(l, 512)
    nq = l // tq
    assert l <= POS_LO * 256 and tq % LANES == 0
    assert b * N_HEADS * nq == bs, "one sample sequence per prompt grid step"

    def step_id(bi, h, i):
        return (bi * N_HEADS + h) * nq + i

    def page_map(p):
        return lambda bi, h, i, pt: (layer, pt[step_id(bi, h, i) * n_pages + p], 0, 0)

    page_specs = lambda: [pl.BlockSpec((None, None, N_HEADS * PAGE, HEAD_W), page_map(p)) for p in range(n_pages)]
    grid_spec = pltpu.PrefetchScalarGridSpec(
        num_scalar_prefetch=1,
        grid=(b, N_HEADS, nq),
        in_specs=[
            pl.BlockSpec((None, tq, HEAD_W), lambda bi, h, i, pt: (bi, i, SEG_AQ * N_HEADS + h)),
            pl.BlockSpec((None, l, HEAD_W), lambda bi, h, i, pt: (bi, 0, SEG_AK * N_HEADS + h)),
            pl.BlockSpec((None, l, HEAD_W), lambda bi, h, i, pt: (bi, 0, SEG_AV * N_HEADS + h)),
            pl.BlockSpec((None, tq, HEAD_W), lambda bi, h, i, pt: (bi, i, SEG_AG * N_HEADS + h)),
            pl.BlockSpec((4, A_DH), lambda bi, h, i, pt: (0, 0)),
            pl.BlockSpec((1, HEAD_W), lambda bi, h, i, pt: (0, 0)),
            pl.BlockSpec((None, ls, N_A), lambda bi, h, i, pt: (step_id(bi, h, i), 0, 0)),
        ] + page_specs() + page_specs(),
        out_specs=[
            pl.BlockSpec((None, tq, HEAD_W), lambda bi, h, i, pt: (bi, i, h)),
            pl.BlockSpec((None, ls, GROUP_W), lambda bi, h, i, pt: (step_id(bi, h, i), 0, 0)),
        ],
        scratch_shapes=[
            pltpu.VMEM((2, l, HEAD_W), BF16),
            pltpu.VMEM((2, tq, HEAD_W), BF16),
            pltpu.VMEM((tq, tq), F32),
            pltpu.VMEM((tq, tq), F32),
            pltpu.VMEM((2, tq, LANES), F32),
            pltpu.VMEM((2, tq, LANES), F32),
            pltpu.VMEM((2, tq, HEAD_W), F32),
        ],
    )
    return pl.pallas_call(
        functools.partial(_attn_kernel, tq=tq, n_pages=n_pages, ls=ls, lam_init=lam_init),
        grid_spec=grid_spec,
        out_shape=[jax.ShapeDtypeStruct((b, l, GROUP_W), BF16), jax.ShapeDtypeStruct((bs, ls, GROUP_W), F32)],
        compiler_params=_cparams(("parallel", "parallel", "arbitrary")),
        name="attn",
    )(page_table.reshape(-1), a16, a16, a16, a16, lam_p, dnw.reshape(1, HEAD_W), xs16,
      *([cache_k4] * n_pages), *([cache_v4] * n_pages))


def _cross_kernel(q_ref, g_ref, mk_ref, mv_ref, o_ref, *, nseq):
    for s in range(nseq):
        q = q_ref[s] * (HEAD_W ** -0.5)
        gate = g_ref[s]
        mk = mk_ref[s].astype(BF16)
        mv = mv_ref[s].astype(BF16)
        outs = []
        for h in range(N_HEADS):
            sl = slice(h * HEAD_W, (h + 1) * HEAD_W)
            sc = _dot_nt(q[:, sl].astype(BF16), mk[:, sl])
            e = jnp.exp(sc - sc.max(axis=-1, keepdims=True))
            o = _dot(e.astype(BF16), mv[:, sl]) / jnp.sum(e, axis=-1, keepdims=True)
            outs.append(o * _silu(gate[:, sl]))
        o_ref[s] = jnp.concatenate(outs, axis=1).astype(o_ref.dtype)


def _cross_prompt(p4, mk, mv):
    nb, nt, tq, _ = p4.shape
    n_mem = mk.shape[1]
    return pl.pallas_call(
        functools.partial(_cross_kernel, nseq=1),
        grid=(nb, nt),
        in_specs=[
            pl.BlockSpec((1, None, tq, GROUP_W), lambda b, i: (b, i, 0, SEG_XQ)),
            pl.BlockSpec((1, None, tq, GROUP_W), lambda b, i: (b, i, 0, SEG_XG)),
            pl.BlockSpec((1, n_mem, GROUP_W), lambda b, i: (b, 0, 0)),
            pl.BlockSpec((1, n_mem, GROUP_W), lambda b, i: (b, 0, 0)),
        ],
        out_specs=pl.BlockSpec((1, None, tq, GROUP_W), lambda b, i: (b, i, 0, 0)),
        out_shape=jax.ShapeDtypeStruct((nb, nt, tq, GROUP_W), BF16),
        compiler_params=_cparams(("parallel", "parallel")),
        name="cross_prompt",
    )(p4, p4, mk, mv)


def _cross_sample_kernel(q_ref, g_ref, mk_ref, mv_ref, o_ref, *, nseq, ls):
    rows = N_HEADS * ls
    n_rows = mk_ref.shape[1]
    own = (_iota((1, n_rows), 1) & (N_HEADS - 1)) == _div(_iota((rows, 1), 0), ls)
    for s in range(nseq):
        q = q_ref[s] * (HEAD_W ** -0.5)
        gate = g_ref[s]
        qx = jnp.concatenate([q[:, h * HEAD_W:(h + 1) * HEAD_W] for h in range(N_HEADS)], axis=0)
        sc = jnp.where(own, _dot_nt(qx.astype(BF16), mk_ref[s].astype(BF16)), NEG_INF)
        e = jnp.exp(sc - sc.max(axis=-1, keepdims=True))
        o = _dot(e.astype(BF16), mv_ref[s].astype(BF16)) / jnp.sum(e, axis=-1, keepdims=True)
        o = jnp.concatenate([o[h * ls:(h + 1) * ls] for h in range(N_HEADS)], axis=1)
        o_ref[s] = o * _silu(gate)


def _cross_sample(p3, mk, mv, layer):
    bs, ls, _ = p3.shape
    n_rows = mk.shape[2]
    nseq = 8
    return pl.pallas_call(
        functools.partial(_cross_sample_kernel, nseq=nseq, ls=ls),
        grid=(bs // nseq,),
        in_specs=[
            pl.BlockSpec((nseq, ls, GROUP_W), lambda b: (b, 0, SEG_XQ)),
            pl.BlockSpec((nseq, ls, GROUP_W), lambda b: (b, 0, SEG_XG)),
            pl.BlockSpec((None, nseq, n_rows, HEAD_W), lambda b: (layer, b, 0, 0)),
            pl.BlockSpec((None, nseq, n_rows, HEAD_W), lambda b: (layer, b, 0, 0)),
        ],
        out_specs=pl.BlockSpec((nseq, ls, GROUP_W), lambda b: (b, 0, 0)),
        out_shape=jax.ShapeDtypeStruct((bs, ls, GROUP_W), F32),
        compiler_params=_cparams(("parallel",)),
        name="cross_sample",
    )(p3, p3, mk, mv)


def _col(x, lane_idx):
    lane = _iota(x.shape, 1)
    return jnp.broadcast_to(jnp.sum(jnp.where(lane == lane_idx, x, 0.0), axis=-1, keepdims=True), x.shape)


def _l2_normalize(x, scale):
    ss = jnp.broadcast_to(jnp.sum(x * x, axis=-1, keepdims=True), x.shape)
    return x * (lax.rsqrt(ss + EPS) * scale)


def _split3_dot(mat_bf16, x):
    hi = x.astype(BF16)
    r1 = x - hi.astype(F32)
    mid = r1.astype(BF16)
    lo = (r1 - mid.astype(F32)).astype(BF16)
    return _dot(mat_bf16, hi) + _dot(mat_bf16, mid) + _dot(mat_bf16, lo)


def _run_interleaved(*gens):
    results = [None] * len(gens)
    live = list(enumerate(gens))
    while live:
        still = []
        for idx, g in live:
            try:
                next(g)
                still.append((idx, g))
            except StopIteration as done:
                results[idx] = done.value
        live = still
    return results


def _gdn_blocks(blocks, alog_row, dt_row, blk):
    return _run_interleaved(_gdn_blocks_staged(blocks, alog_row, dt_row, blk))[0]


def _gdn_blocks_staged(blocks, alog_row, dt_row, blk):
    sh = int(math.log2(blk))
    n_pairs = N_HEADS // 2
    lane = _iota((1, LANES), 1)
    neg_a = jnp.where(lane < N_HEADS, -jnp.exp(alog_row), 0.0)
    ti = _iota((CHUNK, CHUNK), 0)
    tj = _iota((CHUNK, CHUNK), 1)
    same_t = (ti >> sh) == (tj >> sh)
    tri_incl = (same_t & (tj <= ti)).astype(BF16)
    tri_all = same_t.astype(BF16)
    pi = _iota((CHUNK, LANES), 0)
    pj = _iota((CHUNK, LANES), 1) & (CHUNK - 1)
    left = _iota((CHUNK, LANES), 1) < CHUNK
    same = (pi >> sh) == (pj >> sh)
    incl = same & (pj <= pi)
    strict = same & (pj < pi)
    eye = jnp.where(pi == pj, 1.0, 0.0)
    zeros_h = jnp.zeros((CHUNK, HEAD_W), F32)

    def block_diag(m):
        return jnp.concatenate([jnp.where(left, m, 0.0), jnp.where(left, 0.0, m)], axis=0)

    st = []
    for qkv, ab in blocks:
        hsl = [slice(h * HEAD_W, (h + 1) * HEAD_W) for h in range(N_HEADS)]
        q = [qkv[0][:, sl] for sl in hsl]
        k = [qkv[1][:, sl] for sl in hsl]
        v = [qkv[2][:, sl] for sl in hsl]
        qn = [_l2_normalize(x, HEAD_W ** -0.5) for x in q]
        kn = [_l2_normalize(x, 1.0) for x in k]
        g = neg_a * _softplus(ab + dt_row)
        beta_full = _sigmoid(ab)
        gcum = _split3_dot(tri_incl, g)
        gtot = _split3_dot(tri_all, g)
        gc = [_col(gcum, h) for h in range(N_HEADS)]
        gl = [_col(gtot, h) for h in range(N_HEADS)]
        beta = [_col(beta_full, N_HEADS + h) for h in range(N_HEADS)]
        gt = jnp.concatenate([gcum, gcum], axis=0).T
        kb = [kn[h] * beta[h] for h in range(N_HEADS)]
        eg = [jnp.exp(gc[h]) for h in range(N_HEADS)]
        blk_state = dict(gl=gl, qg=[qn[h] * eg[h] for h in range(N_HEADS)],
                         kdec=[kn[h] * jnp.exp(gl[h] - gc[h]) for h in range(N_HEADS)],
                         decay=[], lhs=[], rnt=[], rhs=[])
        for p in range(n_pairs):
            a, b = 2 * p, 2 * p + 1
            gr = jnp.where(lane < CHUNK, gt[a:a + 1, :], gt[b:b + 1, :])
            gcp = jnp.where(left, gc[a], gc[b])
            blk_state["decay"].append(jnp.where(incl, jnp.exp(jnp.where(incl, gcp - gr, 0.0)), 0.0))
            blk_state["lhs"].append(jnp.concatenate(
                [jnp.concatenate([kb[a], kb[b]], axis=1), jnp.concatenate([qn[a], qn[b]], axis=1)],
                axis=0).astype(BF16))
            blk_state["rnt"].append(jnp.concatenate(
                [jnp.concatenate([kn[a], zeros_h], axis=1), jnp.concatenate([zeros_h, kn[b]], axis=1)],
                axis=0).astype(BF16))
            blk_state["rhs"].append(jnp.concatenate(
                [jnp.concatenate([v[a] * beta[a], kb[a] * eg[a]], axis=1),
                 jnp.concatenate([v[b] * beta[b], kb[b] * eg[b]], axis=1)], axis=0).astype(BF16))
        st.append(blk_state)
        yield

    for s in st:
        s["nmat"], s["qkm"] = [], []
        for p in range(n_pairs):
            kq = _dot_nt(s["lhs"][p], s["rnt"][p])
            s["nmat"].append(jnp.where(strict, kq[0:CHUNK] * s["decay"][p], 0.0))
            s["qkm"].append(kq[CHUNK:2 * CHUNK] * s["decay"][p])
    yield

    first = ((pi >> 1) == (pj >> 1)) & ((pi & 1) == 1) & ((pj & 1) == 0)
    for s in st:
        s["x"] = [eye - jnp.where(first, nm, 0.0) for nm in s["nmat"]]
    b = 2
    while b < blk:
        sb = int(math.log2(2 * b))
        mask = ((pi >> sb) == (pj >> sb)) & ((pi & (2 * b - 1)) >= b) & ((pj & (2 * b - 1)) < b)
        for s in st:
            s["xbd"] = [block_diag(x).astype(BF16) for x in s["x"]]
            s["t"] = [_dot(s["x"][p].astype(BF16), block_diag(jnp.where(mask, s["nmat"][p], 0.0)).astype(BF16))
                      for p in range(n_pairs)]
        yield
        for s in st:
            s["x"] = [s["x"][p] - _dot(s["t"][p].astype(BF16), s["xbd"][p]) for p in range(n_pairs)]
        yield
        b *= 2

    outs = []
    for s in st:
        u, w = [], []
        for p in range(n_pairs):
            sol = _dot(block_diag(s["x"][p]).astype(BF16), s["rhs"][p])
            for half in range(2):
                rows = slice(half * CHUNK, (half + 1) * CHUNK)
                u.append(sol[rows, 0:HEAD_W])
                w.append(sol[rows, HEAD_W:2 * HEAD_W])
        outs.append(dict(u=u, w=w, qg=s["qg"], kdec=s["kdec"], gl=s["gl"],
                         qkbd=[block_diag(m).astype(BF16) for m in s["qkm"]]))
    return outs


def _gdn_outputs(pre, vnew, qs, gate, dnw):
    outs = [None] * N_HEADS
    for p in range(N_HEADS // 2):
        a, b = 2 * p, 2 * p + 1
        o = jnp.concatenate([qs[a], qs[b]], axis=0) + _dot(
            pre["qkbd"][p], jnp.concatenate([vnew[a], vnew[b]], axis=0).astype(BF16))
        outs[a], outs[b] = o[0:CHUNK], o[CHUNK:2 * CHUNK]
    return jnp.concatenate([_rms(outs[h], dnw) * _silu(gate[:, h * HEAD_W:(h + 1) * HEAD_W])
                            for h in range(N_HEADS)], axis=1)


def _pad_t(a):
    return jnp.concatenate([a, jnp.zeros((LANES - CHUNK, LANES), F32)], axis=0).T


def _pad_rows(a):
    return jnp.concatenate([a, jnp.zeros((LANES - CHUNK, LANES), F32)], axis=0)


def _conv_taps(xbuf, p, base, nrows, cw):
    cur = xbuf[p, pl.ds(base + 8, nrows), :]
    row = _iota((nrows, GROUP_W), 0)
    acc = cur * cw[CONV_W - 1:CONV_W, p * GROUP_W:(p + 1) * GROUP_W]
    for k in range(1, CONV_W):
        head = xbuf[p, pl.ds(base + 8 - k, 8), :]
        if nrows > 8:
            head = jnp.concatenate([head, jnp.zeros((nrows - 8, GROUP_W), F32)], axis=0)
        shifted = jnp.where(row < k, head, pltpu.roll(cur, k, axis=0))
        acc = acc + shifted * cw[CONV_W - 1 - k:CONV_W - k, p * GROUP_W:(p + 1) * GROUP_W]
    return _silu(acc)


def _gdnp_kernel(bq_ref, bk_ref, bv_ref, bg_ref, ab_ref, cw_ref, alog_ref, dt_ref, dnw_ref,
                 o_ref, sout_ref, cout_ref, xbuf, ybuf, s_scr, *, tb):
    t = pl.program_id(1)

    @pl.when(t == 0)
    def _():
        xbuf[:, 0:8, :] = jnp.zeros((3, 8, GROUP_W), F32)
        s_scr[...] = jnp.zeros(s_scr.shape, F32)

    x_refs = (bq_ref, bk_ref, bv_ref)
    for p in range(3):
        xbuf[p, 8:8 + tb, :] = x_refs[p][...]
    cw = cw_ref[...]
    gate = bg_ref[...]
    ab = ab_ref[...]
    group_rows = GDN_GROUP * CHUNK
    n_groups = tb // group_rows

    def conv_staged(g):
        for p in range(3):
            ybuf[p, g * group_rows:(g + 1) * group_rows, :] = _conv_taps(xbuf, p, g * group_rows, group_rows, cw)
            yield

    def prepass_staged(g):
        rows = [slice(g * group_rows + c * CHUNK, g * group_rows + (c + 1) * CHUNK) for c in range(GDN_GROUP)]
        return _gdn_blocks_staged([([ybuf[p, rs, :] for p in range(3)], ab[rs]) for rs in rows],
                                  alog_ref[...], dt_ref[...], CHUNK)

    def sequential_staged(g, pre):
        for c in range(GDN_GROUP):
            rs = slice(g * group_rows + c * CHUNK, g * group_rows + (c + 1) * CHUNK)
            blk = pre[c]
            vnews, qss = [], []
            for h in range(N_HEADS):
                s_old = s_scr[h]
                wq = jnp.concatenate([blk["w"][h], blk["qg"][h]], axis=0).astype(BF16)
                r = _dot(wq, s_old.astype(BF16))
                vn = blk["u"][h] - r[0:CHUNK]
                vnews.append(vn)
                qss.append(r[CHUNK:2 * CHUNK])
                upd = _dot(_pad_t(blk["kdec"][h]).astype(BF16), _pad_rows(vn).astype(BF16))
                s_scr[h] = s_old * jnp.exp(blk["gl"][h][0:1, :]) + upd
            yield
            o_ref[rs, :] = _gdn_outputs(blk, vnews, qss, gate[rs], dnw_ref[...]).astype(o_ref.dtype)
            yield

    def nothing():
        return
        yield

    _run_interleaved(conv_staged(0))
    pre_prev = None
    for g in range(n_groups):
        conv_next = conv_staged(g + 1) if g + 1 < n_groups else nothing()
        seq_prev = sequential_staged(g - 1, pre_prev) if g > 0 else nothing()
        pre_prev = _run_interleaved(prepass_staged(g), conv_next, seq_prev)[0]
    _run_interleaved(sequential_staged(n_groups - 1, pre_prev))
    for p in range(3):
        xbuf[p, 5:8, :] = xbuf[p, tb + 5:tb + 8, :]

    @pl.when(t == pl.num_programs(1) - 1)
    def _():
        sout_ref[...] = s_scr[...]
        for p in range(3):
            cout_ref[:, p * GROUP_W:(p + 1) * GROUP_W] = xbuf[p, 5:8, :]


def _gdn_prompt(p3, g3, conv_w, alog_row, dt_row, dnw):
    b, l, _ = p3.shape
    tb = min(l, 1024)
    seg = lambda s: pl.BlockSpec((None, tb, GROUP_W), lambda bi, t: (bi, t, s))
    const = lambda shape: pl.BlockSpec(shape, lambda bi, t: (0,) * len(shape))
    return pl.pallas_call(
        functools.partial(_gdnp_kernel, tb=tb),
        grid=(b, l // tb),
        in_specs=[seg(SEG_BQ), seg(SEG_BK), seg(SEG_BV), seg(SEG_BG),
                  pl.BlockSpec((None, tb, LANES), lambda bi, t: (bi, t, 0)),
                  const((CONV_W, 3 * GROUP_W)), const((1, LANES)), const((1, LANES)), const((1, HEAD_W))],
        out_specs=[
            pl.BlockSpec((None, tb, GROUP_W), lambda bi, t: (bi, t, 0)),
            pl.BlockSpec((None, N_HEADS, HEAD_W, HEAD_W), lambda bi, t: (bi, 0, 0, 0)),
            pl.BlockSpec((None, CONV_W - 1, 3 * GROUP_W), lambda bi, t: (bi, 0, 0)),
        ],
        out_shape=[
            jax.ShapeDtypeStruct((b, l, GROUP_W), BF16),
            jax.ShapeDtypeStruct((b, N_HEADS, HEAD_W, HEAD_W), F32),
            jax.ShapeDtypeStruct((b, CONV_W - 1, 3 * GROUP_W), F32),
        ],
        scratch_shapes=[pltpu.VMEM((3, tb + 8, GROUP_W), F32), pltpu.VMEM((3, tb, GROUP_W), F32),
                        pltpu.VMEM((N_HEADS, HEAD_W, HEAD_W), F32)],
        compiler_params=_cparams(("parallel", "arbitrary")),
        name="gdn_prompt",
    )(p3, p3, p3, p3, g3, conv_w, alog_row, dt_row, dnw.reshape(1, HEAD_W))


def _gdns_kernel(bq_ref, bk_ref, bv_ref, bg_ref, ab_ref, cw_ref, alog_ref, dt_ref, dnw_ref, c0_ref, s0_ref,
                 *rest, ls, nseq, nblk):
    o_ref, sout_ref, cout_ref, xbuf = rest[-4:]
    x_refs = (bq_ref, bk_ref, bv_ref)
    cw = cw_ref[...]
    blocks = []
    for blk_i in range(nblk):
        qkv = []
        for p in range(3):
            for s in range(nseq):
                sq = blk_i * nseq + s
                xbuf[p, sq * 16 + 5:sq * 16 + 8, :] = c0_ref[sq, :, p * GROUP_W:(p + 1) * GROUP_W]
                xbuf[p, sq * 16 + 8:sq * 16 + 8 + ls, :] = x_refs[p][sq * ls:(sq + 1) * ls, :]
            qkv.append(jnp.concatenate(
                [_conv_taps(xbuf, p, (blk_i * nseq + s) * 16, ls, cw) for s in range(nseq)], axis=0))
            for s in range(nseq):
                sq = blk_i * nseq + s
                cout_ref[sq, :, p * GROUP_W:(p + 1) * GROUP_W] = xbuf[p, sq * 16 + 5 + ls:sq * 16 + 8 + ls, :]
        blocks.append((qkv, ab_ref[blk_i * CHUNK:(blk_i + 1) * CHUNK, :]))

    pre = _gdn_blocks(blocks, alog_ref[...], dt_ref[...], ls)
    seq_of_row = _div(_iota((2 * CHUNK, 1), 0) & (CHUNK - 1), ls)
    for blk_i, blk in enumerate(pre):
        vnews, qss = [], []
        for h in range(N_HEADS):
            wq = jnp.concatenate([blk["w"][h], blk["qg"][h]], axis=0)
            r = jnp.zeros((2 * CHUNK, HEAD_W), F32)
            for s in range(nseq):
                sq = blk_i * nseq + s
                r = r + _dot(jnp.where(seq_of_row == s, wq, 0.0).astype(BF16), s0_ref[sq, h].astype(BF16))
            vn = blk["u"][h] - r[0:CHUNK]
            vnews.append(vn)
            qss.append(r[CHUNK:2 * CHUNK])
            vn_pad = _pad_rows(vn).astype(BF16)
            for s in range(nseq):
                sq = blk_i * nseq + s
                kd = jnp.where(seq_of_row[0:CHUNK] == s, blk["kdec"][h], 0.0)
                decay = jnp.exp(blk["gl"][h][s * ls:s * ls + 1, :])
                sout_ref[sq, h] = s0_ref[sq, h] * decay + _dot(_pad_t(kd).astype(BF16), vn_pad)
        rows = slice(blk_i * CHUNK, (blk_i + 1) * CHUNK)
        o_ref[rows, :] = _gdn_outputs(blk, vnews, qss, bg_ref[rows, :], dnw_ref[...]).astype(o_ref.dtype)


def _gdn_sample(p2, g2, ls, conv_w, alog_row, dt_row, dnw, conv0, s0, layer, s_all):
    bs = s0.shape[1]
    nseq = CHUNK // ls
    nblk = 2 if bs % (2 * nseq) == 0 else 1
    rows, seqs = nblk * CHUNK, nblk * nseq
    seg = lambda s: pl.BlockSpec((rows, GROUP_W), lambda i: (i, s))
    const = lambda shape: pl.BlockSpec(shape, lambda i: (0,) * len(shape))
    n_in = 11
    extra_in, extra_specs, aliases = (), [], {}
    if s_all is not None:
        extra_in, extra_specs, aliases = (s_all,), [pl.BlockSpec(memory_space=pl.ANY)], {n_in: 1}
    return pl.pallas_call(
        functools.partial(_gdns_kernel, ls=ls, nseq=nseq, nblk=nblk),
        grid=(bs // seqs,),
        in_specs=[seg(SEG_BQ), seg(SEG_BK), seg(SEG_BV), seg(SEG_BG),
                  pl.BlockSpec((rows, LANES), lambda i: (i, 0)),
                  const((CONV_W, 3 * GROUP_W)), const((1, LANES)), const((1, LANES)), const((1, HEAD_W)),
                  pl.BlockSpec((None, seqs, CONV_W - 1, 3 * GROUP_W), lambda i: (layer, i, 0, 0)),
                  pl.BlockSpec((None, seqs, N_HEADS, HEAD_W, HEAD_W), lambda i: (layer, i, 0, 0, 0))] + extra_specs,
        out_specs=[
            pl.BlockSpec((rows, GROUP_W), lambda i: (i, 0)),
            pl.BlockSpec((None, seqs, N_HEADS, HEAD_W, HEAD_W), lambda i: (layer, i, 0, 0, 0)),
            pl.BlockSpec((seqs, CONV_W - 1, 3 * GROUP_W), lambda i: (i, 0, 0)),
        ],
        out_shape=[
            jax.ShapeDtypeStruct((bs * ls, GROUP_W), F32),
            jax.ShapeDtypeStruct(s0.shape, F32),
            jax.ShapeDtypeStruct(conv0.shape[1:], F32),
        ],
        scratch_shapes=[pltpu.VMEM((3, seqs * 16, GROUP_W), F32)],
        input_output_aliases=aliases,
        compiler_params=_cparams(("parallel",)),
        name="gdn_sample",
    )(p2, p2, p2, p2, g2, conv_w, alog_row, dt_row, dnw.reshape(1, HEAD_W), conv0, s0, *extra_in)


def _outproj_kernel(h_ref, oa_ref, ob_ref, ox_ref, w_ref, nf_ref, o_ref, *, final):
    acc = h_ref[...]
    for n, r in enumerate((oa_ref, ob_ref, ox_ref)):
        acc = acc + _dot(r[...].astype(BF16), w_ref[n * GROUP_W:(n + 1) * GROUP_W, :])
    o_ref[...] = _rms(acc, nf_ref[...]) if final else acc


def _outproj(h2d, oa, ob, ox, w_out, norm_f, final):
    t = h2d.shape[0]
    tm = min(t, 1024)
    row = lambda w: pl.BlockSpec((tm, w), lambda i: (i, 0))
    return pl.pallas_call(
        functools.partial(_outproj_kernel, final=final),
        grid=(t // tm,),
        in_specs=[row(D_MODEL), row(GROUP_W), row(GROUP_W), row(GROUP_W),
                  pl.BlockSpec((3 * GROUP_W, D_MODEL), lambda i: (0, 0)),
                  pl.BlockSpec((1, D_MODEL), lambda i: (0, 0))],
        out_specs=row(D_MODEL),
        out_shape=jax.ShapeDtypeStruct((t, D_MODEL), F32),
        compiler_params=_cparams(("parallel",)),
        name="outproj",
    )(h2d, oa, ob, ox, w_out, norm_f.reshape(1, D_MODEL))


def _pad_lanes(v):
    return jnp.pad(v.astype(F32), (0, LANES - v.shape[0])).reshape(1, LANES)


def kernel(x_prompt, x_sample, cache_k, cache_v, state_delta, state_conv, cache_mem_k, cache_mem_v, page_table, mem_prompt, norm_w, w_in, diff_lambda, diff_norm_w, conv_w, a_log, dt_bias, delta_norm_w, norm_mem, w_mem_kv, w_out, norm_f):
    bp, lp, _ = x_prompt.shape
    bs, ls, _ = x_sample.shape
    depth = w_in.shape[0]
    n_mem = mem_prompt.shape[1]
    n_gate = 2 * N_HEADS
    main_w = N_MAIN - 2 * GROUP_W

    w_ab = w_in[:, :, :main_w].astype(BF16)
    w_x = w_in[:, :, main_w + n_gate:].astype(BF16)
    w_gate = jnp.pad(w_in[:, :, main_w:main_w + n_gate], ((0, 0), (0, 0), (0, LANES - n_gate))).astype(BF16)
    w_out_b = w_out.astype(BF16)
    cache_k4 = cache_k.reshape(cache_k.shape[0], cache_k.shape[1], PAGE * N_HEADS, HEAD_W)
    cache_v4 = cache_v.reshape(cache_v.shape[0], cache_v.shape[1], PAGE * N_HEADS, HEAD_W)
    mem_k4 = cache_mem_k.reshape(depth, bs, n_mem * N_HEADS, HEAD_W)
    mem_v4 = cache_mem_v.reshape(depth, bs, n_mem * N_HEADS, HEAD_W)

    mk_all, mv_all = _memkv(mem_prompt.reshape(bp * n_mem, D_MODEL), norm_mem, w_mem_kv.astype(BF16))

    hp = x_prompt.reshape(bp * lp, D_MODEL)
    hs = x_sample.reshape(bs * ls, D_MODEL)
    tq_x = min(lp, 1024)
    sp, cp, cs = ([] for _ in range(3))
    kv_p = kv_s = s_all = None
    for l in range(depth):
        lam_init = 0.8 - 0.6 * math.exp(-0.3 * l)
        alog_row = _pad_lanes(a_log[l])
        dt_row = _pad_lanes(dt_bias[l])
        final = l == depth - 1

        pp, gp, pa16, *kv_p = _proj(hp, norm_w[l], w_ab[l], w_x[l], w_gate[l], l, depth, kv_p)
        ps, gs, sa16, *kv_s = _proj(hs, norm_w[l], w_ab[l], w_x[l], w_gate[l], l, depth, kv_s)
        oa, oa_s = _attn(pa16.reshape(bp, lp, N_A), sa16.reshape(bs, ls, N_A), cache_k4, cache_v4, l, page_table,
                         diff_lambda[l], diff_norm_w[l], lam_init)

        pp3 = pp.reshape(bp, lp, N_P)
        ob, s_new, c_new = _gdn_prompt(pp3, gp.reshape(bp, lp, LANES), conv_w[l], alog_row, dt_row, delta_norm_w[l])
        ox = _cross_prompt(pp.reshape(bp, lp // tq_x, tq_x, N_P), mk_all[l].reshape(bp, n_mem, GROUP_W),
                           mv_all[l].reshape(bp, n_mem, GROUP_W))
        hp = _outproj(hp, oa.reshape(bp * lp, GROUP_W), ob.reshape(bp * lp, GROUP_W),
                      ox.reshape(bp * lp, GROUP_W), w_out_b[l], norm_f, final)
        sp.append(s_new)
        cp.append(c_new)

        ps3 = ps.reshape(bs, ls, N_P)
        ob, s_all, c_new = _gdn_sample(ps, gs, ls, conv_w[l], alog_row, dt_row, delta_norm_w[l],
                                       state_conv, state_delta, l, s_all)
        ox = _cross_sample(ps3, mem_k4, mem_v4, l)
        hs = _outproj(hs, oa_s.reshape(bs * ls, GROUP_W), ob, ox.reshape(bs * ls, GROUP_W), w_out_b[l], norm_f, final)
        cs.append(c_new)

    heads = lambda x, b, n: x.reshape(depth, b, n, N_HEADS, HEAD_W)
    return (hp.reshape(bp, lp, D_MODEL), hs.reshape(bs, ls, D_MODEL),
            heads(kv_p[0], bp, lp), heads(kv_p[1], bp, lp), jnp.stack(sp), jnp.stack(cp),
            heads(mk_all, bp, n_mem), heads(mv_all, bp, n_mem),
            heads(kv_s[0], bs, ls), heads(kv_s[1], bs, ls), s_all, jnp.stack(cs))
```

```python
import functools
import math

import jax
import jax.numpy as jnp
import numpy as np
from jax import lax
from jax.experimental import pallas as pl
from jax.experimental.pallas import tpu as pltpu

F32 = jnp.float32
BF16 = jnp.bfloat16

D_MODEL = 1024
N_HEADS = 4
HEAD_W = 128
GROUP_W = N_HEADS * HEAD_W
A_DH = 64
CONV_W = 4
CHUNK = 64
GDN_GROUP = 4
PAGE = 128
EPS = 1e-6
NEG_INF = -1e30
LOG2E = math.log2(math.e)
N_MAIN = 10 * GROUP_W
N_A = 4 * GROUP_W
N_P = N_MAIN - N_A
LANES = 128
VMEM_LIMIT = 48 * 1024 * 1024

SEG_AQ, SEG_AK, SEG_AV, SEG_AG = range(4)
SEG_BQ, SEG_BK, SEG_BV, SEG_BG, SEG_XQ, SEG_XG = range(6)


def _cparams(sem):
    return pltpu.CompilerParams(dimension_semantics=sem, vmem_limit_bytes=VMEM_LIMIT)


def _iota(shape, dim):
    return lax.broadcasted_iota(jnp.int32, shape, dim)


def _div(x, d):
    assert d & (d - 1) == 0
    return x >> (d.bit_length() - 1)


def _sigmoid(x):
    return 1.0 / (1.0 + jnp.exp(-x))


def _silu(x):
    return x * _sigmoid(x)


def _softplus(x):
    return jnp.maximum(x, 0.0) + jnp.log(1.0 + jnp.exp(-jnp.abs(x)))


def _dot(a, b):
    return jnp.dot(a, b, preferred_element_type=F32)


def _dot_nt(a, b):
    return lax.dot_general(a, b, (((1,), (1,)), ((), ())), preferred_element_type=F32)


def _rms(x, w):
    ms = jnp.mean(x * x, axis=-1, keepdims=True)
    return x * lax.rsqrt(ms + EPS) * w


def _lambda_value(p, lam_init):
    a = jnp.sum(p[0:1, :] * p[1:2, :], axis=-1, keepdims=True)
    b = jnp.sum(p[2:3, :] * p[3:4, :], axis=-1, keepdims=True)
    return jnp.exp(a) - jnp.exp(b) + lam_init


def _proj_kernel(x_ref, nw_ref, w_ref, wx_ref, wg_ref, *rest, tm):
    p_ref, g_ref, a16_ref, k_out, v_out = rest[-5:]
    hn = _rms(x_ref[...], nw_ref[...]).astype(BF16)
    g_ref[...] = _dot(hn, wg_ref[...])

    def scatter_heads(out_ref, cols):
        for h in range(N_HEADS):
            out_ref[pl.ds(h, tm, stride=N_HEADS), :] = cols[:, h * HEAD_W:(h + 1) * HEAD_W]

    n_ab = w_ref.shape[1] // GROUP_W
    for seg in range(N_MAIN // GROUP_W):
        w_seg = (w_ref[:, seg * GROUP_W:(seg + 1) * GROUP_W] if seg < n_ab
                 else wx_ref[:, (seg - n_ab) * GROUP_W:(seg - n_ab + 1) * GROUP_W])
        res = _dot(hn, w_seg)
        if seg < N_A // GROUP_W:
            a16_ref[:, seg * GROUP_W:(seg + 1) * GROUP_W] = res.astype(BF16)
            if seg == SEG_AK:
                scatter_heads(k_out, res)
            if seg == SEG_AV:
                scatter_heads(v_out, res)
        else:
            p_ref[:, seg * GROUP_W - N_A:(seg + 1) * GROUP_W - N_A] = res


def _proj(x2d, norm_w, w_ab, w_x, w_gate, layer, depth, kv_all):
    t = x2d.shape[0]
    tm = min(t, 512)
    kv_spec = pl.BlockSpec((None, N_HEADS * tm, HEAD_W), lambda i: (layer, i, 0))
    kv_shape = jax.ShapeDtypeStruct((depth, N_HEADS * t, HEAD_W), F32)
    n_in = 5
    extra_in, extra_specs, aliases = (), [], {}
    if kv_all is not None:
        extra_in = tuple(kv_all)
        extra_specs = [pl.BlockSpec(memory_space=pl.ANY)] * 2
        aliases = {n_in: 3, n_in + 1: 4}
    resident = dict(pipeline_mode=pl.Buffered(1))
    return pl.pallas_call(
        functools.partial(_proj_kernel, tm=tm),
        grid=(t // tm,),
        in_specs=[
            pl.BlockSpec((tm, D_MODEL), lambda i: (i, 0)),
            pl.BlockSpec((1, D_MODEL), lambda i: (0, 0)),
            pl.BlockSpec((D_MODEL, w_ab.shape[1]), lambda i: (0, 0), **resident),
            pl.BlockSpec((D_MODEL, w_x.shape[1]), lambda i: (0, 0), **resident),
            pl.BlockSpec((D_MODEL, LANES), lambda i: (0, 0), **resident),
        ] + extra_specs,
        out_specs=[
            pl.BlockSpec((tm, N_P), lambda i: (i, 0)),
            pl.BlockSpec((tm, LANES), lambda i: (i, 0)),
            pl.BlockSpec((tm, N_A), lambda i: (i, 0)),
            kv_spec, kv_spec,
        ],
        out_shape=[jax.ShapeDtypeStruct((t, N_P), F32), jax.ShapeDtypeStruct((t, LANES), F32),
                   jax.ShapeDtypeStruct((t, N_A), BF16), kv_shape, kv_shape],
        input_output_aliases=aliases,
        compiler_params=_cparams(("parallel",)),
        name="proj",
    )(x2d, norm_w.reshape(1, D_MODEL), w_ab, w_x, w_gate, *extra_in)


def _memkv_kernel(x_ref, nw_ref, w_ref, k_ref, v_ref):
    hn = _rms(x_ref[...], nw_ref[...]).astype(BF16)
    kv = _dot(hn, w_ref[...])
    k_ref[...] = kv[:, :GROUP_W]
    v_ref[...] = kv[:, GROUP_W:]


def _memkv(mem2d, norm_mem, w_mem_kv_bf16):
    depth = norm_mem.shape[0]
    t = mem2d.shape[0]
    return pl.pallas_call(
        _memkv_kernel,
        grid=(depth,),
        in_specs=[
            pl.BlockSpec((t, D_MODEL), lambda l: (0, 0)),
            pl.BlockSpec((None, 1, D_MODEL), lambda l: (l, 0, 0)),
            pl.BlockSpec((None, D_MODEL, 2 * GROUP_W), lambda l: (l, 0, 0)),
        ],
        out_specs=[
            pl.BlockSpec((None, t, GROUP_W), lambda l: (l, 0, 0)),
            pl.BlockSpec((None, t, GROUP_W), lambda l: (l, 0, 0)),
        ],
        out_shape=[jax.ShapeDtypeStruct((depth, t, GROUP_W), F32)] * 2,
        compiler_params=_cparams(("parallel",)),
        name="memkv",
    )(mem2d, norm_mem.reshape(depth, 1, D_MODEL), w_mem_kv_bf16)


def _alibi_slope(h):
    return jnp.where(h == 0, 0.25, jnp.where(h == 1, 0.0625, jnp.where(h == 2, 0.015625, 0.00390625))).astype(F32)


def _diff_epilogue(o1, o2, lam, dnw, lam_scale, gate):
    o = o1 - lam * o2
    return _rms(o, dnw) * lam_scale * _silu(gate)


def _bf16_split3(x):
    parts = []
    for _ in range(3):
        p = float(np.asarray(x, np.float32).astype(jnp.bfloat16).astype(np.float32))
        parts.append(p)
        x = x - p
    return parts


_ALIBI_PARTS = [_bf16_split3(2.0 ** (-2.0 * (h + 1)) * LOG2E) for h in range(N_HEADS)]
POS_LO = 128
LOOP_TILES = 4


def _by_head(h, values):
    out = jnp.float32(values[-1])
    for idx in range(len(values) - 2, -1, -1):
        out = jnp.where(h == idx, jnp.float32(values[idx]), out)
    return out


def _sample_attention(x_ref, k_refs, v_refs, lam, dnw, lam_scale, ls):
    n_pages = len(k_refs)
    past = n_pages * PAGE
    rows = 2 * ls * N_HEADS
    x = x_ref[...].astype(F32)
    q = x[:, 0:GROUP_W] * (A_DH ** -0.5 * LOG2E)
    k_new = x[:, GROUP_W:2 * GROUP_W]
    v_new = x[:, 2 * GROUP_W:3 * GROUP_W]
    gate = x[:, 3 * GROUP_W:4 * GROUP_W]

    lane_q = _iota((ls, HEAD_W), 1)
    parts = []
    for h in range(N_HEADS):
        qh = q[:, h * HEAD_W:(h + 1) * HEAD_W]
        parts += [jnp.where(lane_q < A_DH, qh, 0.0), jnp.where(lane_q >= A_DH, qh, 0.0)]
    qs = jnp.concatenate(parts, axis=0)
    row_head_w = _div(_iota((rows, GROUP_W), 0), 2 * ls)
    qb = jnp.where(row_head_w == _div(_iota((rows, GROUP_W), 1), HEAD_W),
                   jnp.concatenate([qs] * N_HEADS, axis=1), 0.0).astype(BF16)
    qs = qs.astype(BF16)

    row_head = _div(_iota((rows, 1), 0), 2 * ls)
    slope2 = _alibi_slope(row_head) * LOG2E
    tok = _iota((rows, 1), 0) & (ls - 1)

    col = _iota((1, N_HEADS * PAGE), 1)
    own = (col & (N_HEADS - 1)) == row_head
    key_in_page = _div(col, N_HEADS)
    s_pages = []
    for p in range(n_pages):
        s = _dot_nt(qs, k_refs[p][...].astype(BF16))
        s = s + slope2 * (key_in_page + (p * PAGE - past)).astype(F32)
        s_pages.append(jnp.where(own, s, NEG_INF))
    lane = _iota((1, PAGE), 1)
    pad = jnp.zeros((PAGE - ls, GROUP_W), F32)
    k_pad = jnp.concatenate([k_new, pad], axis=0).astype(BF16)
    v_pad = jnp.concatenate([v_new, pad], axis=0).astype(BF16)
    s_new = _dot_nt(qb, k_pad) + slope2 * lane.astype(F32)
    s_new = jnp.where(lane <= tok, s_new, NEG_INF)

    m = s_new.max(axis=-1, keepdims=True)
    for s in s_pages:
        m = jnp.maximum(m, s.max(axis=-1, keepdims=True))
    e = jnp.exp2(s_new - m)
    l = jnp.sum(e, axis=-1, keepdims=True)
    o_new = _dot(e.astype(BF16), v_pad)
    acc = jnp.zeros((rows, HEAD_W), F32)
    for h in range(N_HEADS):
        acc = acc + jnp.where(row_head == h, o_new[:, h * HEAD_W:(h + 1) * HEAD_W], 0.0)
    for p, s in enumerate(s_pages):
        e = jnp.exp2(s - m)
        l = l + jnp.sum(e, axis=-1, keepdims=True)
        acc = acc + _dot(e.astype(BF16), v_refs[p][...].astype(BF16))
    o = acc / l

    outs = []
    for h in range(N_HEADS):
        o1 = o[2 * ls * h:2 * ls * h + ls]
        o2 = o[2 * ls * h + ls:2 * ls * (h + 1)]
        outs.append(_diff_epilogue(o1, o2, lam, dnw, lam_scale, gate[:, h * HEAD_W:(h + 1) * HEAD_W]))
    return jnp.concatenate(outs, axis=1)


def _attn_kernel(pt_ref, q_ref, k_ref, v_ref, g_ref, lamp_ref, dnw_ref, xs_ref, *rest, tq, n_pages, ls, lam_init):
    del pt_ref
    k_pages = rest[:n_pages]
    v_pages = rest[n_pages:2 * n_pages]
    o_ref, os_ref, kf_ref, qs_ref, s_a, s_b, m_ref, l_ref, acc_ref = rest[2 * n_pages:]
    h = pl.program_id(1)
    i = pl.program_id(2)
    tk = tq
    n_lane_tiles = tk // LANES
    seq = k_ref.shape[0]
    lam = _lambda_value(lamp_ref[...], lam_init)

    @pl.when(i == 0)
    def _():
        def fill(r, carry):
            rows = pl.ds(pl.multiple_of(r * tk, tk), tk)
            k = k_ref[rows, :].astype(F32)
            lane = _iota((tk, HEAD_W), 1)
            pos = r * tk + _iota((tk, HEAD_W), 0)
            hi = _div(pos, POS_LO).astype(F32)
            lo = (pos & (POS_LO - 1)).astype(F32)
            for mp in range(2):
                rel = lane - (A_DH if mp == 0 else 0)
                feat = jnp.where(rel < 3, hi, jnp.where(rel < 6, lo, 0.0))
                own = (lane < A_DH) if mp == 0 else (lane >= A_DH)
                kf_ref[mp, rows, :] = jnp.where(own, k, feat).astype(BF16)
            return carry

        lax.fori_loop(0, seq // tk, fill, 0)

    q = q_ref[...].astype(F32) * (A_DH ** -0.5 * LOG2E)
    lane = _iota((1, HEAD_W), 1)
    c = [_by_head(h, [_ALIBI_PARTS[hh][part] for hh in range(N_HEADS)]) for part in range(3)]
    for mp in range(2):
        rel = lane - (A_DH if mp == 0 else 0)
        qfeat = jnp.zeros((1, HEAD_W), F32)
        for part in range(3):
            qfeat = jnp.where(rel == part, c[part] * POS_LO, jnp.where(rel == 3 + part, c[part], qfeat))
        own = (lane < A_DH) if mp == 0 else (lane >= A_DH)
        qs_ref[mp] = jnp.where(own, q, qfeat).astype(BF16)
    m_ref[...] = jnp.full(m_ref.shape, NEG_INF, F32)
    l_ref[...] = jnp.zeros(l_ref.shape, F32)
    acc_ref[...] = jnp.zeros(acc_ref.shape, F32)

    def scores(mp, j):
        return _dot_nt(qs_ref[mp], kf_ref[mp, pl.ds(pl.multiple_of(j * tk, tk), tk), :])

    def update(mp, s_ref, j, diagonal):
        def tile(cidx):
            s = s_ref[:, cidx * LANES:(cidx + 1) * LANES]
            if diagonal:
                s = jnp.where(cidx * LANES + _iota((1, LANES), 1) <= _iota((tq, 1), 0), s, NEG_INF)
            return s

        mx = tile(0)
        for cidx in range(1, n_lane_tiles):
            mx = jnp.maximum(mx, tile(cidx))
        m_prev = m_ref[mp]
        m_new = jnp.maximum(m_prev, jnp.max(mx, axis=-1, keepdims=True))
        alpha = jnp.exp2(m_prev - m_new)
        lsum = alpha * l_ref[mp]
        ps = []
        for cidx in range(n_lane_tiles):
            p = jnp.exp2(tile(cidx) - m_new)
            lsum = lsum + p
            ps.append(p.astype(BF16))
        v = v_ref[pl.ds(pl.multiple_of(j * tk, tk), tk), :]
        acc_ref[mp] = alpha * acc_ref[mp] + _dot(jnp.concatenate(ps, axis=1), v)
        l_ref[mp] = lsum
        m_ref[mp] = m_new

    s_a[...] = scores(0, 0)
    os_ref[...] = _sample_attention(xs_ref, k_pages, v_pages, lam, dnw_ref[...], 1.0 - lam_init, ls)

    def step(j):
        s_b[...] = scores(1, j)
        update(0, s_a, j, False)
        s_a[...] = scores(0, j + 1)
        update(1, s_b, j, False)

    def quad(jj, carry):
        for u in range(LOOP_TILES):
            step(LOOP_TILES * jj + u)
        return carry

    lax.fori_loop(0, _div(i, LOOP_TILES), quad, 0)
    rem = i & (LOOP_TILES - 1)

    @pl.when(rem >= 2)
    def _():
        step(i - rem)
        step(i - rem + 1)

    @pl.when((rem & 1) == 1)
    def _():
        step(i - 1)

    s_b[...] = scores(1, i)
    update(0, s_a, i, True)
    update(1, s_b, i, True)

    o1 = acc_ref[0] / jnp.sum(l_ref[0], axis=-1, keepdims=True)
    o2 = acc_ref[1] / jnp.sum(l_ref[1], axis=-1, keepdims=True)
    y = _diff_epilogue(o1, o2, lam, dnw_ref[...], 1.0 - lam_init, g_ref[...].astype(F32))
    o_ref[...] = y.astype(o_ref.dtype)


def _attn(a16, xs16, cache_k4, cache_v4, layer, page_table, lam_p, dnw, lam_init):
    b, l, _ = a16.shape
    bs, ls, _ = xs16.shape
    n_pages = page_table.shape[1]
    tq = min(l, 512)
    nq = l // tq
    assert l <= POS_LO * 256 and tq % LANES == 0
    assert b * N_HEADS * nq == bs, "one sample sequence per prompt grid step"

    def step_id(bi, h, i):
        return (bi * N_HEADS + h) * nq + i

    def page_map(p):
        return lambda bi, h, i, pt: (layer, pt[step_id(bi, h, i) * n_pages + p], 0, 0)

    page_specs = lambda: [pl.BlockSpec((None, None, N_HEADS * PAGE, HEAD_W), page_map(p)) for p in range(n_pages)]
    grid_spec = pltpu.PrefetchScalarGridSpec(
        num_scalar_prefetch=1,
        grid=(b, N_HEADS, nq),
        in_specs=[
            pl.BlockSpec((None, tq, HEAD_W), lambda bi, h, i, pt: (bi, i, SEG_AQ * N_HEADS + h)),
            pl.BlockSpec((None, l, HEAD_W), lambda bi, h, i, pt: (bi, 0, SEG_AK * N_HEADS + h)),
            pl.BlockSpec((None, l, HEAD_W), lambda bi, h, i, pt: (bi, 0, SEG_AV * N_HEADS + h)),
            pl.BlockSpec((None, tq, HEAD_W), lambda bi, h, i, pt: (bi, i, SEG_AG * N_HEADS + h)),
            pl.BlockSpec((4, A_DH), lambda bi, h, i, pt: (0, 0)),
            pl.BlockSpec((1, HEAD_W), lambda bi, h, i, pt: (0, 0)),
            pl.BlockSpec((None, ls, N_A), lambda bi, h, i, pt: (step_id(bi, h, i), 0, 0)),
        ] + page_specs() + page_specs(),
        out_specs=[
            pl.BlockSpec((None, tq, HEAD_W), lambda bi, h, i, pt: (bi, i, h)),
            pl.BlockSpec((None, ls, GROUP_W), lambda bi, h, i, pt: (step_id(bi, h, i), 0, 0)),
        ],
        scratch_shapes=[
            pltpu.VMEM((2, l, HEAD_W), BF16),
            pltpu.VMEM((2, tq, HEAD_W), BF16),
            pltpu.VMEM((tq, tq), F32),
            pltpu.VMEM((tq, tq), F32),
            pltpu.VMEM((2, tq, LANES), F32),
            pltpu.VMEM((2, tq, LANES), F32),
            pltpu.VMEM((2, tq, HEAD_W), F32),
        ],
    )
    return pl.pallas_call(
        functools.partial(_attn_kernel, tq=tq, n_pages=n_pages, ls=ls, lam_init=lam_init),
        grid_spec=grid_spec,
        out_shape=[jax.ShapeDtypeStruct((b, l, GROUP_W), BF16), jax.ShapeDtypeStruct((bs, ls, GROUP_W), F32)],
        compiler_params=_cparams(("parallel", "parallel", "arbitrary")),
        name="attn",
    )(page_table.reshape(-1), a16, a16, a16, a16, lam_p, dnw.reshape(1, HEAD_W), xs16,
      *([cache_k4] * n_pages), *([cache_v4] * n_pages))


def _cross_kernel(q_ref, g_ref, mk_ref, mv_ref, o_ref, *, nseq):
    for s in range(nseq):
        q = q_ref[s] * (HEAD_W ** -0.5)
        gate = g_ref[s]
        mk = mk_ref[s].astype(BF16)
        mv = mv_ref[s].astype(BF16)
        outs = []
        for h in range(N_HEADS):
            sl = slice(h * HEAD_W, (h + 1) * HEAD_W)
            sc = _dot_nt(q[:, sl].astype(BF16), mk[:, sl])
            e = jnp.exp(sc - sc.max(axis=-1, keepdims=True))
            o = _dot(e.astype(BF16), mv[:, sl]) / jnp.sum(e, axis=-1, keepdims=True)
            outs.append(o * _silu(gate[:, sl]))
        o_ref[s] = jnp.concatenate(outs, axis=1).astype(o_ref.dtype)


def _cross_prompt(p4, mk, mv):
    nb, nt, tq, _ = p4.shape
    n_mem = mk.shape[1]
    return pl.pallas_call(
        functools.partial(_cross_kernel, nseq=1),
        grid=(nb, nt),
        in_specs=[
            pl.BlockSpec((1, None, tq, GROUP_W), lambda b, i: (b, i, 0, SEG_XQ)),
            pl.BlockSpec((1, None, tq, GROUP_W), lambda b, i: (b, i, 0, SEG_XG)),
            pl.BlockSpec((1, n_mem, GROUP_W), lambda b, i: (b, 0, 0)),
            pl.BlockSpec((1, n_mem, GROUP_W), lambda b, i: (b, 0, 0)),
        ],
        out_specs=pl.BlockSpec((1, None, tq, GROUP_W), lambda b, i: (b, i, 0, 0)),
        out_shape=jax.ShapeDtypeStruct((nb, nt, tq, GROUP_W), BF16),
        compiler_params=_cparams(("parallel", "parallel")),
        name="cross_prompt",
    )(p4, p4, mk, mv)


def _cross_sample_kernel(q_ref, g_ref, mk_ref, mv_ref, o_ref, *, nseq, ls):
    rows = N_HEADS * ls
    n_rows = mk_ref.shape[1]
    own = (_iota((1, n_rows), 1) & (N_HEADS - 1)) == _div(_iota((rows, 1), 0), ls)
    scs, es, os_ = [], [], []
    for s in range(nseq):
        q = q_ref[s] * (HEAD_W ** -0.5)
        qx = jnp.concatenate([q[:, h * HEAD_W:(h + 1) * HEAD_W] for h in range(N_HEADS)], axis=0)
        scs.append(jnp.where(own, _dot_nt(qx.astype(BF16), mk_ref[s].astype(BF16)), NEG_INF))
    for s in range(nseq):
        es.append(jnp.exp(scs[s] - scs[s].max(axis=-1, keepdims=True)))
    for s in range(nseq):
        os_.append(_dot(es[s].astype(BF16), mv_ref[s].astype(BF16)) / jnp.sum(es[s], axis=-1, keepdims=True))
    for s in range(nseq):
        o = jnp.concatenate([os_[s][h * ls:(h + 1) * ls] for h in range(N_HEADS)], axis=1)
        o_ref[s] = o * _silu(g_ref[s])


def _cross_sample(p3, mk, mv, layer):
    bs, ls, _ = p3.shape
    n_rows = mk.shape[2]
    nseq = 8
    return pl.pallas_call(
        functools.partial(_cross_sample_kernel, nseq=nseq, ls=ls),
        grid=(bs // nseq,),
        in_specs=[
            pl.BlockSpec((nseq, ls, GROUP_W), lambda b: (b, 0, SEG_XQ)),
            pl.BlockSpec((nseq, ls, GROUP_W), lambda b: (b, 0, SEG_XG)),
            pl.BlockSpec((None, nseq, n_rows, HEAD_W), lambda b: (layer, b, 0, 0)),
            pl.BlockSpec((None, nseq, n_rows, HEAD_W), lambda b: (layer, b, 0, 0)),
        ],
        out_specs=pl.BlockSpec((nseq, ls, GROUP_W), lambda b: (b, 0, 0)),
        out_shape=jax.ShapeDtypeStruct((bs, ls, GROUP_W), F32),
        compiler_params=_cparams(("parallel",)),
        name="cross_sample",
    )(p3, p3, mk, mv)


def _col(x, lane_idx):
    lane = _iota(x.shape, 1)
    return jnp.broadcast_to(jnp.sum(jnp.where(lane == lane_idx, x, 0.0), axis=-1, keepdims=True), x.shape)


def _l2_normalize(x, scale):
    ss = jnp.broadcast_to(jnp.sum(x * x, axis=-1, keepdims=True), x.shape)
    return x * (lax.rsqrt(ss + EPS) * scale)


def _split3_dot(mat_bf16, x):
    hi = x.astype(BF16)
    r1 = x - hi.astype(F32)
    mid = r1.astype(BF16)
    lo = (r1 - mid.astype(F32)).astype(BF16)
    return _dot(mat_bf16, hi) + _dot(mat_bf16, mid) + _dot(mat_bf16, lo)


def _run_interleaved(*gens):
    results = [None] * len(gens)
    live = list(enumerate(gens))
    while live:
        still = []
        for idx, g in live:
            try:
                next(g)
                still.append((idx, g))
            except StopIteration as done:
                results[idx] = done.value
        live = still
    return results


def _gdn_blocks(blocks, alog_row, dt_row, blk):
    return _run_interleaved(_gdn_blocks_staged(blocks, alog_row, dt_row, blk))[0]


def _gdn_blocks_staged(blocks, alog_row, dt_row, blk):
    sh = int(math.log2(blk))
    n_pairs = N_HEADS // 2
    lane = _iota((1, LANES), 1)
    neg_a = jnp.where(lane < N_HEADS, -jnp.exp(alog_row), 0.0)
    ti = _iota((CHUNK, CHUNK), 0)
    tj = _iota((CHUNK, CHUNK), 1)
    same_t = (ti >> sh) == (tj >> sh)
    tri_incl = (same_t & (tj <= ti)).astype(BF16)
    tri_all = same_t.astype(BF16)
    pi = _iota((CHUNK, LANES), 0)
    pj = _iota((CHUNK, LANES), 1) & (CHUNK - 1)
    left = _iota((CHUNK, LANES), 1) < CHUNK
    same = (pi >> sh) == (pj >> sh)
    incl = same & (pj <= pi)
    strict = same & (pj < pi)
    eye = jnp.where(pi == pj, 1.0, 0.0)
    zeros_h = jnp.zeros((CHUNK, HEAD_W), F32)

    def block_diag(m):
        return jnp.concatenate([jnp.where(left, m, 0.0), jnp.where(left, 0.0, m)], axis=0)

    st = []
    for qkv, ab in blocks:
        hsl = [slice(h * HEAD_W, (h + 1) * HEAD_W) for h in range(N_HEADS)]
        q = [qkv[0][:, sl] for sl in hsl]
        k = [qkv[1][:, sl] for sl in hsl]
        v = [qkv[2][:, sl] for sl in hsl]
        qn = [_l2_normalize(x, HEAD_W ** -0.5) for x in q]
        kn = [_l2_normalize(x, 1.0) for x in k]
        g = neg_a * _softplus(ab + dt_row)
        beta_full = _sigmoid(ab)
        gcum = _split3_dot(tri_incl, g)
        gtot = _split3_dot(tri_all, g)
        gc = [_col(gcum, h) for h in range(N_HEADS)]
        gl = [_col(gtot, h) for h in range(N_HEADS)]
        beta = [_col(beta_full, N_HEADS + h) for h in range(N_HEADS)]
        gt = jnp.concatenate([gcum, gcum], axis=0).T
        kb = [kn[h] * beta[h] for h in range(N_HEADS)]
        eg = [jnp.exp(gc[h]) for h in range(N_HEADS)]
        blk_state = dict(gl=gl, qg=[qn[h] * eg[h] for h in range(N_HEADS)],
                         kdec=[kn[h] * jnp.exp(gl[h] - gc[h]) for h in range(N_HEADS)],
                         decay=[], lhs=[], rnt=[], rhs=[])
        for p in range(n_pairs):
            a, b = 2 * p, 2 * p + 1
            gr = jnp.where(lane < CHUNK, gt[a:a + 1, :], gt[b:b + 1, :])
            gcp = jnp.where(left, gc[a], gc[b])
            blk_state["decay"].append(jnp.where(incl, jnp.exp(jnp.where(incl, gcp - gr, 0.0)), 0.0))
            blk_state["lhs"].append(jnp.concatenate(
                [jnp.concatenate([kb[a], kb[b]], axis=1), jnp.concatenate([qn[a], qn[b]], axis=1)],
                axis=0).astype(BF16))
            blk_state["rnt"].append(jnp.concatenate(
                [jnp.concatenate([kn[a], zeros_h], axis=1), jnp.concatenate([zeros_h, kn[b]], axis=1)],
                axis=0).astype(BF16))
            blk_state["rhs"].append(jnp.concatenate(
                [jnp.concatenate([v[a] * beta[a], kb[a] * eg[a]], axis=1),
                 jnp.concatenate([v[b] * beta[b], kb[b] * eg[b]], axis=1)], axis=0).astype(BF16))
        st.append(blk_state)
        yield

    for s in st:
        s["nmat"], s["qkm"] = [], []
        for p in range(n_pairs):
            kq = _dot_nt(s["lhs"][p], s["rnt"][p])
            s["nmat"].append(jnp.where(strict, kq[0:CHUNK] * s["decay"][p], 0.0))
            s["qkm"].append(kq[CHUNK:2 * CHUNK] * s["decay"][p])
    yield

    first = ((pi >> 1) == (pj >> 1)) & ((pi & 1) == 1) & ((pj & 1) == 0)
    for s in st:
        s["x"] = [eye - jnp.where(first, nm, 0.0) for nm in s["nmat"]]
    b = 2
    while b < blk:
        sb = int(math.log2(2 * b))
        mask = ((pi >> sb) == (pj >> sb)) & ((pi & (2 * b - 1)) >= b) & ((pj & (2 * b - 1)) < b)
        for s in st:
            s["xbd"] = [block_diag(x).astype(BF16) for x in s["x"]]
            s["t"] = [_dot(s["x"][p].astype(BF16), block_diag(jnp.where(mask, s["nmat"][p], 0.0)).astype(BF16))
                      for p in range(n_pairs)]
        yield
        for s in st:
            s["x"] = [s["x"][p] - _dot(s["t"][p].astype(BF16), s["xbd"][p]) for p in range(n_pairs)]
        yield
        b *= 2

    outs = []
    for s in st:
        u, w = [], []
        for p in range(n_pairs):
            sol = _dot(block_diag(s["x"][p]).astype(BF16), s["rhs"][p])
            for half in range(2):
                rows = slice(half * CHUNK, (half + 1) * CHUNK)
                u.append(sol[rows, 0:HEAD_W])
                w.append(sol[rows, HEAD_W:2 * HEAD_W])
        outs.append(dict(u=u, w=w, qg=s["qg"], kdec=s["kdec"], gl=s["gl"],
                         qkbd=[block_diag(m).astype(BF16) for m in s["qkm"]]))
    return outs


def _gdn_outputs(pre, vnew, qs, gate, dnw):
    outs = [None] * N_HEADS
    for p in range(N_HEADS // 2):
        a, b = 2 * p, 2 * p + 1
        o = jnp.concatenate([qs[a], qs[b]], axis=0) + _dot(
            pre["qkbd"][p], jnp.concatenate([vnew[a], vnew[b]], axis=0).astype(BF16))
        outs[a], outs[b] = o[0:CHUNK], o[CHUNK:2 * CHUNK]
    return jnp.concatenate([_rms(outs[h], dnw) * _silu(gate[:, h * HEAD_W:(h + 1) * HEAD_W])
                            for h in range(N_HEADS)], axis=1)


def _pad_t(a):
    return jnp.concatenate([a, jnp.zeros((LANES - CHUNK, LANES), F32)], axis=0).T


def _pad_rows(a):
    return jnp.concatenate([a, jnp.zeros((LANES - CHUNK, LANES), F32)], axis=0)


def _conv_taps(xbuf, p, base, nrows, cw):
    cur = xbuf[p, pl.ds(base + 8, nrows), :]
    row = _iota((nrows, GROUP_W), 0)
    acc = cur * cw[CONV_W - 1:CONV_W, p * GROUP_W:(p + 1) * GROUP_W]
    for k in range(1, CONV_W):
        head = xbuf[p, pl.ds(base + 8 - k, 8), :]
        if nrows > 8:
            head = jnp.concatenate([head, jnp.zeros((nrows - 8, GROUP_W), F32)], axis=0)
        shifted = jnp.where(row < k, head, pltpu.roll(cur, k, axis=0))
        acc = acc + shifted * cw[CONV_W - 1 - k:CONV_W - k, p * GROUP_W:(p + 1) * GROUP_W]
    return _silu(acc)


def _gdnp_kernel(bq_ref, bk_ref, bv_ref, bg_ref, ab_ref, cw_ref, alog_ref, dt_ref, dnw_ref,
                 o_ref, sout_ref, cout_ref, xbuf, ybuf, s_scr, *, tb):
    t = pl.program_id(1)

    @pl.when(t == 0)
    def _():
        xbuf[:, 0:8, :] = jnp.zeros((3, 8, GROUP_W), F32)
        s_scr[...] = jnp.zeros(s_scr.shape, F32)

    x_refs = (bq_ref, bk_ref, bv_ref)
    for p in range(3):
        xbuf[p, 8:8 + tb, :] = x_refs[p][...]
    cw = cw_ref[...]
    gate = bg_ref[...]
    ab = ab_ref[...]
    group_rows = GDN_GROUP * CHUNK
    n_groups = tb // group_rows

    def conv_staged(g):
        for p in range(3):
            ybuf[p, g * group_rows:(g + 1) * group_rows, :] = _conv_taps(xbuf, p, g * group_rows, group_rows, cw)
            yield

    def prepass_staged(g):
        rows = [slice(g * group_rows + c * CHUNK, g * group_rows + (c + 1) * CHUNK) for c in range(GDN_GROUP)]
        return _gdn_blocks_staged([([ybuf[p, rs, :] for p in range(3)], ab[rs]) for rs in rows],
                                  alog_ref[...], dt_ref[...], CHUNK)

    def sequential_staged(g, pre):
        for c in range(GDN_GROUP):
            rs = slice(g * group_rows + c * CHUNK, g * group_rows + (c + 1) * CHUNK)
            blk = pre[c]
            vnews, qss = [], []
            for h in range(N_HEADS):
                s_old = s_scr[h]
                wq = jnp.concatenate([blk["w"][h], blk["qg"][h]], axis=0).astype(BF16)
                r = _dot(wq, s_old.astype(BF16))
                vn = blk["u"][h] - r[0:CHUNK]
                vnews.append(vn)
                qss.append(r[CHUNK:2 * CHUNK])
                upd = _dot(_pad_t(blk["kdec"][h]).astype(BF16), _pad_rows(vn).astype(BF16))
                s_scr[h] = s_old * jnp.exp(blk["gl"][h][0:1, :]) + upd
            yield
            o_ref[rs, :] = _gdn_outputs(blk, vnews, qss, gate[rs], dnw_ref[...]).astype(o_ref.dtype)
            yield

    def nothing():
        return
        yield

    _run_interleaved(conv_staged(0))
    pre_prev = None
    for g in range(n_groups):
        conv_next = conv_staged(g + 1) if g + 1 < n_groups else nothing()
        seq_prev = sequential_staged(g - 1, pre_prev) if g > 0 else nothing()
        pre_prev = _run_interleaved(prepass_staged(g), conv_next, seq_prev)[0]
    _run_interleaved(sequential_staged(n_groups - 1, pre_prev))
    for p in range(3):
        xbuf[p, 5:8, :] = xbuf[p, tb + 5:tb + 8, :]

    @pl.when(t == pl.num_programs(1) - 1)
    def _():
        sout_ref[...] = s_scr[...]
        for p in range(3):
            cout_ref[:, p * GROUP_W:(p + 1) * GROUP_W] = xbuf[p, 5:8, :]


def _gdn_prompt(p3, g3, conv_w, alog_row, dt_row, dnw):
    b, l, _ = p3.shape
    tb = min(l, 1024)
    seg = lambda s: pl.BlockSpec((None, tb, GROUP_W), lambda bi, t: (bi, t, s))
    const = lambda shape: pl.BlockSpec(shape, lambda bi, t: (0,) * len(shape))
    return pl.pallas_call(
        functools.partial(_gdnp_kernel, tb=tb),
        grid=(b, l // tb),
        in_specs=[seg(SEG_BQ), seg(SEG_BK), seg(SEG_BV), seg(SEG_BG),
                  pl.BlockSpec((None, tb, LANES), lambda bi, t: (bi, t, 0)),
                  const((CONV_W, 3 * GROUP_W)), const((1, LANES)), const((1, LANES)), const((1, HEAD_W))],
        out_specs=[
            pl.BlockSpec((None, tb, GROUP_W), lambda bi, t: (bi, t, 0)),
            pl.BlockSpec((None, N_HEADS, HEAD_W, HEAD_W), lambda bi, t: (bi, 0, 0, 0)),
            pl.BlockSpec((None, CONV_W - 1, 3 * GROUP_W), lambda bi, t: (bi, 0, 0)),
        ],
        out_shape=[
            jax.ShapeDtypeStruct((b, l, GROUP_W), BF16),
            jax.ShapeDtypeStruct((b, N_HEADS, HEAD_W, HEAD_W), F32),
            jax.ShapeDtypeStruct((b, CONV_W - 1, 3 * GROUP_W), F32),
        ],
        scratch_shapes=[pltpu.VMEM((3, tb + 8, GROUP_W), F32), pltpu.VMEM((3, tb, GROUP_W), F32),
                        pltpu.VMEM((N_HEADS, HEAD_W, HEAD_W), F32)],
        compiler_params=_cparams(("parallel", "arbitrary")),
        name="gdn_prompt",
    )(p3, p3, p3, p3, g3, conv_w, alog_row, dt_row, dnw.reshape(1, HEAD_W))


def _gdns_kernel(bq_ref, bk_ref, bv_ref, bg_ref, ab_ref, cw_ref, alog_ref, dt_ref, dnw_ref, c0_ref, s0_ref,
                 *rest, ls, nseq, nblk):
    o_ref, sout_ref, cout_ref, xbuf = rest[-4:]
    x_refs = (bq_ref, bk_ref, bv_ref)
    cw = cw_ref[...]
    blocks = []
    for blk_i in range(nblk):
        qkv = []
        for p in range(3):
            for s in range(nseq):
                sq = blk_i * nseq + s
                xbuf[p, sq * 16 + 5:sq * 16 + 8, :] = c0_ref[sq, :, p * GROUP_W:(p + 1) * GROUP_W]
                xbuf[p, sq * 16 + 8:sq * 16 + 8 + ls, :] = x_refs[p][sq * ls:(sq + 1) * ls, :]
            qkv.append(jnp.concatenate(
                [_conv_taps(xbuf, p, (blk_i * nseq + s) * 16, ls, cw) for s in range(nseq)], axis=0))
            for s in range(nseq):
                sq = blk_i * nseq + s
                cout_ref[sq, :, p * GROUP_W:(p + 1) * GROUP_W] = xbuf[p, sq * 16 + 5 + ls:sq * 16 + 8 + ls, :]
        blocks.append((qkv, ab_ref[blk_i * CHUNK:(blk_i + 1) * CHUNK, :]))

    pre = _gdn_blocks(blocks, alog_ref[...], dt_ref[...], ls)
    seq_of_row = _div(_iota((2 * CHUNK, 1), 0) & (CHUNK - 1), ls)
    for blk_i, blk in enumerate(pre):
        vnews, qss = [], []
        for h in range(N_HEADS):
            wq = jnp.concatenate([blk["w"][h], blk["qg"][h]], axis=0)
            r = jnp.zeros((2 * CHUNK, HEAD_W), F32)
            for s in range(nseq):
                sq = blk_i * nseq + s
                r = r + _dot(jnp.where(seq_of_row == s, wq, 0.0).astype(BF16), s0_ref[sq, h].astype(BF16))
            vn = blk["u"][h] - r[0:CHUNK]
            vnews.append(vn)
            qss.append(r[CHUNK:2 * CHUNK])
            vn_pad = _pad_rows(vn).astype(BF16)
            for s in range(nseq):
                sq = blk_i * nseq + s
                kd = jnp.where(seq_of_row[0:CHUNK] == s, blk["kdec"][h], 0.0)
                decay = jnp.exp(blk["gl"][h][s * ls:s * ls + 1, :])
                sout_ref[sq, h] = s0_ref[sq, h] * decay + _dot(_pad_t(kd).astype(BF16), vn_pad)
        rows = slice(blk_i * CHUNK, (blk_i + 1) * CHUNK)
        o_ref[rows, :] = _gdn_outputs(blk, vnews, qss, bg_ref[rows, :], dnw_ref[...]).astype(o_ref.dtype)


def _gdn_sample(p2, g2, ls, conv_w, alog_row, dt_row, dnw, conv0, s0, layer, s_all):
    bs = s0.shape[1]
    nseq = CHUNK // ls
    nblk = 2 if bs % (2 * nseq) == 0 else 1
    rows, seqs = nblk * CHUNK, nblk * nseq
    seg = lambda s: pl.BlockSpec((rows, GROUP_W), lambda i: (i, s))
    const = lambda shape: pl.BlockSpec(shape, lambda i: (0,) * len(shape))
    n_in = 11
    extra_in, extra_specs, aliases = (), [], {}
    if s_all is not None:
        extra_in, extra_specs, aliases = (s_all,), [pl.BlockSpec(memory_space=pl.ANY)], {n_in: 1}
    return pl.pallas_call(
        functools.partial(_gdns_kernel, ls=ls, nseq=nseq, nblk=nblk),
        grid=(bs // seqs,),
        in_specs=[seg(SEG_BQ), seg(SEG_BK), seg(SEG_BV), seg(SEG_BG),
                  pl.BlockSpec((rows, LANES), lambda i: (i, 0)),
                  const((CONV_W, 3 * GROUP_W)), const((1, LANES)), const((1, LANES)), const((1, HEAD_W)),
                  pl.BlockSpec((None, seqs, CONV_W - 1, 3 * GROUP_W), lambda i: (layer, i, 0, 0)),
                  pl.BlockSpec((None, seqs, N_HEADS, HEAD_W, HEAD_W), lambda i: (layer, i, 0, 0, 0))] + extra_specs,
        out_specs=[
            pl.BlockSpec((rows, GROUP_W), lambda i: (i, 0)),
            pl.BlockSpec((None, seqs, N_HEADS, HEAD_W, HEAD_W), lambda i: (layer, i, 0, 0, 0)),
            pl.BlockSpec((seqs, CONV_W - 1, 3 * GROUP_W), lambda i: (i, 0, 0)),
        ],
        out_shape=[
            jax.ShapeDtypeStruct((bs * ls, GROUP_W), F32),
            jax.ShapeDtypeStruct(s0.shape, F32),
            jax.ShapeDtypeStruct(conv0.shape[1:], F32),
        ],
        scratch_shapes=[pltpu.VMEM((3, seqs * 16, GROUP_W), F32)],
        input_output_aliases=aliases,
        compiler_params=_cparams(("parallel",)),
        name="gdn_sample",
    )(p2, p2, p2, p2, g2, conv_w, alog_row, dt_row, dnw.reshape(1, HEAD_W), conv0, s0, *extra_in)


def _outproj_kernel(h_ref, oa_ref, ob_ref, ox_ref, w_ref, nf_ref, o_ref, *, final):
    acc = h_ref[...]
    for n, r in enumerate((oa_ref, ob_ref, ox_ref)):
        acc = acc + _dot(r[...].astype(BF16), w_ref[n * GROUP_W:(n + 1) * GROUP_W, :])
    o_ref[...] = _rms(acc, nf_ref[...]) if final else acc


def _outproj(h2d, oa, ob, ox, w_out, norm_f, final):
    t = h2d.shape[0]
    tm = min(t, 1024)
    row = lambda w: pl.BlockSpec((tm, w), lambda i: (i, 0))
    return pl.pallas_call(
        functools.partial(_outproj_kernel, final=final),
        grid=(t // tm,),
        in_specs=[row(D_MODEL), row(GROUP_W), row(GROUP_W), row(GROUP_W),
                  pl.BlockSpec((3 * GROUP_W, D_MODEL), lambda i: (0, 0)),
                  pl.BlockSpec((1, D_MODEL), lambda i: (0, 0))],
        out_specs=row(D_MODEL),
        out_shape=jax.ShapeDtypeStruct((t, D_MODEL), F32),
        compiler_params=_cparams(("parallel",)),
        name="outproj",
    )(h2d, oa, ob, ox, w_out, norm_f.reshape(1, D_MODEL))


def _pad_lanes(v):
    return jnp.pad(v.astype(F32), (0, LANES - v.shape[0])).reshape(1, LANES)


def kernel(x_prompt, x_sample, cache_k, cache_v, state_delta, state_conv, cache_mem_k, cache_mem_v, page_table, mem_prompt, norm_w, w_in, diff_lambda, diff_norm_w, conv_w, a_log, dt_bias, delta_norm_w, norm_mem, w_mem_kv, w_out, norm_f):
    bp, lp, _ = x_prompt.shape
    bs, ls, _ = x_sample.shape
    depth = w_in.shape[0]
    n_mem = mem_prompt.shape[1]
    n_gate = 2 * N_HEADS
    main_w = N_MAIN - 2 * GROUP_W

    w_ab = w_in[:, :, :main_w].astype(BF16)
    w_x = w_in[:, :, main_w + n_gate:].astype(BF16)
    w_gate = jnp.pad(w_in[:, :, main_w:main_w + n_gate], ((0, 0), (0, 0), (0, LANES - n_gate))).astype(BF16)
    w_out_b = w_out.astype(BF16)
    cache_k4 = cache_k.reshape(cache_k.shape[0], cache_k.shape[1], PAGE * N_HEADS, HEAD_W)
    cache_v4 = cache_v.reshape(cache_v.shape[0], cache_v.shape[1], PAGE * N_HEADS, HEAD_W)
    mem_k4 = cache_mem_k.reshape(depth, bs, n_mem * N_HEADS, HEAD_W)
    mem_v4 = cache_mem_v.reshape(depth, bs, n_mem * N_HEADS, HEAD_W)

    mk_all, mv_all = _memkv(mem_prompt.reshape(bp * n_mem, D_MODEL), norm_mem, w_mem_kv.astype(BF16))

    hp = x_prompt.reshape(bp * lp, D_MODEL)
    hs = x_sample.reshape(bs * ls, D_MODEL)
    tq_x = min(lp, 1024)
    sp, cp, cs = ([] for _ in range(3))
    kv_p = kv_s = s_all = None
    for l in range(depth):
        lam_init = 0.8 - 0.6 * math.exp(-0.3 * l)
        alog_row = _pad_lanes(a_log[l])
        dt_row = _pad_lanes(dt_bias[l])
        final = l == depth - 1

        pp, gp, pa16, *kv_p = _proj(hp, norm_w[l], w_ab[l], w_x[l], w_gate[l], l, depth, kv_p)
        ps, gs, sa16, *kv_s = _proj(hs, norm_w[l], w_ab[l], w_x[l], w_gate[l], l, depth, kv_s)
        oa, oa_s = _attn(pa16.reshape(bp, lp, N_A), sa16.reshape(bs, ls, N_A), cache_k4, cache_v4, l, page_table,
                         diff_lambda[l], diff_norm_w[l], lam_init)

        pp3 = pp.reshape(bp, lp, N_P)
        ob, s_new, c_new = _gdn_prompt(pp3, gp.reshape(bp, lp, LANES), conv_w[l], alog_row, dt_row, delta_norm_w[l])
        ox = _cross_prompt(pp.reshape(bp, lp // tq_x, tq_x, N_P), mk_all[l].reshape(bp, n_mem, GROUP_W),
                           mv_all[l].reshape(bp, n_mem, GROUP_W))
        hp = _outproj(hp, oa.reshape(bp * lp, GROUP_W), ob.reshape(bp * lp, GROUP_W),
                      ox.reshape(bp * lp, GROUP_W), w_out_b[l], norm_f, final)
        sp.append(s_new)
        cp.append(c_new)

        ps3 = ps.reshape(bs, ls, N_P)
        ob, s_all, c_new = _gdn_sample(ps, gs, ls, conv_w[l], alog_row, dt_row, delta_norm_w[l],
                                       state_conv, state_delta, l, s_all)
        ox = _cross_sample(ps3, mem_k4, mem_v4, l)
        hs = _outproj(hs, oa_s.reshape(bs * ls, GROUP_W), ob, ox.reshape(bs * ls, GROUP_W), w_out_b[l], norm_f, final)
        cs.append(c_new)

    heads = lambda x, b, n: x.reshape(depth, b, n, N_HEADS, HEAD_W)
    return (hp.reshape(bp, lp, D_MODEL), hs.reshape(bs, ls, D_MODEL),
            heads(kv_p[0], bp, lp), heads(kv_p[1], bp, lp), jnp.stack(sp), jnp.stack(cp),
            heads(mk_all, bp, n_mem), heads(mv_all, bp, n_mem),
            heads(kv_s[0], bs, ls), heads(kv_s[1], bs, ls), s_all, jnp.stack(cs))
```
